```python
import math
import jax, jax.numpy as jnp
from jax import lax
import numpy as np

D_MODEL = 1024
BATCH = 2
SEQ = 8192
DEPTH = 1

MIX_WIDTH = D_MODEL
RET_WIDTH = MIX_WIDTH // 2
DIFF_WIDTH = MIX_WIDTH - RET_WIDTH
RET_HD = 64
N_RET_HEADS = RET_WIDTH // RET_HD
DIFF_VD = 128
N_DIFF_HEADS = DIFF_WIDTH // DIFF_VD
DIFF_HD = DIFF_VD // 2
RET_CHUNK = 128
Q_BLOCK = 128
ROPE_BASE = 10000.0
D_FF = ((8 * D_MODEL // 3 + 255) // 256) * 256
NORM_EPS = 1e-6
LAYER_INDEX = 1
LAMBDA_INIT = 0.8 - 0.6 * math.exp(-0.3 * (LAYER_INDEX - 1))
LAMBDA_STD = 0.1
IN_SPLITS = [RET_WIDTH, RET_WIDTH, RET_WIDTH, RET_WIDTH,
             N_DIFF_HEADS * 2 * DIFF_HD,
             N_DIFF_HEADS * 2 * DIFF_HD,
             DIFF_WIDTH]
IN_WIDTH = sum(IN_SPLITS)

kernel_name = "hymba_retnet_diffattn_swiglu"


def rms_norm(x, g):
    xf = x.astype(jnp.float32)
    y = xf * lax.rsqrt(jnp.mean(xf * xf, axis=-1, keepdims=True) + NORM_EPS)
    return (y * g.astype(jnp.float32)).astype(x.dtype)


def rotary(x, pos):
    d = x.shape[-1]
    freqs = 1.0 / (ROPE_BASE ** (jnp.arange(0, d, 2, dtype=jnp.float32) / d))
    ang = pos.astype(jnp.float32)[:, None] * freqs[None, :]
    cos, sin = jnp.cos(ang), jnp.sin(ang)
    xf = x.astype(jnp.float32)
    x1, x2 = xf[..., : d // 2], xf[..., d // 2:]
    return jnp.concatenate([x1 * cos - x2 * sin, x1 * sin + x2 * cos], axis=-1)


def retention_chunkwise(q, k, v, log_g):
    B, H, S, dk = q.shape
    dv = v.shape[-1]
    n = S // RET_CHUNK

    def to_chunks(t):
        return t.reshape(B, H, n, RET_CHUNK, t.shape[-1]).transpose(2, 0, 1, 3, 4)

    idx = jnp.arange(RET_CHUNK, dtype=jnp.float32)
    rel = idx[:, None] - idx[None, :]
    decay_in = jnp.where(rel[None] >= 0,
                         jnp.exp(log_g[:, None, None] * jnp.maximum(rel, 0.0)[None]), 0.0)
    q_dec = jnp.exp(log_g[:, None] * (idx[None, :] + 1.0))
    k_dec = jnp.exp(log_g[:, None] * (RET_CHUNK - 1.0 - idx[None, :]))
    chunk_dec = jnp.exp(log_g * RET_CHUNK)

    def step(state, inp):
        qc, kc, vc = inp
        inner = jnp.einsum('bhid,bhjd->bhij', qc, kc) * decay_in[None]
        o = (jnp.einsum('bhij,bhje->bhie', inner, vc)
             + jnp.einsum('bhid,bhde->bhie', qc * q_dec[None, :, :, None], state))
        new_state = (state * chunk_dec[None, :, None, None]
                     + jnp.einsum('bhjd,bhje->bhde', kc * k_dec[None, :, :, None], vc))
        return new_state, o

    state0 = jnp.zeros((B, H, dk, dv), jnp.float32)
    _, o = lax.scan(step, state0, (to_chunks(q), to_chunks(k), to_chunks(v)))
    return o.transpose(1, 2, 0, 3, 4).reshape(B, H, S, dv)


def diff_attention(q, k, v, lam):
    S = q.shape[3]
    nb = S // Q_BLOCK
    scale = 1.0 / math.sqrt(q.shape[-1])
    kpos = jnp.arange(S)
    kf = k.astype(jnp.float32)
    vf = v.astype(jnp.float32)

    def block(i):
        qb = lax.dynamic_slice_in_dim(q, i * Q_BLOCK, Q_BLOCK, axis=3).astype(jnp.float32)
        s = jnp.einsum('bhmqd,bhmkd->bhmqk', qb, kf) * scale
        qpos = i * Q_BLOCK + jnp.arange(Q_BLOCK)
        mask = kpos[None, :] <= qpos[:, None]
        s = jnp.where(mask[None, None, None], s, -1e30)
        p = jax.nn.softmax(s, axis=-1)
        a = p[:, :, 0] - lam * p[:, :, 1]
        return jnp.einsum('bhqk,bhkd->bhqd', a, vf)

    o = lax.map(block, jnp.arange(nb))
    B, H = q.shape[0], q.shape[1]
    return o.transpose(1, 2, 0, 3, 4).reshape(B, H, S, v.shape[-1])


def setup_inputs(seed: int = 0) -> dict:
    key = jax.random.key(seed)
    ks = jax.random.split(key, 16)
    f32 = jnp.float32
    nrm = lambda k, shape, s: jax.random.normal(k, shape, f32) * s
    gain = lambda k, shape: 1.0 + 0.02 * jax.random.normal(k, shape, f32)
    return {
        "x": jax.random.normal(ks[0], (BATCH, SEQ, D_MODEL), f32),
        "norm1_g": gain(ks[1], (DEPTH, D_MODEL)),
        "w_in": nrm(ks[2], (DEPTH, D_MODEL, IN_WIDTH), D_MODEL ** -0.5),
        "ret_norm_g": gain(ks[3], (DEPTH, N_RET_HEADS, RET_HD)),
        "diff_q_norm_g": gain(ks[4], (DEPTH, DIFF_HD)),
        "diff_k_norm_g": gain(ks[5], (DEPTH, DIFF_HD)),
        "lambda_q1": nrm(ks[6], (DEPTH, DIFF_HD), LAMBDA_STD),
        "lambda_k1": nrm(ks[7], (DEPTH, DIFF_HD), LAMBDA_STD),
        "lambda_q2": nrm(ks[8], (DEPTH, DIFF_HD), LAMBDA_STD),
        "lambda_k2": nrm(ks[9], (DEPTH, DIFF_HD), LAMBDA_STD),
        "diff_subln_g": gain(ks[10], (DEPTH, DIFF_VD)),
        "w_out": nrm(ks[11], (DEPTH, MIX_WIDTH, D_MODEL), MIX_WIDTH ** -0.5),
        "norm2_g": gain(ks[12], (DEPTH, D_MODEL)),
        "w_gate": nrm(ks[13], (DEPTH, D_MODEL, D_FF), D_MODEL ** -0.5),
        "w_up": nrm(ks[14], (DEPTH, D_MODEL, D_FF), D_MODEL ** -0.5),
        "w_down": nrm(ks[15], (DEPTH, D_FF, D_MODEL), D_FF ** -0.5),
    }


def reference(x, norm1_g, w_in, ret_norm_g, diff_q_norm_g, diff_k_norm_g,
              lambda_q1, lambda_k1, lambda_q2, lambda_k2, diff_subln_g,
              w_out, norm2_g, w_gate, w_up, w_down):
    B, S, _ = x.shape
    pos = jnp.arange(S)
    log_g = jnp.log(1.0 - 2.0 ** (-5.0 - jnp.arange(N_RET_HEADS, dtype=jnp.float32)))
    split_pts = list(np.cumsum(IN_SPLITS)[:-1])

    for l in range(DEPTH):
        h = rms_norm(x, norm1_g[l])
        proj = jnp.einsum('bsd,de->bse', h, w_in[l])
        rq, rk, rv, rg, dq, dk, dvv = jnp.split(proj, split_pts, axis=-1)

        heads = lambda t, H, d: t.reshape(B, S, H, d).transpose(0, 2, 1, 3)
        q_r = rotary(heads(rq, N_RET_HEADS, RET_HD), pos)
        k_r = rotary(heads(rk, N_RET_HEADS, RET_HD), pos) * (RET_HD ** -0.5)
        v_r = heads(rv, N_RET_HEADS, RET_HD).astype(jnp.float32)
        ret = retention_chunkwise(q_r, k_r, v_r, log_g)
        ret = ret.transpose(0, 2, 1, 3).astype(x.dtype)
        ret = rms_norm(ret, ret_norm_g[l]).reshape(B, S, RET_WIDTH)
        ret = ret * jax.nn.silu(rg)

        q_d = rms_norm(dq.reshape(B, S, N_DIFF_HEADS, 2, DIFF_HD), diff_q_norm_g[l])
        k_d = rms_norm(dk.reshape(B, S, N_DIFF_HEADS, 2, DIFF_HD), diff_k_norm_g[l])
        q_d = q_d.transpose(0, 2, 3, 1, 4)
        k_d = k_d.transpose(0, 2, 3, 1, 4)
        v_d = heads(dvv, N_DIFF_HEADS, DIFF_VD)
        lam = (jnp.exp(jnp.sum(lambda_q1[l].astype(jnp.float32) * lambda_k1[l].astype(jnp.float32)))
               - jnp.exp(jnp.sum(lambda_q2[l].astype(jnp.float32) * lambda_k2[l].astype(jnp.float32)))
               + LAMBDA_INIT)
        dif = diff_attention(q_d, k_d, v_d, lam)
        dif = dif.transpose(0, 2, 1, 3).astype(x.dtype)
        dif = (rms_norm(dif, diff_subln_g[l]) * (1.0 - LAMBDA_INIT)).reshape(B, S, DIFF_WIDTH)

        mix = jnp.concatenate([ret, dif.astype(ret.dtype)], axis=-1)
        x = x + jnp.einsum('bse,ed->bsd', mix, w_out[l]).astype(x.dtype)

        h2 = rms_norm(x, norm2_g[l])
        ff = jax.nn.silu(jnp.einsum('bsd,df->bsf', h2, w_gate[l])) * jnp.einsum('bsd,df->bsf', h2, w_up[l])
        x = x + jnp.einsum('bsf,fd->bsd', ff, w_down[l]).astype(x.dtype)
    return x
```

```python
import functools
import math

import jax
import jax.numpy as jnp
from jax import lax
from jax.experimental import pallas as pl
from jax.experimental.pallas import tpu as pltpu

F32 = jnp.float32
BF16 = jnp.bfloat16

LANES = 128
VMEM_LIMIT_BYTES = 56 * 1024 * 1024

RET_HD = 64
DIFF_HD = 64
DIFF_VD = 2 * DIFF_HD
HEADS_PER_TILE = LANES // RET_HD
RET_CHUNK = 128
ROPE_BASE = 10000.0
NORM_EPS = 1e-6
LAMBDA_INIT = 0.8 - 0.6 * math.exp(-0.3 * 0)
MASK_VALUE = -1e30
LOG2E = 1.0 / math.log(2.0)

ROW_TILE = 512
RET_TILE = 512
ATT_BLOCK = 512


def _dot(a, b):
    return jnp.dot(a, b, preferred_element_type=F32)


def _dot_nt(a, b):
    return lax.dot_general(a, b, (((1,), (1,)), ((), ())), preferred_element_type=F32)


def _dot_tn(a, b):
    return lax.dot_general(a, b, (((0,), (0,)), ((), ())), preferred_element_type=F32)


def _group_sum(sq, group_ones):
    hi = sq.astype(BF16)
    lo = (sq - hi.astype(F32)).astype(BF16)
    return _dot(hi, group_ones) + _dot(lo, group_ones)


def _inproj_kernel(x_ref, g1_ref, w_ref, cos_ref, sin_ref, gq_ref, gk_ref, ones_ref,
                   rq_ref, rk_ref, rv_ref, rg_ref, dqt_ref, dk_ref, dvt_ref, *, width, kv_block):
    x = x_ref[0]
    tm = x.shape[0]
    ms = jnp.mean(x * x, axis=-1, keepdims=True)
    h = (x * lax.rsqrt(ms + NORM_EPS) * g1_ref[...]).astype(BF16)

    def proj(j):
        return _dot(h, w_ref[:, j * width:(j + 1) * width])

    n_tiles = width // LANES
    cos = cos_ref[...]
    sin = sin_ref[...]
    lane = lax.broadcasted_iota(jnp.int32, (tm, LANES), 1)
    first_half = (lane % RET_HD) < (RET_HD // 2)

    def rotary(t):
        partner = jnp.where(first_half, pltpu.roll(t, LANES - RET_HD // 2, 1),
                            pltpu.roll(t, RET_HD // 2, 1))
        return t * cos + partner * sin

    rq = proj(0)
    for t in range(n_tiles):
        sl = slice(t * LANES, (t + 1) * LANES)
        rq_ref[0, :, sl] = rotary(rq[:, sl]).astype(BF16)
    rk = proj(1)
    for t in range(n_tiles):
        sl = slice(t * LANES, (t + 1) * LANES)
        rk_ref[0, :, sl] = (rotary(rk[:, sl]) * (RET_HD ** -0.5)).astype(BF16)
    rv_ref[0] = proj(2).astype(BF16)
    rg_ref[0] = proj(3).astype(BF16)

    ones = ones_ref[...]

    def qk_norm(t, g):
        ssq = _group_sum(t * t, ones)
        return t * lax.rsqrt(ssq * (1.0 / DIFF_HD) + NORM_EPS) * g

    dq = proj(4)
    q_scale = (DIFF_HD ** -0.5) * LOG2E
    for t in range(n_tiles):
        sl = slice(t * LANES, (t + 1) * LANES)
        qn = qk_norm(dq[:, sl], gq_ref[...]) * q_scale
        dqt_ref[0, sl, :] = qn.T.astype(BF16)
    dk = proj(5)
    for t in range(n_tiles):
        sl = slice(t * LANES, (t + 1) * LANES)
        dk_ref[0, :, sl] = qk_norm(dk[:, sl], gk_ref[...]).astype(BF16)
    dv = proj(6)
    for t in range(n_tiles):
        sl = slice(t * LANES, (t + 1) * LANES)
        vt = dv[:, sl].T.astype(BF16)
        for j in range(tm // kv_block):
            dvt_ref[0, j, sl, :] = vt[:, j * kv_block:(j + 1) * kv_block]


def _retention_kernel(q_ref, k_ref, v_ref, gate_ref, dmat_ref, qdec_ref, kdec_ref, cdec_ref,
                      bdmask_ref, ones_ref, gain_ref, o_ref, state_ref, *, n_tiles, n_chunks):
    @pl.when(pl.program_id(1) == 0)
    def _():
        state_ref[...] = jnp.zeros_like(state_ref)

    c = RET_CHUNK
    lane = lax.broadcasted_iota(jnp.int32, (c, LANES), 1)
    head_a = lane < RET_HD
    bdmask = bdmask_ref[...]
    ones = ones_ref[...]

    for ci in range(n_chunks):
        rows = slice(ci * c, (ci + 1) * c)
        for p in range(n_tiles):
            sl = slice(p * LANES, (p + 1) * LANES)
            q2 = q_ref[0, rows, sl]
            k2 = k_ref[0, rows, sl]
            v2 = v_ref[0, rows, sl]
            zero = jnp.zeros_like(q2)
            qs = jnp.concatenate([jnp.where(head_a, q2, zero), jnp.where(head_a, zero, q2)], axis=0)
            scores = _dot_nt(qs, k2)
            inner = (scores * dmat_ref[p]).astype(BF16)
            intra2 = _dot(inner, v2)
            intra = jnp.where(head_a, intra2[:c], intra2[c:])
            state = state_ref[p]
            qd = (q2.astype(F32) * qdec_ref[:, sl]).astype(BF16)
            o = intra + _dot(qd, state.astype(BF16))
            kd = (k2.astype(F32) * kdec_ref[:, sl]).astype(BF16)
            state_ref[p] = state * cdec_ref[:, sl] + _dot_tn(kd, v2) * bdmask

            ssq = _group_sum(o * o, ones)
            y = o * lax.rsqrt(ssq * (1.0 / RET_HD) + NORM_EPS) * gain_ref[:, sl]
            g = gate_ref[0, rows, sl].astype(F32)
            o_ref[0, rows, sl] = (y * (g / (1.0 + jnp.exp(-g)))).astype(BF16)


def _diffattn_kernel(qt_ref, k_ref, vt_ref, lq1_ref, lk1_ref, lq2_ref, lk2_ref, gain_ref,
                     o_ref, m_ref, l_ref, acc_ref, *, block):
    qi = pl.program_id(2)
    qt = qt_ref[0]
    row = lax.broadcasted_iota(jnp.int32, qt.shape, 0)
    zero = jnp.zeros_like(qt)
    q_maps = (jnp.where(row < DIFF_HD, qt, zero), jnp.where(row < DIFF_HD, zero, qt))

    m_ref[...] = jnp.full_like(m_ref, MASK_VALUE)
    l_ref[...] = jnp.zeros_like(l_ref)
    acc_ref[...] = jnp.zeros_like(acc_ref)

    def step(kj, masked):
        start = pl.multiple_of(kj * block, block)
        k = k_ref[0, pl.ds(start, block), :]
        vt = vt_ref[0, kj]
        if masked:
            kpos = lax.broadcasted_iota(jnp.int32, (block, block), 0)
            qpos = lax.broadcasted_iota(jnp.int32, (block, block), 1)
            causal = kpos <= qpos
        for mi in range(2):
            s = _dot(k, q_maps[mi])
            if masked:
                s = jnp.where(causal, s, MASK_VALUE)
            m_old = m_ref[mi]
            m_new = jnp.maximum(m_old, jnp.max(s, axis=0, keepdims=True))
            alpha = jnp.exp2(m_old - m_new)
            p = jnp.exp2(s - m_new)
            l_ref[mi] = alpha * l_ref[mi] + jnp.sum(p, axis=0, keepdims=True)
            acc_ref[mi] = alpha * acc_ref[mi] + _dot(vt, p.astype(BF16))
            m_ref[mi] = m_new

    def body(kj, carry):
        step(kj, masked=False)
        return carry

    lax.fori_loop(0, qi, body, 0)
    step(qi, masked=True)

    lam = (jnp.exp(jnp.sum(lq1_ref[...] * lk1_ref[...], axis=-1, keepdims=True))
           - jnp.exp(jnp.sum(lq2_ref[...] * lk2_ref[...], axis=-1, keepdims=True))
           + LAMBDA_INIT)
    o = acc_ref[0] / l_ref[0] - lam * (acc_ref[1] / l_ref[1])
    ms = jnp.mean(o * o, axis=0, keepdims=True)
    o = o * lax.rsqrt(ms + NORM_EPS)
    o_ref[0] = (o.T * gain_ref[...] * (1.0 - LAMBDA_INIT)).astype(BF16)


def _out_ffn_kernel(x_ref, ret_ref, dif_ref, wo_ref, g2_ref, wg_ref, wu_ref, wd_ref, o_ref, *,
                    ret_width, ff_chunks):
    x1 = (x_ref[0] + _dot(ret_ref[0], wo_ref[:ret_width, :])
          + _dot(dif_ref[0], wo_ref[ret_width:, :]))
    ms = jnp.mean(x1 * x1, axis=-1, keepdims=True)
    h = (x1 * lax.rsqrt(ms + NORM_EPS) * g2_ref[...]).astype(BF16)
    o_ref[0] = x1
    ff = None
    for lo, hi in ff_chunks:
        g = _dot(h, wg_ref[:, lo:hi])
        u = _dot(h, wu_ref[:, lo:hi])
        a = (g / (1.0 + jnp.exp(-g)) * u).astype(BF16)
        d = _dot(a, wd_ref[lo:hi, :])
        ff = d if ff is None else ff + d
    o_ref[0] += ff


def _resident(shape):
    return pl.BlockSpec(shape, lambda *_: (0,) * len(shape), pipeline_mode=pl.Buffered(1))


def _ff_chunks(d_ff, mxu_cols=256, target=1536):
    chunks, lo = [], 0
    while lo < d_ff:
        hi = min(d_ff, lo + target)
        if d_ff - hi and (d_ff - hi) < mxu_cols:
            hi = d_ff
        chunks.append((lo, hi))
        lo = hi
    return tuple(chunks)


def kernel(x, norm1_g, w_in, ret_norm_g, diff_q_norm_g, diff_k_norm_g, lambda_q1, lambda_k1,
           lambda_q2, lambda_k2, diff_subln_g, w_out, norm2_g, w_gate, w_up, w_down):
    B, S, D = x.shape
    depth = w_in.shape[0]
    n_ret_heads = ret_norm_g.shape[1]
    ret_width = n_ret_heads * RET_HD
    in_width = w_in.shape[2]
    width = ret_width
    assert in_width == 7 * width and w_out.shape[1] == 2 * width
    assert width % LANES == 0 and S % ROW_TILE == 0 and S % RET_TILE == 0 and S % ATT_BLOCK == 0
    assert ROW_TILE % ATT_BLOCK == 0 and RET_TILE % RET_CHUNK == 0
    n_diff_heads = width // DIFF_VD
    n_tiles = width // LANES
    d_ff = w_gate.shape[2]
    nk = S // ATT_BLOCK
    params = functools.partial(pltpu.CompilerParams, vmem_limit_bytes=VMEM_LIMIT_BYTES)

    pos = jnp.arange(S, dtype=F32)
    freqs = 1.0 / (ROPE_BASE ** (jnp.arange(0, RET_HD, 2, dtype=F32) / RET_HD))
    ang = pos[:, None] * freqs[None, :]
    cos_t = jnp.tile(jnp.cos(ang), (1, LANES // (RET_HD // 2)))
    sin_half = jnp.sin(ang)
    sin_t = jnp.tile(jnp.concatenate([-sin_half, sin_half], axis=-1), (1, HEADS_PER_TILE))
    grp = jnp.arange(LANES) // RET_HD
    group_ones = (grp[:, None] == grp[None, :]).astype(BF16)
    bdmask = group_ones.astype(F32)

    log_g = jnp.log(1.0 - 2.0 ** (-5.0 - jnp.arange(n_ret_heads, dtype=F32)))
    idx = jnp.arange(RET_CHUNK, dtype=F32)
    rel = idx[:, None] - idx[None, :]
    decay_in = jnp.where(rel[None] >= 0,
                         jnp.exp(log_g[:, None, None] * jnp.maximum(rel, 0.0)[None]), 0.0)
    dmat = decay_in.reshape(n_tiles, HEADS_PER_TILE * RET_CHUNK, RET_CHUNK)
    by_lane = lambda t: jnp.repeat(t, RET_HD, axis=0).T
    qdec = by_lane(jnp.exp(log_g[:, None] * (idx[None, :] + 1.0)))
    kdec = by_lane(jnp.exp(log_g[:, None] * (RET_CHUNK - 1.0 - idx[None, :])))
    cdec = by_lane(jnp.exp(log_g * RET_CHUNK)[:, None])

    for l in range(depth):
        row = lambda b, i: (b, i, 0)
        col = lambda b, i: (b, 0, i)
        tab = lambda b, i: (i, 0)
        act = jax.ShapeDtypeStruct((B, S, width), BF16)
        act_t = jax.ShapeDtypeStruct((B, width, S), BF16)
        rq, rk, rv, rg, dqt, dk, dvt = pl.pallas_call(
            functools.partial(_inproj_kernel, width=width, kv_block=ATT_BLOCK),
            grid=(B, S // ROW_TILE),
            in_specs=[
                pl.BlockSpec((1, ROW_TILE, D), row),
                _resident((1, D)),
                _resident((D, in_width)),
                pl.BlockSpec((ROW_TILE, LANES), tab),
                pl.BlockSpec((ROW_TILE, LANES), tab),
                _resident((1, LANES)),
                _resident((1, LANES)),
                _resident((LANES, LANES)),
            ],
            out_specs=[
                pl.BlockSpec((1, ROW_TILE, width), row),
                pl.BlockSpec((1, ROW_TILE, width), row),
                pl.BlockSpec((1, ROW_TILE, width), row),
                pl.BlockSpec((1, ROW_TILE, width), row),
                pl.BlockSpec((1, width, ROW_TILE), col),
                pl.BlockSpec((1, ROW_TILE, width), row),
                pl.BlockSpec((1, ROW_TILE // ATT_BLOCK, width, ATT_BLOCK), lambda b, i: (b, i, 0, 0)),
            ],
            out_shape=[act, act, act, act, act_t, act,
                       jax.ShapeDtypeStruct((B, nk, width, ATT_BLOCK), BF16)],
            compiler_params=params(dimension_semantics=("parallel", "parallel")),
            name="inproj",
        )(x, norm1_g[l][None, :], w_in[l].astype(BF16), cos_t, sin_t,
          jnp.tile(diff_q_norm_g[l], LANES // DIFF_HD)[None, :],
          jnp.tile(diff_k_norm_g[l], LANES // DIFF_HD)[None, :], group_ones)

        seq = pl.BlockSpec((1, RET_TILE, width), row)
        ret = pl.pallas_call(
            functools.partial(_retention_kernel, n_tiles=n_tiles, n_chunks=RET_TILE // RET_CHUNK),
            grid=(B, S // RET_TILE),
            in_specs=[seq, seq, seq, seq,
                      _resident(dmat.shape), _resident(qdec.shape), _resident(kdec.shape),
                      _resident(cdec.shape), _resident(bdmask.shape), _resident(group_ones.shape),
                      _resident((1, width))],
            out_specs=seq,
            out_shape=act,
            scratch_shapes=[pltpu.VMEM((n_tiles, LANES, LANES), F32)],
            compiler_params=params(dimension_semantics=("arbitrary", "arbitrary")),
            name="retention",
        )(rq, rk, rv, rg, dmat, qdec, kdec, cdec, bdmask, group_ones,
          ret_norm_g[l].reshape(1, width))

        lam_spec = _resident((1, DIFF_HD))
        dif = pl.pallas_call(
            functools.partial(_diffattn_kernel, block=ATT_BLOCK),
            grid=(B, n_diff_heads, S // ATT_BLOCK),
            in_specs=[
                pl.BlockSpec((1, DIFF_VD, ATT_BLOCK), lambda b, h, i: (b, h, i)),
                pl.BlockSpec((1, S, DIFF_VD), lambda b, h, i: (b, 0, h)),
                pl.BlockSpec((1, nk, DIFF_VD, ATT_BLOCK), lambda b, h, i: (b, 0, h, 0)),
                lam_spec, lam_spec, lam_spec, lam_spec,
                _resident((1, DIFF_VD)),
            ],
            out_specs=pl.BlockSpec((1, ATT_BLOCK, DIFF_VD), lambda b, h, i: (b, i, h)),
            out_shape=act,
            scratch_shapes=[pltpu.VMEM((2, 1, ATT_BLOCK), F32),
                            pltpu.VMEM((2, 1, ATT_BLOCK), F32),
                            pltpu.VMEM((2, DIFF_VD, ATT_BLOCK), F32)],
            compiler_params=params(dimension_semantics=("parallel", "parallel", "arbitrary")),
            name="diffattn",
        )(dqt, dk, dvt, lambda_q1[l][None, :], lambda_k1[l][None, :], lambda_q2[l][None, :],
          lambda_k2[l][None, :], diff_subln_g[l][None, :])

        tile = pl.BlockSpec((1, ROW_TILE, D), row)
        half = pl.BlockSpec((1, ROW_TILE, width), row)
        x = pl.pallas_call(
            functools.partial(_out_ffn_kernel, ret_width=width, ff_chunks=_ff_chunks(d_ff)),
            grid=(B, S // ROW_TILE),
            in_specs=[tile, half, half,
                      _resident((2 * width, D)), _resident((1, D)),
                      _resident((D, d_ff)), _resident((D, d_ff)), _resident((d_ff, D))],
            out_specs=tile,
            out_shape=jax.ShapeDtypeStruct((B, S, D), F32),
            compiler_params=params(dimension_semantics=("parallel", "parallel")),
            name="out_ffn",
        )(x, ret, dif, w_out[l].astype(BF16), norm2_g[l][None, :],
          w_gate[l].astype(BF16), w_up[l].astype(BF16), w_down[l].astype(BF16))
    return x
```

```python
import functools
import math

import jax
import jax.numpy as jnp
from jax import lax
from jax.experimental import pallas as pl
from jax.experimental.pallas import tpu as pltpu

F32 = jnp.float32
BF16 = jnp.bfloat16

LANES = 128
VMEM_LIMIT_BYTES = 56 * 1024 * 1024

RET_HD = 64
DIFF_HD = 64
DIFF_VD = 2 * DIFF_HD
HEADS_PER_TILE = LANES // RET_HD
RET_CHUNK = 128
ROPE_BASE = 10000.0
NORM_EPS = 1e-6
LAMBDA_INIT = 0.8 - 0.6 * math.exp(-0.3 * 0)
MASK_VALUE = -1e30
LOG2E = 1.0 / math.log(2.0)

ROW_TILE = 512
RET_TILE = 512
ATT_BLOCK = 512
ATT_STRIP = 256


def _dot(a, b):
    return jnp.dot(a, b, preferred_element_type=F32)


def _dot_nt(a, b):
    return lax.dot_general(a, b, (((1,), (1,)), ((), ())), preferred_element_type=F32)


def _dot_tn(a, b):
    return lax.dot_general(a, b, (((0,), (0,)), ((), ())), preferred_element_type=F32)


def _group_sum(sq, group_ones):
    hi = sq.astype(BF16)
    lo = (sq - hi.astype(F32)).astype(BF16)
    return _dot(hi, group_ones) + _dot(lo, group_ones)


def _inproj_kernel(x_ref, g1_ref, w_ref, cos_ref, sin_ref, gq_ref, gk_ref, ones_ref,
                   rq_ref, rk_ref, rv_ref, rg_ref, dqt_ref, dk_ref, dvt_ref, *, width, kv_block):
    x = x_ref[0]
    tm = x.shape[0]
    ms = jnp.mean(x * x, axis=-1, keepdims=True)
    h = (x * lax.rsqrt(ms + NORM_EPS) * g1_ref[...]).astype(BF16)

    def proj(j):
        return _dot(h, w_ref[:, j * width:(j + 1) * width])

    n_tiles = width // LANES
    cos = cos_ref[...]
    sin = sin_ref[...]
    lane = lax.broadcasted_iota(jnp.int32, (tm, LANES), 1)
    first_half = (lane % RET_HD) < (RET_HD // 2)

    def rotary(t):
        partner = jnp.where(first_half, pltpu.roll(t, LANES - RET_HD // 2, 1),
                            pltpu.roll(t, RET_HD // 2, 1))
        return t * cos + partner * sin

    rq = proj(0)
    for t in range(n_tiles):
        sl = slice(t * LANES, (t + 1) * LANES)
        rq_ref[0, :, sl] = rotary(rq[:, sl]).astype(BF16)
    rk = proj(1)
    for t in range(n_tiles):
        sl = slice(t * LANES, (t + 1) * LANES)
        rk_ref[0, :, sl] = (rotary(rk[:, sl]) * (RET_HD ** -0.5)).astype(BF16)
    rv_ref[0] = proj(2).astype(BF16)
    rg_ref[0] = proj(3).astype(BF16)

    ones = ones_ref[...]

    def qk_norm(t, g):
        ssq = _group_sum(t * t, ones)
        return t * lax.rsqrt(ssq * (1.0 / DIFF_HD) + NORM_EPS) * g

    dq = proj(4)
    q_scale = (DIFF_HD ** -0.5) * LOG2E
    for t in range(n_tiles):
        sl = slice(t * LANES, (t + 1) * LANES)
        qn = qk_norm(dq[:, sl], gq_ref[...]) * q_scale
        dqt_ref[0, sl, :] = qn.T.astype(BF16)
    dk = proj(5)
    for t in range(n_tiles):
        sl = slice(t * LANES, (t + 1) * LANES)
        dk_ref[0, :, sl] = qk_norm(dk[:, sl], gk_ref[...]).astype(BF16)
    dv = proj(6)
    for t in range(n_tiles):
        sl = slice(t * LANES, (t + 1) * LANES)
        vt = dv[:, sl].T.astype(BF16)
        for j in range(tm // kv_block):
            dvt_ref[0, j, sl, :] = vt[:, j * kv_block:(j + 1) * kv_block]


def _retention_kernel(q_ref, k_ref, v_ref, gate_ref, dmat_ref, qdec_ref, kdec_ref, cdec_ref,
                      bdmask_ref, ones_ref, gain_ref, o_ref, state_ref, *, n_tiles, n_chunks):
    @pl.when(pl.program_id(1) == 0)
    def _():
        state_ref[...] = jnp.zeros_like(state_ref)

    c = RET_CHUNK
    lane = lax.broadcasted_iota(jnp.int32, (c, LANES), 1)
    head_a = lane < RET_HD
    bdmask = bdmask_ref[...]
    ones = ones_ref[...]

    for ci in range(n_chunks):
        rows = slice(ci * c, (ci + 1) * c)
        for p in range(n_tiles):
            sl = slice(p * LANES, (p + 1) * LANES)
            q2 = q_ref[0, rows, sl]
            k2 = k_ref[0, rows, sl]
            v2 = v_ref[0, rows, sl]
            zero = jnp.zeros_like(q2)
            qs = jnp.concatenate([jnp.where(head_a, q2, zero), jnp.where(head_a, zero, q2)], axis=0)
            scores = _dot_nt(qs, k2)
            inner = (scores * dmat_ref[p]).astype(BF16)
            intra2 = _dot(inner, v2)
            intra = jnp.where(head_a, intra2[:c], intra2[c:])
            state = state_ref[p]
            qd = (q2.astype(F32) * qdec_ref[:, sl]).astype(BF16)
            o = intra + _dot(qd, state.astype(BF16))
            kd = (k2.astype(F32) * kdec_ref[:, sl]).astype(BF16)
            state_ref[p] = state * cdec_ref[:, sl] + _dot_tn(kd, v2) * bdmask

            ssq = _group_sum(o * o, ones)
            y = o * lax.rsqrt(ssq * (1.0 / RET_HD) + NORM_EPS) * gain_ref[:, sl]
            g = gate_ref[0, rows, sl].astype(F32)
            o_ref[0, rows, sl] = (y * (g / (1.0 + jnp.exp(-g)))).astype(BF16)


def _diffattn_kernel(qt_ref, k_ref, vt_ref, lq1_ref, lk1_ref, lq2_ref, lk2_ref, gain_ref,
                     o_ref, s_ref, m_ref, l_ref, acc_ref, *, block, strip):
    qi = pl.program_id(2)
    chains = [(mi, t) for mi in range(2) for t in range(block // strip)]
    qt = qt_ref[0]
    row = lax.broadcasted_iota(jnp.int32, qt.shape, 0)
    zero = jnp.zeros_like(qt)
    q_maps = (jnp.where(row < DIFF_HD, qt, zero), jnp.where(row < DIFF_HD, zero, qt))

    m_ref[...] = jnp.full_like(m_ref, MASK_VALUE)
    l_ref[...] = jnp.zeros_like(l_ref)
    acc_ref[...] = jnp.zeros_like(acc_ref)

    def scores(kj, c):
        mi, t = chains[c]
        start = pl.multiple_of(kj * block, block)
        k = k_ref[0, pl.ds(start, block), :]
        s_ref[c] = _dot(k, q_maps[mi][:, t * strip:(t + 1) * strip])

    def softmax_pv(kj, c, masked):
        mi, t = chains[c]
        cols = slice(t * strip, (t + 1) * strip)
        s = s_ref[c]
        if masked:
            kpos = lax.broadcasted_iota(jnp.int32, (block, strip), 0)
            qpos = lax.broadcasted_iota(jnp.int32, (block, strip), 1) + t * strip
            s = jnp.where(kpos <= qpos, s, MASK_VALUE)
        m_old = m_ref[mi, :, cols]
        m_new = jnp.maximum(m_old, jnp.max(s, axis=0, keepdims=True))
        alpha = jnp.exp2(m_old - m_new)
        p = jnp.exp2(s - m_new)
        l_ref[mi, :, cols] = alpha * l_ref[mi, :, cols] + jnp.sum(p, axis=0, keepdims=True)
        acc_ref[mi, :, cols] = alpha * acc_ref[mi, :, cols] + _dot(vt_ref[0, kj], p.astype(BF16))
        m_ref[mi, :, cols] = m_new

    for c in range(len(chains)):
        scores(0, c)

    def body(kj, carry):
        for c in range(len(chains)):
            softmax_pv(kj, c, masked=False)
            scores(kj + 1, c)
        return carry

    lax.fori_loop(0, qi, body, 0)
    for c in range(len(chains)):
        softmax_pv(qi, c, masked=True)

    lam = (jnp.exp(jnp.sum(lq1_ref[...] * lk1_ref[...], axis=-1, keepdims=True))
           - jnp.exp(jnp.sum(lq2_ref[...] * lk2_ref[...], axis=-1, keepdims=True))
           + LAMBDA_INIT)
    o = acc_ref[0] / l_ref[0] - lam * (acc_ref[1] / l_ref[1])
    ms = jnp.mean(o * o, axis=0, keepdims=True)
    o = o * lax.rsqrt(ms + NORM_EPS)
    o_ref[0] = (o.T * gain_ref[...] * (1.0 - LAMBDA_INIT)).astype(BF16)


def _out_ffn_kernel(x_ref, ret_ref, dif_ref, wo_ref, g2_ref, wg_ref, wu_ref, wd_ref, o_ref, *,
                    ret_width, ff_chunks):
    x1 = (x_ref[0] + _dot(ret_ref[0], wo_ref[:ret_width, :])
          + _dot(dif_ref[0], wo_ref[ret_width:, :]))
    ms = jnp.mean(x1 * x1, axis=-1, keepdims=True)
    h = (x1 * lax.rsqrt(ms + NORM_EPS) * g2_ref[...]).astype(BF16)
    o_ref[0] = x1
    ff = None
    for lo, hi in ff_chunks:
        g = _dot(h, wg_ref[:, lo:hi])
        u = _dot(h, wu_ref[:, lo:hi])
        a = (g / (1.0 + jnp.exp(-g)) * u).astype(BF16)
        d = _dot(a, wd_ref[lo:hi, :])
        ff = d if ff is None else ff + d
    o_ref[0] += ff


def _resident(shape):
    return pl.BlockSpec(shape, lambda *_: (0,) * len(shape), pipeline_mode=pl.Buffered(1))


def _ff_chunks(d_ff, mxu_cols=256, target=1536):
    chunks, lo = [], 0
    while lo < d_ff:
        hi = min(d_ff, lo + target)
        if d_ff - hi and (d_ff - hi) < mxu_cols:
            hi = d_ff
        chunks.append((lo, hi))
        lo = hi
    return tuple(chunks)


def kernel(x, norm1_g, w_in, ret_norm_g, diff_q_norm_g, diff_k_norm_g, lambda_q1, lambda_k1,
           lambda_q2, lambda_k2, diff_subln_g, w_out, norm2_g, w_gate, w_up, w_down):
    B, S, D = x.shape
    depth = w_in.shape[0]
    n_ret_heads = ret_norm_g.shape[1]
    ret_width = n_ret_heads * RET_HD
    in_width = w_in.shape[2]
    width = ret_width
    assert in_width == 7 * width and w_out.shape[1] == 2 * width
    assert width % LANES == 0 and S % ROW_TILE == 0 and S % RET_TILE == 0 and S % ATT_BLOCK == 0
    assert ROW_TILE % ATT_BLOCK == 0 and RET_TILE % RET_CHUNK == 0
    n_diff_heads = width // DIFF_VD
    n_tiles = width // LANES
    d_ff = w_gate.shape[2]
    nk = S // ATT_BLOCK
    params = functools.partial(pltpu.CompilerParams, vmem_limit_bytes=VMEM_LIMIT_BYTES)

    pos = jnp.arange(S, dtype=F32)
    freqs = 1.0 / (ROPE_BASE ** (jnp.arange(0, RET_HD, 2, dtype=F32) / RET_HD))
    ang = pos[:, None] * freqs[None, :]
    cos_t = jnp.tile(jnp.cos(ang), (1, LANES // (RET_HD // 2)))
    sin_half = jnp.sin(ang)
    sin_t = jnp.tile(jnp.concatenate([-sin_half, sin_half], axis=-1), (1, HEADS_PER_TILE))
    grp = jnp.arange(LANES) // RET_HD
    group_ones = (grp[:, None] == grp[None, :]).astype(BF16)
    bdmask = group_ones.astype(F32)

    log_g = jnp.log(1.0 - 2.0 ** (-5.0 - jnp.arange(n_ret_heads, dtype=F32)))
    idx = jnp.arange(RET_CHUNK, dtype=F32)
    rel = idx[:, None] - idx[None, :]
    decay_in = jnp.where(rel[None] >= 0,
                         jnp.exp(log_g[:, None, None] * jnp.maximum(rel, 0.0)[None]), 0.0)
    dmat = decay_in.reshape(n_tiles, HEADS_PER_TILE * RET_CHUNK, RET_CHUNK)
    by_lane = lambda t: jnp.repeat(t, RET_HD, axis=0).T
    qdec = by_lane(jnp.exp(log_g[:, None] * (idx[None, :] + 1.0)))
    kdec = by_lane(jnp.exp(log_g[:, None] * (RET_CHUNK - 1.0 - idx[None, :])))
    cdec = by_lane(jnp.exp(log_g * RET_CHUNK)[:, None])

    for l in range(depth):
        row = lambda b, i: (b, i, 0)
        col = lambda b, i: (b, 0, i)
        tab = lambda b, i: (i, 0)
        act = jax.ShapeDtypeStruct((B, S, width), BF16)
        act_t = jax.ShapeDtypeStruct((B, width, S), BF16)
        rq, rk, rv, rg, dqt, dk, dvt = pl.pallas_call(
            functools.partial(_inproj_kernel, width=width, kv_block=ATT_BLOCK),
            grid=(B, S // ROW_TILE),
            in_specs=[
                pl.BlockSpec((1, ROW_TILE, D), row),
                _resident((1, D)),
                _resident((D, in_width)),
                pl.BlockSpec((ROW_TILE, LANES), tab),
                pl.BlockSpec((ROW_TILE, LANES), tab),
                _resident((1, LANES)),
                _resident((1, LANES)),
                _resident((LANES, LANES)),
            ],
            out_specs=[
                pl.BlockSpec((1, ROW_TILE, width), row),
                pl.BlockSpec((1, ROW_TILE, width), row),
                pl.BlockSpec((1, ROW_TILE, width), row),
                pl.BlockSpec((1, ROW_TILE, width), row),
                pl.BlockSpec((1, width, ROW_TILE), col),
                pl.BlockSpec((1, ROW_TILE, width), row),
                pl.BlockSpec((1, ROW_TILE // ATT_BLOCK, width, ATT_BLOCK), lambda b, i: (b, i, 0, 0)),
            ],
            out_shape=[act, act, act, act, act_t, act,
                       jax.ShapeDtypeStruct((B, nk, width, ATT_BLOCK), BF16)],
            compiler_params=params(dimension_semantics=("parallel", "parallel")),
            name="inproj",
        )(x, norm1_g[l][None, :], w_in[l].astype(BF16), cos_t, sin_t,
          jnp.tile(diff_q_norm_g[l], LANES // DIFF_HD)[None, :],
          jnp.tile(diff_k_norm_g[l], LANES // DIFF_HD)[None, :], group_ones)

        seq = pl.BlockSpec((1, RET_TILE, width), row)
        ret = pl.pallas_call(
            functools.partial(_retention_kernel, n_tiles=n_tiles, n_chunks=RET_TILE // RET_CHUNK),
            grid=(B, S // RET_TILE),
            in_specs=[seq, seq, seq, seq,
                      _resident(dmat.shape), _resident(qdec.shape), _resident(kdec.shape),
                      _resident(cdec.shape), _resident(bdmask.shape), _resident(group_ones.shape),
                      _resident((1, width))],
            out_specs=seq,
            out_shape=act,
            scratch_shapes=[pltpu.VMEM((n_tiles, LANES, LANES), F32)],
            compiler_params=params(dimension_semantics=("arbitrary", "arbitrary")),
            name="retention",
        )(rq, rk, rv, rg, dmat, qdec, kdec, cdec, bdmask, group_ones,
          ret_norm_g[l].reshape(1, width))

        lam_spec = _resident((1, DIFF_HD))
        dif = pl.pallas_call(
            functools.partial(_diffattn_kernel, block=ATT_BLOCK, strip=ATT_STRIP),
            grid=(B, n_diff_heads, S // ATT_BLOCK),
            in_specs=[
                pl.BlockSpec((1, DIFF_VD, ATT_BLOCK), lambda b, h, i: (b, h, i)),
                pl.BlockSpec((1, S, DIFF_VD), lambda b, h, i: (b, 0, h)),
                pl.BlockSpec((1, nk, DIFF_VD, ATT_BLOCK), lambda b, h, i: (b, 0, h, 0)),
                lam_spec, lam_spec, lam_spec, lam_spec,
                _resident((1, DIFF_VD)),
            ],
            out_specs=pl.BlockSpec((1, ATT_BLOCK, DIFF_VD), lambda b, h, i: (b, i, h)),
            out_shape=act,
            scratch_shapes=[pltpu.VMEM((2 * ATT_BLOCK // ATT_STRIP, ATT_BLOCK, ATT_STRIP), F32),
                            pltpu.VMEM((2, 1, ATT_BLOCK), F32),
                            pltpu.VMEM((2, 1, ATT_BLOCK), F32),
                            pltpu.VMEM((2, DIFF_VD, ATT_BLOCK), F32)],
            compiler_params=params(dimension_semantics=("parallel", "parallel", "arbitrary")),
            name="diffattn",
        )(dqt, dk, dvt, lambda_q1[l][None, :], lambda_k1[l][None, :], lambda_q2[l][None, :],
          lambda_k2[l][None, :], diff_subln_g[l][None, :])

        tile = pl.BlockSpec((1, ROW_TILE, D), row)
        half = pl.BlockSpec((1, ROW_TILE, width), row)
        x = pl.pallas_call(
            functools.partial(_out_ffn_kernel, ret_width=width, ff_chunks=_ff_chunks(d_ff)),
            grid=(B, S // ROW_TILE),
            in_specs=[tile, half, half,
                      _resident((2 * width, D)), _resident((1, D)),
                      _resident((D, d_ff)), _resident((D, d_ff)), _resident((d_ff, D))],
            out_specs=tile,
            out_shape=jax.ShapeDtypeStruct((B, S, D), F32),
            compiler_params=params(dimension_semantics=("parallel", "parallel")),
            name="out_ffn",
        )(x, ret, dif, w_out[l].astype(BF16), norm2_g[l][None, :],
          w_gate[l].astype(BF16), w_up[l].astype(BF16), w_down[l].astype(BF16))
    return x
```

```python
import functools
import math

import jax
import jax.numpy as jnp
from jax import lax
from jax.experimental import pallas as pl
from jax.experimental.pallas import tpu as pltpu

F32 = jnp.float32
BF16 = jnp.bfloat16

LANES = 128
VMEM_LIMIT_BYTES = 56 * 1024 * 1024

RET_HD = 64
DIFF_HD = 64
DIFF_VD = 2 * DIFF_HD
HEADS_PER_TILE = LANES // RET_HD
RET_CHUNK = 128
ROPE_BASE = 10000.0
NORM_EPS = 1e-6
LAMBDA_INIT = 0.8 - 0.6 * math.exp(-0.3 * 0)
MASK_VALUE = -1e30
LOG2E = 1.0 / math.log(2.0)

ROW_TILE = 512
RET_TILE = 512
ATT_QBLOCK = 1024
ATT_BLOCK = 512
ATT_STRIP = 256


def _dot(a, b):
    return jnp.dot(a, b, preferred_element_type=F32)


def _dot_nt(a, b):
    return lax.dot_general(a, b, (((1,), (1,)), ((), ())), preferred_element_type=F32)


def _dot_tn(a, b):
    return lax.dot_general(a, b, (((0,), (0,)), ((), ())), preferred_element_type=F32)


def _group_sum(sq, group_ones):
    hi = sq.astype(BF16)
    lo = (sq - hi.astype(F32)).astype(BF16)
    return _dot(hi, group_ones) + _dot(lo, group_ones)


def _inproj_kernel(x_ref, g1_ref, w_ref, cos_ref, sin_ref, gq_ref, gk_ref, ones_ref,
                   rq_ref, rk_ref, rv_ref, rg_ref, dqt_ref, dk_ref, dvt_ref, *, width, kv_block):
    x = x_ref[0]
    tm = x.shape[0]
    ms = jnp.mean(x * x, axis=-1, keepdims=True)
    h = (x * lax.rsqrt(ms + NORM_EPS) * g1_ref[...]).astype(BF16)

    def proj(j):
        return _dot(h, w_ref[:, j * width:(j + 1) * width])

    n_tiles = width // LANES
    cos = cos_ref[...]
    sin = sin_ref[...]
    lane = lax.broadcasted_iota(jnp.int32, (tm, LANES), 1)
    first_half = (lane % RET_HD) < (RET_HD // 2)

    def rotary(t):
        partner = jnp.where(first_half, pltpu.roll(t, LANES - RET_HD // 2, 1),
                            pltpu.roll(t, RET_HD // 2, 1))
        return t * cos + partner * sin

    rq = proj(0)
    for t in range(n_tiles):
        sl = slice(t * LANES, (t + 1) * LANES)
        rq_ref[0, :, sl] = rotary(rq[:, sl]).astype(BF16)
    rk = proj(1)
    for t in range(n_tiles):
        sl = slice(t * LANES, (t + 1) * LANES)
        rk_ref[0, :, sl] = (rotary(rk[:, sl]) * (RET_HD ** -0.5)).astype(BF16)
    rv_ref[0] = proj(2).astype(BF16)
    rg_ref[0] = proj(3).astype(BF16)

    ones = ones_ref[...]

    def qk_norm(t, g):
        ssq = _group_sum(t * t, ones)
        return t * lax.rsqrt(ssq * (1.0 / DIFF_HD) + NORM_EPS) * g

    dq = proj(4)
    q_scale = (DIFF_HD ** -0.5) * LOG2E
    for t in range(n_tiles):
        sl = slice(t * LANES, (t + 1) * LANES)
        qn = qk_norm(dq[:, sl], gq_ref[...]) * q_scale
        dqt_ref[0, sl, :] = qn.T.astype(BF16)
    dk = proj(5)
    for t in range(n_tiles):
        sl = slice(t * LANES, (t + 1) * LANES)
        dk_ref[0, :, sl] = qk_norm(dk[:, sl], gk_ref[...]).astype(BF16)
    dv = proj(6)
    for t in range(n_tiles):
        sl = slice(t * LANES, (t + 1) * LANES)
        vt = dv[:, sl].T.astype(BF16)
        for j in range(tm // kv_block):
            dvt_ref[0, j, sl, :] = vt[:, j * kv_block:(j + 1) * kv_block]


def _retention_kernel(q_ref, k_ref, v_ref, gate_ref, dmat_ref, qdec_ref, kdec_ref, cdec_ref,
                      bdmask_ref, ones_ref, gain_ref, o_ref, state_ref, *, n_tiles, n_chunks):
    @pl.when(pl.program_id(1) == 0)
    def _():
        state_ref[...] = jnp.zeros_like(state_ref)

    c = RET_CHUNK
    lane = lax.broadcasted_iota(jnp.int32, (c, LANES), 1)
    head_a = lane < RET_HD
    bdmask = bdmask_ref[...]
    ones = ones_ref[...]

    for ci in range(n_chunks):
        rows = slice(ci * c, (ci + 1) * c)
        for p in range(n_tiles):
            sl = slice(p * LANES, (p + 1) * LANES)
            q2 = q_ref[0, rows, sl]
            k2 = k_ref[0, rows, sl]
            v2 = v_ref[0, rows, sl]
            zero = jnp.zeros_like(q2)
            qs = jnp.concatenate([jnp.where(head_a, q2, zero), jnp.where(head_a, zero, q2)], axis=0)
            scores = _dot_nt(qs, k2)
            inner = (scores * dmat_ref[p]).astype(BF16)
            intra2 = _dot(inner, v2)
            intra = jnp.where(head_a, intra2[:c], intra2[c:])
            state = state_ref[p]
            qd = (q2.astype(F32) * qdec_ref[:, sl]).astype(BF16)
            o = intra + _dot(qd, state.astype(BF16))
            kd = (k2.astype(F32) * kdec_ref[:, sl]).astype(BF16)
            state_ref[p] = state * cdec_ref[:, sl] + _dot_tn(kd, v2) * bdmask

            ssq = _group_sum(o * o, ones)
            y = o * lax.rsqrt(ssq * (1.0 / RET_HD) + NORM_EPS) * gain_ref[:, sl]
            g = gate_ref[0, rows, sl].astype(F32)
            o_ref[0, rows, sl] = (y * (g / (1.0 + jnp.exp(-g)))).astype(BF16)


def _diffattn_kernel(qt_ref, k_ref, vt_ref, lq1_ref, lk1_ref, lq2_ref, lk2_ref, gain_ref,
                     o_ref, s_ref, m_ref, l_ref, acc_ref, *, qblock, block, strip):
    qi = pl.program_id(2)
    n_strips = qblock // strip
    chains = [(mi, t) for mi in range(2) for t in range(n_strips)]
    per_q = qblock // block
    qt = qt_ref[0]
    row = lax.broadcasted_iota(jnp.int32, qt.shape, 0)
    zero = jnp.zeros_like(qt)
    q_maps = (jnp.where(row < DIFF_HD, qt, zero), jnp.where(row < DIFF_HD, zero, qt))

    m_ref[...] = jnp.full_like(m_ref, MASK_VALUE)
    l_ref[...] = jnp.zeros_like(l_ref)
    acc_ref[...] = jnp.zeros_like(acc_ref)

    def scores(kj, c, buf, nkeys=block):
        mi, t = chains[c]
        start = pl.multiple_of(kj * block, block)
        k = k_ref[0, pl.ds(start, nkeys), :]
        s_ref[buf, c, :nkeys, :] = _dot(k, q_maps[mi][:, t * strip:(t + 1) * strip])

    def softmax_pv(kj, c, buf, nkeys=block, diag=None):
        mi, t = chains[c]
        cols = slice(t * strip, (t + 1) * strip)
        s = s_ref[buf, c, :nkeys, :]
        if diag is not None:
            kpos = lax.broadcasted_iota(jnp.int32, (nkeys, strip), 0) + diag * block
            qpos = lax.broadcasted_iota(jnp.int32, (nkeys, strip), 1) + t * strip
            s = jnp.where(kpos <= qpos, s, MASK_VALUE)
        m_old = m_ref[mi, :, cols]
        m_new = jnp.maximum(m_old, jnp.max(s, axis=0, keepdims=True))
        alpha = jnp.exp2(m_old - m_new)
        p = jnp.exp2(s - m_new)
        l_ref[mi, :, cols] = alpha * l_ref[mi, :, cols] + jnp.sum(p, axis=0, keepdims=True)
        acc_ref[mi, :, cols] = (alpha * acc_ref[mi, :, cols]
                                + _dot(vt_ref[0, kj, :, :nkeys], p.astype(BF16)))
        m_ref[mi, :, cols] = m_new

    n_chains = len(chains)
    for c in range(n_chains):
        scores(0, c, 0)

    def body(t, carry):
        for c in range(n_chains):
            scores(2 * t + 1, c, 1)
            softmax_pv(2 * t, c, 0)
        for c in range(n_chains):
            scores(2 * t + 2, c, 0)
            softmax_pv(2 * t + 1, c, 1)
        return carry

    n_full = qi * per_q
    lax.fori_loop(0, n_full // 2, body, 0)

    def diag_work(d):
        work = []
        for c, (mi, t) in enumerate(chains):
            nkeys = min(block, (t + 1) * strip - d * block)
            if nkeys > 0:
                masked = t * strip < d * block + nkeys - 1
                work.append((c, nkeys, d if masked else None))
        return work

    for d in range(per_q):
        if d + 1 < per_q:
            for c, nkeys, _ in diag_work(d + 1):
                scores(n_full + d + 1, c, (d + 1) % 2, nkeys)
        for c, nkeys, diag in diag_work(d):
            softmax_pv(n_full + d, c, d % 2, nkeys, diag)

    lam = (jnp.exp(jnp.sum(lq1_ref[...] * lk1_ref[...], axis=-1, keepdims=True))
           - jnp.exp(jnp.sum(lq2_ref[...] * lk2_ref[...], axis=-1, keepdims=True))
           + LAMBDA_INIT)
    o = acc_ref[0] / l_ref[0] - lam * (acc_ref[1] / l_ref[1])
    ms = jnp.mean(o * o, axis=0, keepdims=True)
    o = o * lax.rsqrt(ms + NORM_EPS)
    o_ref[0] = (o.T * gain_ref[...] * (1.0 - LAMBDA_INIT)).astype(BF16)


def _out_ffn_kernel(x_ref, ret_ref, dif_ref, wo_ref, g2_ref, wg_ref, wu_ref, wd_ref, o_ref, *,
                    ret_width, ff_chunks):
    x1 = (x_ref[0] + _dot(ret_ref[0], wo_ref[:ret_width, :])
          + _dot(dif_ref[0], wo_ref[ret_width:, :]))
    ms = jnp.mean(x1 * x1, axis=-1, keepdims=True)
    h = (x1 * lax.rsqrt(ms + NORM_EPS) * g2_ref[...]).astype(BF16)
    o_ref[0] = x1
    ff = None
    for lo, hi in ff_chunks:
        g = _dot(h, wg_ref[:, lo:hi])
        u = _dot(h, wu_ref[:, lo:hi])
        a = (g / (1.0 + jnp.exp(-g)) * u).astype(BF16)
        d = _dot(a, wd_ref[lo:hi, :])
        ff = d if ff is None else ff + d
    o_ref[0] += ff


def _resident(shape):
    return pl.BlockSpec(shape, lambda *_: (0,) * len(shape), pipeline_mode=pl.Buffered(1))


def _ff_chunks(d_ff, mxu_cols=256, target=1536):
    chunks, lo = [], 0
    while lo < d_ff:
        hi = min(d_ff, lo + target)
        if d_ff - hi and (d_ff - hi) < mxu_cols:
            hi = d_ff
        chunks.append((lo, hi))
        lo = hi
    return tuple(chunks)


def kernel(x, norm1_g, w_in, ret_norm_g, diff_q_norm_g, diff_k_norm_g, lambda_q1, lambda_k1,
           lambda_q2, lambda_k2, diff_subln_g, w_out, norm2_g, w_gate, w_up, w_down):
    B, S, D = x.shape
    depth = w_in.shape[0]
    n_ret_heads = ret_norm_g.shape[1]
    ret_width = n_ret_heads * RET_HD
    in_width = w_in.shape[2]
    width = ret_width
    assert in_width == 7 * width and w_out.shape[1] == 2 * width
    assert width % LANES == 0 and S % ROW_TILE == 0 and S % RET_TILE == 0 and S % ATT_BLOCK == 0
    assert ROW_TILE % ATT_BLOCK == 0 and RET_TILE % RET_CHUNK == 0
    assert S % ATT_QBLOCK == 0 and (ATT_QBLOCK // ATT_BLOCK) % 2 == 0 and ATT_QBLOCK % ATT_STRIP == 0
    n_diff_heads = width // DIFF_VD
    n_tiles = width // LANES
    d_ff = w_gate.shape[2]
    nk = S // ATT_BLOCK
    params = functools.partial(pltpu.CompilerParams, vmem_limit_bytes=VMEM_LIMIT_BYTES)

    pos = jnp.arange(S, dtype=F32)
    freqs = 1.0 / (ROPE_BASE ** (jnp.arange(0, RET_HD, 2, dtype=F32) / RET_HD))
    ang = pos[:, None] * freqs[None, :]
    cos_t = jnp.tile(jnp.cos(ang), (1, LANES // (RET_HD // 2)))
    sin_half = jnp.sin(ang)
    sin_t = jnp.tile(jnp.concatenate([-sin_half, sin_half], axis=-1), (1, HEADS_PER_TILE))
    grp = jnp.arange(LANES) // RET_HD
    group_ones = (grp[:, None] == grp[None, :]).astype(BF16)
    bdmask = group_ones.astype(F32)

    log_g = jnp.log(1.0 - 2.0 ** (-5.0 - jnp.arange(n_ret_heads, dtype=F32)))
    idx = jnp.arange(RET_CHUNK, dtype=F32)
    rel = idx[:, None] - idx[None, :]
    decay_in = jnp.where(rel[None] >= 0,
                         jnp.exp(log_g[:, None, None] * jnp.maximum(rel, 0.0)[None]), 0.0)
    dmat = decay_in.reshape(n_tiles, HEADS_PER_TILE * RET_CHUNK, RET_CHUNK)
    by_lane = lambda t: jnp.repeat(t, RET_HD, axis=0).T
    qdec = by_lane(jnp.exp(log_g[:, None] * (idx[None, :] + 1.0)))
    kdec = by_lane(jnp.exp(log_g[:, None] * (RET_CHUNK - 1.0 - idx[None, :])))
    cdec = by_lane(jnp.exp(log_g * RET_CHUNK)[:, None])

    for l in range(depth):
        row = lambda b, i: (b, i, 0)
        col = lambda b, i: (b, 0, i)
        tab = lambda b, i: (i, 0)
        act = jax.ShapeDtypeStruct((B, S, width), BF16)
        act_t = jax.ShapeDtypeStruct((B, width, S), BF16)
        rq, rk, rv, rg, dqt, dk, dvt = pl.pallas_call(
            functools.partial(_inproj_kernel, width=width, kv_block=ATT_BLOCK),
            grid=(B, S // ROW_TILE),
            in_specs=[
                pl.BlockSpec((1, ROW_TILE, D), row),
                _resident((1, D)),
                _resident((D, in_width)),
                pl.BlockSpec((ROW_TILE, LANES), tab),
                pl.BlockSpec((ROW_TILE, LANES), tab),
                _resident((1, LANES)),
                _resident((1, LANES)),
                _resident((LANES, LANES)),
            ],
            out_specs=[
                pl.BlockSpec((1, ROW_TILE, width), row),
                pl.BlockSpec((1, ROW_TILE, width), row),
                pl.BlockSpec((1, ROW_TILE, width), row),
                pl.BlockSpec((1, ROW_TILE, width), row),
                pl.BlockSpec((1, width, ROW_TILE), col),
                pl.BlockSpec((1, ROW_TILE, width), row),
                pl.BlockSpec((1, ROW_TILE // ATT_BLOCK, width, ATT_BLOCK), lambda b, i: (b, i, 0, 0)),
            ],
            out_shape=[act, act, act, act, act_t, act,
                       jax.ShapeDtypeStruct((B, nk, width, ATT_BLOCK), BF16)],
            compiler_params=params(dimension_semantics=("parallel", "parallel")),
            name="inproj",
        )(x, norm1_g[l][None, :], w_in[l].astype(BF16), cos_t, sin_t,
          jnp.tile(diff_q_norm_g[l], LANES // DIFF_HD)[None, :],
          jnp.tile(diff_k_norm_g[l], LANES // DIFF_HD)[None, :], group_ones)

        seq = pl.BlockSpec((1, RET_TILE, width), row)
        ret = pl.pallas_call(
            functools.partial(_retention_kernel, n_tiles=n_tiles, n_chunks=RET_TILE // RET_CHUNK),
            grid=(B, S // RET_TILE),
            in_specs=[seq, seq, seq, seq,
                      _resident(dmat.shape), _resident(qdec.shape), _resident(kdec.shape),
                      _resident(cdec.shape), _resident(bdmask.shape), _resident(group_ones.shape),
                      _resident((1, width))],
            out_specs=seq,
            out_shape=act,
            scratch_shapes=[pltpu.VMEM((n_tiles, LANES, LANES), F32)],
            compiler_params=params(dimension_semantics=("arbitrary", "arbitrary")),
            name="retention",
        )(rq, rk, rv, rg, dmat, qdec, kdec, cdec, bdmask, group_ones,
          ret_norm_g[l].reshape(1, width))

        lam_spec = _resident((1, DIFF_HD))
        dif = pl.pallas_call(
            functools.partial(_diffattn_kernel, qblock=ATT_QBLOCK, block=ATT_BLOCK, strip=ATT_STRIP),
            grid=(B, n_diff_heads, S // ATT_QBLOCK),
            in_specs=[
                pl.BlockSpec((1, DIFF_VD, ATT_QBLOCK), lambda b, h, i: (b, h, i)),
                pl.BlockSpec((1, S, DIFF_VD), lambda b, h, i: (b, 0, h)),
                pl.BlockSpec((1, nk, DIFF_VD, ATT_BLOCK), lambda b, h, i: (b, 0, h, 0)),
                lam_spec, lam_spec, lam_spec, lam_spec,
                _resident((1, DIFF_VD)),
            ],
            out_specs=pl.BlockSpec((1, ATT_QBLOCK, DIFF_VD), lambda b, h, i: (b, i, h)),
            out_shape=act,
            scratch_shapes=[pltpu.VMEM((2, 2 * ATT_QBLOCK // ATT_STRIP, ATT_BLOCK, ATT_STRIP), F32),
                            pltpu.VMEM((2, 1, ATT_QBLOCK), F32),
                            pltpu.VMEM((2, 1, ATT_QBLOCK), F32),
                            pltpu.VMEM((2, DIFF_VD, ATT_QBLOCK), F32)],
            compiler_params=params(dimension_semantics=("parallel", "parallel", "arbitrary")),
            name="diffattn",
        )(dqt, dk, dvt, lambda_q1[l][None, :], lambda_k1[l][None, :], lambda_q2[l][None, :],
          lambda_k2[l][None, :], diff_subln_g[l][None, :])

        tile = pl.BlockSpec((1, ROW_TILE, D), row)
        half = pl.BlockSpec((1, ROW_TILE, width), row)
        x = pl.pallas_call(
            functools.partial(_out_ffn_kernel, ret_width=width, ff_chunks=_ff_chunks(d_ff)),
            grid=(B, S // ROW_TILE),
            in_specs=[tile, half, half,
                      _resident((2 * width, D)), _resident((1, D)),
                      _resident((D, d_ff)), _resident((D, d_ff)), _resident((d_ff, D))],
            out_specs=tile,
            out_shape=jax.ShapeDtypeStruct((B, S, D), F32),
            compiler_params=params(dimension_semantics=("parallel", "parallel")),
            name="out_ffn",
        )(x, ret, dif, w_out[l].astype(BF16), norm2_g[l][None, :],
          w_gate[l].astype(BF16), w_up[l].astype(BF16), w_down[l].astype(BF16))
    return x
```

```python
import functools
import math

import jax
import jax.numpy as jnp
import numpy as np
from jax import lax
from jax.experimental import pallas as pl
from jax.experimental.pallas import tpu as pltpu

F32 = jnp.float32
BF16 = jnp.bfloat16

LANES = 128
VMEM_LIMIT_BYTES = 56 * 1024 * 1024

RET_HD = 64
DIFF_HD = 64
DIFF_VD = 2 * DIFF_HD
HEADS_PER_TILE = LANES // RET_HD
RET_CHUNK = 128
ROPE_BASE = 10000.0
NORM_EPS = 1e-6
LAMBDA_INIT = 0.8 - 0.6 * math.exp(-0.3 * 0)
MASK_VALUE = -1e30
LOG2E = 1.0 / math.log(2.0)

ROW_TILE = 512
RET_TILE = 512
ATT_QBLOCK = 1024
ATT_BLOCK = 512
ATT_STRIP = 256


def _dot(a, b):
    return jnp.dot(a, b, preferred_element_type=F32)


def _dot_nt(a, b):
    return lax.dot_general(a, b, (((1,), (1,)), ((), ())), preferred_element_type=F32)


def _dot_tn(a, b):
    return lax.dot_general(a, b, (((0,), (0,)), ((), ())), preferred_element_type=F32)


def _group_sum(sq, group_ones):
    return _dot(sq.astype(BF16), group_ones)


def _inproj_kernel(x_ref, g1_ref, w_ref, cos_ref, sin_ref, gq_ref, gk_ref, ones_ref,
                   rq_ref, rk_ref, rv_ref, rg_ref, dqt_ref, dk_ref, dvt_ref, *, width, kv_block):
    x = x_ref[0]
    tm = x.shape[0]
    ms = jnp.mean(x * x, axis=-1, keepdims=True)
    h = (x * lax.rsqrt(ms + NORM_EPS) * g1_ref[...]).astype(BF16)

    def proj(j):
        return _dot(h, w_ref[:, j * width:(j + 1) * width])

    n_tiles = width // LANES
    cos = cos_ref[...]
    sin = sin_ref[...]
    lane = lax.broadcasted_iota(jnp.int32, (tm, LANES), 1)
    first_half = (lane % RET_HD) < (RET_HD // 2)

    def rotary(t):
        partner = jnp.where(first_half, pltpu.roll(t, LANES - RET_HD // 2, 1),
                            pltpu.roll(t, RET_HD // 2, 1))
        return t * cos + partner * sin

    rq = proj(0)
    for t in range(n_tiles):
        sl = slice(t * LANES, (t + 1) * LANES)
        rq_ref[0, :, sl] = rotary(rq[:, sl]).astype(BF16)
    rk = proj(1)
    for t in range(n_tiles):
        sl = slice(t * LANES, (t + 1) * LANES)
        rk_ref[0, :, sl] = (rotary(rk[:, sl]) * (RET_HD ** -0.5)).astype(BF16)
    rv_ref[0] = proj(2).astype(BF16)
    rg_ref[0] = proj(3).astype(BF16)

    ones = ones_ref[...]

    def qk_norm(t, g):
        ssq = _group_sum(t * t, ones)
        return t * lax.rsqrt(ssq * (1.0 / DIFF_HD) + NORM_EPS) * g

    dq = proj(4)
    q_scale = (DIFF_HD ** -0.5) * LOG2E
    for t in range(n_tiles):
        sl = slice(t * LANES, (t + 1) * LANES)
        qn = qk_norm(dq[:, sl], gq_ref[...]) * q_scale
        dqt_ref[0, sl, :] = qn.T.astype(BF16)
    dk = proj(5)
    for t in range(n_tiles):
        sl = slice(t * LANES, (t + 1) * LANES)
        dk_ref[0, :, sl] = qk_norm(dk[:, sl], gk_ref[...]).astype(BF16)
    dv = proj(6)
    for t in range(n_tiles):
        sl = slice(t * LANES, (t + 1) * LANES)
        vt = dv[:, sl].T.astype(BF16)
        for j in range(tm // kv_block):
            dvt_ref[0, j, sl, :] = vt[:, j * kv_block:(j + 1) * kv_block]


def _retention_kernel(q_ref, k_ref, v_ref, gate_ref, dmat_ref, qdec_ref, kdec_ref, cdec_ref,
                      bdmask_ref, ones_ref, gain_ref, o_ref, state_ref, acc_ref, *, n_tiles, n_chunks):
    @pl.when(pl.program_id(1) == 0)
    def _():
        state_ref[...] = jnp.zeros_like(state_ref)

    c = RET_CHUNK
    lane = lax.broadcasted_iota(jnp.int32, (c, LANES), 1)
    head_a = lane < RET_HD
    bdmask = bdmask_ref[...]

    for p in range(n_tiles):
        sl = slice(p * LANES, (p + 1) * LANES)
        kvs = []
        for ci in range(n_chunks):
            rows = slice(ci * c, (ci + 1) * c)
            q2 = q_ref[0, rows, sl]
            k2 = k_ref[0, rows, sl]
            v2 = v_ref[0, rows, sl]
            zero = jnp.zeros_like(q2)
            qs = jnp.concatenate([jnp.where(head_a, q2, zero), jnp.where(head_a, zero, q2)], axis=0)
            scores = _dot_nt(qs, k2)
            inner = (scores * dmat_ref[p]).astype(BF16)
            intra2 = _dot(inner, v2)
            acc_ref[rows, sl] = jnp.where(head_a, intra2[:c], intra2[c:])
            kd = (k2.astype(F32) * kdec_ref[:, sl]).astype(BF16)
            kvs.append(_dot_tn(kd, v2) * bdmask)
        state = state_ref[p]
        for ci in range(n_chunks):
            rows = slice(ci * c, (ci + 1) * c)
            qd = (q_ref[0, rows, sl].astype(F32) * qdec_ref[:, sl]).astype(BF16)
            acc_ref[rows, sl] += _dot(qd, state.astype(BF16))
            state = state * cdec_ref[:, sl] + kvs[ci]
        state_ref[p] = state

    for p in range(n_tiles):
        sl = slice(p * LANES, (p + 1) * LANES)
        o = acc_ref[:, sl]
        ssq = _group_sum(o * o, ones_ref[...])
        y = o * lax.rsqrt(ssq * (1.0 / RET_HD) + NORM_EPS) * gain_ref[:, sl]
        g = gate_ref[0, :, sl].astype(F32)
        o_ref[0, :, sl] = (y * (g / (1.0 + jnp.exp(-g)))).astype(BF16)


def _diffattn_kernel(qt_ref, k_ref, vt_ref, lq1_ref, lk1_ref, lq2_ref, lk2_ref, gain_ref,
                     o_ref, s_ref, m_ref, l_ref, acc_ref, *, qblock, block, strip):
    qi = pl.program_id(2)
    n_strips = qblock // strip
    chains = [(mi, t) for mi in range(2) for t in range(n_strips)]
    per_q = qblock // block
    qt = qt_ref[0]
    row = lax.broadcasted_iota(jnp.int32, qt.shape, 0)
    zero = jnp.zeros_like(qt)
    q_maps = (jnp.where(row < DIFF_HD, qt, zero), jnp.where(row < DIFF_HD, zero, qt))

    m_ref[...] = jnp.full_like(m_ref, MASK_VALUE)
    l_ref[...] = jnp.zeros_like(l_ref)
    acc_ref[...] = jnp.zeros_like(acc_ref)

    def scores(kj, c, buf, nkeys=block):
        mi, t = chains[c]
        start = pl.multiple_of(kj * block, block)
        k = k_ref[0, pl.ds(start, nkeys), :]
        s_ref[buf, c, :nkeys, :] = _dot(k, q_maps[mi][:, t * strip:(t + 1) * strip])

    def softmax_pv(kj, c, buf, nkeys=block, diag=None):
        mi, t = chains[c]
        cols = slice(t * strip, (t + 1) * strip)
        s = s_ref[buf, c, :nkeys, :]
        if diag is not None:
            kpos = lax.broadcasted_iota(jnp.int32, (nkeys, strip), 0) + diag * block
            qpos = lax.broadcasted_iota(jnp.int32, (nkeys, strip), 1) + t * strip
            s = jnp.where(kpos <= qpos, s, MASK_VALUE)
        m_old = m_ref[mi, :, cols]
        m_new = jnp.maximum(m_old, jnp.max(s, axis=0, keepdims=True))
        alpha = jnp.exp2(m_old - m_new)
        p = jnp.exp2(s - m_new)
        l_ref[mi, :, cols] = alpha * l_ref[mi, :, cols] + jnp.sum(p, axis=0, keepdims=True)
        acc_ref[mi, :, cols] = (alpha * acc_ref[mi, :, cols]
                                + _dot(vt_ref[0, kj, :, :nkeys], p.astype(BF16)))
        m_ref[mi, :, cols] = m_new

    n_chains = len(chains)
    for c in range(n_chains):
        scores(0, c, 0)

    def body(t, carry):
        for c in range(n_chains):
            scores(2 * t + 1, c, 1)
            softmax_pv(2 * t, c, 0)
        for c in range(n_chains):
            scores(2 * t + 2, c, 0)
            softmax_pv(2 * t + 1, c, 1)
        return carry

    n_full = qi * per_q
    lax.fori_loop(0, n_full // 2, body, 0)

    def diag_work(d):
        work = []
        for c, (mi, t) in enumerate(chains):
            nkeys = min(block, (t + 1) * strip - d * block)
            if nkeys > 0:
                masked = t * strip < d * block + nkeys - 1
                work.append((c, nkeys, d if masked else None))
        return work

    for d in range(per_q):
        if d + 1 < per_q:
            for c, nkeys, _ in diag_work(d + 1):
                scores(n_full + d + 1, c, (d + 1) % 2, nkeys)
        for c, nkeys, diag in diag_work(d):
            softmax_pv(n_full + d, c, d % 2, nkeys, diag)

    lam = (jnp.exp(jnp.sum(lq1_ref[...] * lk1_ref[...], axis=-1, keepdims=True))
           - jnp.exp(jnp.sum(lq2_ref[...] * lk2_ref[...], axis=-1, keepdims=True))
           + LAMBDA_INIT)
    o = acc_ref[0] / l_ref[0] - lam * (acc_ref[1] / l_ref[1])
    ms = jnp.mean(o * o, axis=0, keepdims=True)
    o = o * lax.rsqrt(ms + NORM_EPS)
    o_ref[0] = (o.T * gain_ref[...] * (1.0 - LAMBDA_INIT)).astype(BF16)


def _out_ffn_kernel(x_ref, ret_ref, dif_ref, wo_ref, g2_ref, wg_ref, wu_ref, wd_ref, o_ref, *,
                    ret_width, ff_chunks):
    x1 = (x_ref[0] + _dot(ret_ref[0], wo_ref[:ret_width, :])
          + _dot(dif_ref[0], wo_ref[ret_width:, :]))
    ms = jnp.mean(x1 * x1, axis=-1, keepdims=True)
    h = (x1 * lax.rsqrt(ms + NORM_EPS) * g2_ref[...]).astype(BF16)
    o_ref[0] = x1
    ff = None
    for lo, hi in ff_chunks:
        g = _dot(h, wg_ref[:, lo:hi])
        u = _dot(h, wu_ref[:, lo:hi])
        a = (g / (1.0 + jnp.exp(-g)) * u).astype(BF16)
        d = _dot(a, wd_ref[lo:hi, :])
        ff = d if ff is None else ff + d
    o_ref[0] += ff


def _resident(shape):
    return pl.BlockSpec(shape, lambda *_: (0,) * len(shape), pipeline_mode=pl.Buffered(1))


def _ff_chunks(d_ff, mxu_cols=256, target=1536):
    chunks, lo = [], 0
    while lo < d_ff:
        hi = min(d_ff, lo + target)
        if d_ff - hi and (d_ff - hi) < mxu_cols:
            hi = d_ff
        chunks.append((lo, hi))
        lo = hi
    return tuple(chunks)


def kernel(x, norm1_g, w_in, ret_norm_g, diff_q_norm_g, diff_k_norm_g, lambda_q1, lambda_k1,
           lambda_q2, lambda_k2, diff_subln_g, w_out, norm2_g, w_gate, w_up, w_down):
    B, S, D = x.shape
    depth = w_in.shape[0]
    n_ret_heads = ret_norm_g.shape[1]
    ret_width = n_ret_heads * RET_HD
    in_width = w_in.shape[2]
    width = ret_width
    assert in_width == 7 * width and w_out.shape[1] == 2 * width
    assert width % LANES == 0 and S % ROW_TILE == 0 and S % RET_TILE == 0 and S % ATT_BLOCK == 0
    assert ROW_TILE % ATT_BLOCK == 0 and RET_TILE % RET_CHUNK == 0
    assert S % ATT_QBLOCK == 0 and (ATT_QBLOCK // ATT_BLOCK) % 2 == 0 and ATT_QBLOCK % ATT_STRIP == 0
    n_diff_heads = width // DIFF_VD
    n_tiles = width // LANES
    d_ff = w_gate.shape[2]
    nk = S // ATT_BLOCK
    params = functools.partial(pltpu.CompilerParams, vmem_limit_bytes=VMEM_LIMIT_BYTES)

    pos = np.arange(S, dtype=np.float64)
    freqs = 1.0 / (ROPE_BASE ** (np.arange(0, RET_HD, 2, dtype=np.float64) / RET_HD))
    ang = pos[:, None] * freqs[None, :]
    cos_t = jnp.asarray(np.tile(np.cos(ang), (1, LANES // (RET_HD // 2))), F32)
    sin_half = np.sin(ang)
    sin_t = jnp.asarray(np.tile(np.concatenate([-sin_half, sin_half], axis=-1), (1, HEADS_PER_TILE)), F32)
    grp = np.arange(LANES) // RET_HD
    same_group = (grp[:, None] == grp[None, :]).astype(np.float32)
    group_ones = jnp.asarray(same_group, BF16)
    bdmask = jnp.asarray(same_group, F32)

    log_g = np.log(1.0 - 2.0 ** (-5.0 - np.arange(n_ret_heads, dtype=np.float64)))
    idx = np.arange(RET_CHUNK, dtype=np.float64)
    rel = idx[:, None] - idx[None, :]
    decay_in = np.where(rel[None] >= 0, np.exp(log_g[:, None, None] * np.maximum(rel, 0.0)[None]), 0.0)
    dmat = jnp.asarray(decay_in.reshape(n_tiles, HEADS_PER_TILE * RET_CHUNK, RET_CHUNK), F32)
    by_lane = lambda t: jnp.asarray(np.repeat(t, RET_HD, axis=0).T, F32)
    qdec = by_lane(np.exp(log_g[:, None] * (idx[None, :] + 1.0)))
    kdec = by_lane(np.exp(log_g[:, None] * (RET_CHUNK - 1.0 - idx[None, :])))
    cdec = by_lane(np.exp(log_g * RET_CHUNK)[:, None])

    for l in range(depth):
        row = lambda b, i: (b, i, 0)
        col = lambda b, i: (b, 0, i)
        tab = lambda b, i: (i, 0)
        act = jax.ShapeDtypeStruct((B, S, width), BF16)
        act_t = jax.ShapeDtypeStruct((B, width, S), BF16)
        rq, rk, rv, rg, dqt, dk, dvt = pl.pallas_call(
            functools.partial(_inproj_kernel, width=width, kv_block=ATT_BLOCK),
            grid=(B, S // ROW_TILE),
            in_specs=[
                pl.BlockSpec((1, ROW_TILE, D), row),
                _resident((1, D)),
                _resident((D, in_width)),
                pl.BlockSpec((ROW_TILE, LANES), tab),
                pl.BlockSpec((ROW_TILE, LANES), tab),
                _resident((1, LANES)),
                _resident((1, LANES)),
                _resident((LANES, LANES)),
            ],
            out_specs=[
                pl.BlockSpec((1, ROW_TILE, width), row),
                pl.BlockSpec((1, ROW_TILE, width), row),
                pl.BlockSpec((1, ROW_TILE, width), row),
                pl.BlockSpec((1, ROW_TILE, width), row),
                pl.BlockSpec((1, width, ROW_TILE), col),
                pl.BlockSpec((1, ROW_TILE, width), row),
                pl.BlockSpec((1, ROW_TILE // ATT_BLOCK, width, ATT_BLOCK), lambda b, i: (b, i, 0, 0)),
            ],
            out_shape=[act, act, act, act, act_t, act,
                       jax.ShapeDtypeStruct((B, nk, width, ATT_BLOCK), BF16)],
            compiler_params=params(dimension_semantics=("parallel", "parallel")),
            name="inproj",
        )(x, norm1_g[l][None, :], w_in[l].astype(BF16), cos_t, sin_t,
          jnp.tile(diff_q_norm_g[l], LANES // DIFF_HD)[None, :],
          jnp.tile(diff_k_norm_g[l], LANES // DIFF_HD)[None, :], group_ones)

        seq = pl.BlockSpec((1, RET_TILE, width), row)
        ret = pl.pallas_call(
            functools.partial(_retention_kernel, n_tiles=n_tiles, n_chunks=RET_TILE // RET_CHUNK),
            grid=(B, S // RET_TILE),
            in_specs=[seq, seq, seq, seq,
                      _resident(dmat.shape), _resident(qdec.shape), _resident(kdec.shape),
                      _resident(cdec.shape), _resident(bdmask.shape), _resident(group_ones.shape),
                      _resident((1, width))],
            out_specs=seq,
            out_shape=act,
            scratch_shapes=[pltpu.VMEM((n_tiles, LANES, LANES), F32),
                            pltpu.VMEM((RET_TILE, width), F32)],
            compiler_params=params(dimension_semantics=("arbitrary", "arbitrary")),
            name="retention",
        )(rq, rk, rv, rg, dmat, qdec, kdec, cdec, bdmask, group_ones,
          ret_norm_g[l].reshape(1, width))

        lam_spec = _resident((1, DIFF_HD))
        dif = pl.pallas_call(
            functools.partial(_diffattn_kernel, qblock=ATT_QBLOCK, block=ATT_BLOCK, strip=ATT_STRIP),
            grid=(B, n_diff_heads, S // ATT_QBLOCK),
            in_specs=[
                pl.BlockSpec((1, DIFF_VD, ATT_QBLOCK), lambda b, h, i: (b, h, i)),
                pl.BlockSpec((1, S, DIFF_VD), lambda b, h, i: (b, 0, h)),
                pl.BlockSpec((1, nk, DIFF_VD, ATT_BLOCK), lambda b, h, i: (b, 0, h, 0)),
                lam_spec, lam_spec, lam_spec, lam_spec,
                _resident((1, DIFF_VD)),
            ],
            out_specs=pl.BlockSpec((1, ATT_QBLOCK, DIFF_VD), lambda b, h, i: (b, i, h)),
            out_shape=act,
            scratch_shapes=[pltpu.VMEM((2, 2 * ATT_QBLOCK // ATT_STRIP, ATT_BLOCK, ATT_STRIP), F32),
                            pltpu.VMEM((2, 1, ATT_QBLOCK), F32),
                            pltpu.VMEM((2, 1, ATT_QBLOCK), F32),
                            pltpu.VMEM((2, DIFF_VD, ATT_QBLOCK), F32)],
            compiler_params=params(dimension_semantics=("parallel", "parallel", "arbitrary")),
            name="diffattn",
        )(dqt, dk, dvt, lambda_q1[l][None, :], lambda_k1[l][None, :], lambda_q2[l][None, :],
          lambda_k2[l][None, :], diff_subln_g[l][None, :])

        tile = pl.BlockSpec((1, ROW_TILE, D), row)
        half = pl.BlockSpec((1, ROW_TILE, width), row)
        x = pl.pallas_call(
            functools.partial(_out_ffn_kernel, ret_width=width, ff_chunks=_ff_chunks(d_ff)),
            grid=(B, S // ROW_TILE),
            in_specs=[tile, half, half,
                      _resident((2 * width, D)), _resident((1, D)),
                      _resident((D, d_ff)), _resident((D, d_ff)), _resident((d_ff, D))],
            out_specs=tile,
            out_shape=jax.ShapeDtypeStruct((B, S, D), F32),
            compiler_params=params(dimension_semantics=("parallel", "parallel")),
            name="out_ffn",
        )(x, ret, dif, w_out[l].astype(BF16), norm2_g[l][None, :],
          w_gate[l].astype(BF16), w_up[l].astype(BF16), w_down[l].astype(BF16))
    return x
```

```python
import functools
import math

import jax
import jax.numpy as jnp
import numpy as np
from jax import lax
from jax.experimental import pallas as pl
from jax.experimental.pallas import tpu as pltpu

F32 = jnp.float32
BF16 = jnp.bfloat16

LANES = 128
BF16_ROWS = 16
VMEM_LIMIT_BYTES = 56 * 1024 * 1024

RET_HD = 64
DIFF_HD = 64
DIFF_VD = 2 * DIFF_HD
HEADS_PER_TILE = LANES // RET_HD
RET_CHUNK = 128
ROPE_BASE = 10000.0
NORM_EPS = 1e-6
LAMBDA_INIT = 0.8 - 0.6 * math.exp(-0.3 * 0)
MASK_VALUE = -1e30
LOG2E = 1.0 / math.log(2.0)

ROW_TILE = 512
RET_TILE = 512
ATT_QBLOCK = 1024
ATT_BLOCK = 512
ATT_STRIP = 256
ATT_RING = 16
ATT_LEAD = 4


def _dot(a, b):
    return jnp.dot(a, b, preferred_element_type=F32)


def _dot_nt(a, b):
    return lax.dot_general(a, b, (((1,), (1,)), ((), ())), preferred_element_type=F32)


def _dot_tn(a, b):
    return lax.dot_general(a, b, (((0,), (0,)), ((), ())), preferred_element_type=F32)


def _group_sum(sq, group_ones):
    return _dot(sq.astype(BF16), group_ones)


def _inproj_kernel(x_ref, g1_ref, w_ref, cos_ref, sin_ref, gq_ref, gk_ref, ones_ref,
                   rq_ref, rk_ref, rv_ref, rg_ref, dqt_ref, dk_ref, dvt_ref, *, width, kv_block):
    x = x_ref[0]
    tm = x.shape[0]
    ms = jnp.mean(x * x, axis=-1, keepdims=True)
    h = (x * lax.rsqrt(ms + NORM_EPS) * g1_ref[...]).astype(BF16)

    def proj(j):
        return _dot(h, w_ref[:, j * width:(j + 1) * width])

    n_tiles = width // LANES
    cos = cos_ref[...]
    sin = sin_ref[...]
    lane = lax.broadcasted_iota(jnp.int32, (tm, LANES), 1)
    first_half = (lane % RET_HD) < (RET_HD // 2)

    def rotary(t):
        partner = jnp.where(first_half, pltpu.roll(t, LANES - RET_HD // 2, 1),
                            pltpu.roll(t, RET_HD // 2, 1))
        return t * cos + partner * sin

    rq = proj(0)
    for t in range(n_tiles):
        sl = slice(t * LANES, (t + 1) * LANES)
        rq_ref[0, :, sl] = rotary(rq[:, sl]).astype(BF16)
    rk = proj(1)
    for t in range(n_tiles):
        sl = slice(t * LANES, (t + 1) * LANES)
        rk_ref[0, :, sl] = (rotary(rk[:, sl]) * (RET_HD ** -0.5)).astype(BF16)
    rv_ref[0] = proj(2).astype(BF16)
    rg_ref[0] = proj(3).astype(BF16)

    ones = ones_ref[...]

    def qk_norm(t, g):
        ssq = _group_sum(t * t, ones)
        return t * lax.rsqrt(ssq * (1.0 / DIFF_HD) + NORM_EPS) * g

    dq = proj(4)
    q_scale = (DIFF_HD ** -0.5) * LOG2E
    for t in range(n_tiles):
        sl = slice(t * LANES, (t + 1) * LANES)
        qn = qk_norm(dq[:, sl], gq_ref[...]) * q_scale
        dqt_ref[0, sl, :] = qn.T.astype(BF16)
    dk = proj(5)
    for t in range(n_tiles):
        sl = slice(t * LANES, (t + 1) * LANES)
        dk_ref[0, :, sl] = qk_norm(dk[:, sl], gk_ref[...]).astype(BF16)
    dv = proj(6)
    for t in range(n_tiles):
        sl = slice(t * LANES, (t + 1) * LANES)
        vt = dv[:, sl].T.astype(BF16)
        for j in range(tm // kv_block):
            dvt_ref[0, j, sl, :] = vt[:, j * kv_block:(j + 1) * kv_block]


def _retention_kernel(q_ref, k_ref, v_ref, gate_ref, dmat_ref, qdec_ref, kdec_ref, cdec_ref,
                      bdmask_ref, ones_ref, gain_ref, o_ref, state_ref, acc_ref, *, n_tiles, n_chunks):
    @pl.when(pl.program_id(1) == 0)
    def _():
        state_ref[...] = jnp.zeros_like(state_ref)

    c = RET_CHUNK
    lane = lax.broadcasted_iota(jnp.int32, (c, LANES), 1)
    head_a = lane < RET_HD
    bdmask = bdmask_ref[...]

    for p in range(n_tiles):
        sl = slice(p * LANES, (p + 1) * LANES)
        kvs = []
        for ci in range(n_chunks):
            rows = slice(ci * c, (ci + 1) * c)
            q2 = q_ref[0, rows, sl]
            k2 = k_ref[0, rows, sl]
            v2 = v_ref[0, rows, sl]
            zero = jnp.zeros_like(q2)
            qs = jnp.concatenate([jnp.where(head_a, q2, zero), jnp.where(head_a, zero, q2)], axis=0)
            scores = _dot_nt(qs, k2)
            inner = (scores * dmat_ref[p]).astype(BF16)
            intra2 = _dot(inner, v2)
            acc_ref[rows, sl] = jnp.where(head_a, intra2[:c], intra2[c:])
            kd = (k2.astype(F32) * kdec_ref[:, sl]).astype(BF16)
            kvs.append(_dot_tn(kd, v2) * bdmask)
        state = state_ref[p]
        for ci in range(n_chunks):
            rows = slice(ci * c, (ci + 1) * c)
            qd = (q_ref[0, rows, sl].astype(F32) * qdec_ref[:, sl]).astype(BF16)
            acc_ref[rows, sl] += _dot(qd, state.astype(BF16))
            state = state * cdec_ref[:, sl] + kvs[ci]
        state_ref[p] = state

    for p in range(n_tiles):
        sl = slice(p * LANES, (p + 1) * LANES)
        o = acc_ref[:, sl]
        ssq = _group_sum(o * o, ones_ref[...])
        y = o * lax.rsqrt(ssq * (1.0 / RET_HD) + NORM_EPS) * gain_ref[:, sl]
        g = gate_ref[0, :, sl].astype(F32)
        o_ref[0, :, sl] = (y * (g / (1.0 + jnp.exp(-g)))).astype(BF16)


def _diffattn_kernel(qt_ref, k_ref, vt_ref, lq1_ref, lk1_ref, lq2_ref, lk2_ref, gain_ref,
                     o_ref, s_ref, smax_ref, m_ref, l_ref, acc_ref, *, qblock, block, strip, lead):
    qi = pl.program_id(2)
    ring = s_ref.shape[0]
    n_strips = qblock // strip
    chains = [(mi, t) for mi in range(2) for t in range(n_strips)]
    n_chains = len(chains)
    per_q = qblock // block
    qt = qt_ref[0]
    row = lax.broadcasted_iota(jnp.int32, qt.shape, 0)
    zero = jnp.zeros_like(qt)
    q_maps = (jnp.where(row < DIFF_HD, qt, zero), jnp.where(row < DIFF_HD, zero, qt))

    m_ref[...] = jnp.full_like(m_ref, MASK_VALUE)
    l_ref[...] = jnp.zeros_like(l_ref)
    acc_ref[...] = jnp.zeros_like(acc_ref)

    def scores(kj, c, slot, nkeys=block):
        mi, t = chains[c]
        start = pl.multiple_of(kj * block, block)
        k = k_ref[0, pl.ds(start, nkeys), :]
        sc = _dot(k, q_maps[mi][:, t * strip:(t + 1) * strip])
        s_ref[slot, :nkeys, :] = sc
        smax_ref[slot] = jnp.max(sc, axis=0, keepdims=True)

    def softmax_pv(kj, c, slot, nkeys=block, diag=None):
        mi, t = chains[c]
        cols = slice(t * strip, (t + 1) * strip)
        s = s_ref[slot, :nkeys, :]
        if diag is not None:
            kpos = lax.broadcasted_iota(jnp.int32, (nkeys, strip), 0) + diag * block
            qpos = lax.broadcasted_iota(jnp.int32, (nkeys, strip), 1) + t * strip
            s = jnp.where(kpos <= qpos, s, MASK_VALUE)
            s_max = jnp.max(s, axis=0, keepdims=True)
        else:
            s_max = smax_ref[slot]
        m_old = m_ref[mi, :, cols]
        m_new = jnp.maximum(m_old, s_max)
        alpha = jnp.exp2(m_old - m_new)
        p = jnp.exp2(s - m_new)
        l_ref[mi, :, cols] = alpha * l_ref[mi, :, cols] + jnp.sum(p, axis=0, keepdims=True)
        acc_ref[mi, :, cols] = (alpha * acc_ref[mi, :, cols]
                                + _dot(vt_ref[0, kj, :, :nkeys], p.astype(BF16)))
        m_ref[mi, :, cols] = m_new

    for c in range(lead):
        scores(0, c, c)

    trip = 2 * n_chains

    def body(t, carry):
        for j in range(trip):
            nxt = j + lead
            scores(2 * t + nxt // n_chains, nxt % n_chains, nxt % ring)
            softmax_pv(2 * t + j // n_chains, j % n_chains, j % ring)
        return carry

    n_full = qi * per_q
    lax.fori_loop(0, n_full // 2, body, 0)

    tail = []
    for d in range(per_q):
        for c, (mi, t) in enumerate(chains):
            nkeys = min(block, (t + 1) * strip - d * block)
            if nkeys > 0:
                masked = t * strip < d * block + nkeys - 1
                tail.append((d, c, nkeys, d if masked else None))
    for i, (d, c, nkeys, diag) in enumerate(tail):
        if i < lead:
            assert d == 0 and c == i
        if i + lead < len(tail):
            d2, c2, nkeys2, _ = tail[i + lead]
            scores(n_full + d2, c2, (i + lead) % ring, nkeys2)
        softmax_pv(n_full + d, c, i % ring, nkeys, diag)

    lam = (jnp.exp(jnp.sum(lq1_ref[...] * lk1_ref[...], axis=-1, keepdims=True))
           - jnp.exp(jnp.sum(lq2_ref[...] * lk2_ref[...], axis=-1, keepdims=True))
           + LAMBDA_INIT)
    o = acc_ref[0] / l_ref[0] - lam * (acc_ref[1] / l_ref[1])
    ms = jnp.mean(o * o, axis=0, keepdims=True)
    o = o * lax.rsqrt(ms + NORM_EPS)
    o_ref[0] = (o.T * gain_ref[...] * (1.0 - LAMBDA_INIT)).astype(BF16)


def _out_ffn_kernel(x_ref, ret_ref, dif_ref, wo_ref, g2_ref, wg_ref, wu_ref, wd_ref, o_ref, *,
                    ret_width, ff_chunks):
    x1 = (x_ref[0] + _dot(ret_ref[0], wo_ref[:ret_width, :])
          + _dot(dif_ref[0], wo_ref[ret_width:, :]))
    ms = jnp.mean(x1 * x1, axis=-1, keepdims=True)
    h = (x1 * lax.rsqrt(ms + NORM_EPS) * g2_ref[...]).astype(BF16)
    o_ref[0] = x1
    ff = None
    for lo, hi in ff_chunks:
        g = _dot(h, wg_ref[:, lo:hi])
        u = _dot(h, wu_ref[:, lo:hi])
        a = (g / (1.0 + jnp.exp(-g)) * u).astype(BF16)
        d = _dot(a, wd_ref[lo:hi, :])
        ff = d if ff is None else ff + d
    o_ref[0] += ff


def _resident(shape):
    return pl.BlockSpec(shape, lambda *_: (0,) * len(shape), pipeline_mode=pl.Buffered(1))


def _ff_chunks(d_ff, mxu_cols=256, target=1536):
    chunks, lo = [], 0
    while lo < d_ff:
        hi = min(d_ff, lo + target)
        if d_ff - hi and (d_ff - hi) < mxu_cols:
            hi = d_ff
        chunks.append((lo, hi))
        lo = hi
    return tuple(chunks)


def kernel(x, norm1_g, w_in, ret_norm_g, diff_q_norm_g, diff_k_norm_g, lambda_q1, lambda_k1,
           lambda_q2, lambda_k2, diff_subln_g, w_out, norm2_g, w_gate, w_up, w_down):
    B, S, D = x.shape
    depth = w_in.shape[0]
    n_ret_heads = ret_norm_g.shape[1]
    ret_width = n_ret_heads * RET_HD
    in_width = w_in.shape[2]
    width = ret_width
    assert in_width == 7 * width and w_out.shape[1] == 2 * width
    assert width % LANES == 0 and S % ROW_TILE == 0 and S % RET_TILE == 0 and S % ATT_BLOCK == 0
    assert ROW_TILE % ATT_BLOCK == 0 and RET_TILE % RET_CHUNK == 0
    assert S % ATT_QBLOCK == 0 and (ATT_QBLOCK // ATT_BLOCK) % 2 == 0 and ATT_QBLOCK % ATT_STRIP == 0
    assert (4 * ATT_QBLOCK // ATT_STRIP) % ATT_RING == 0 and ATT_LEAD < ATT_RING
    assert ATT_LEAD <= 2 * ATT_QBLOCK // ATT_STRIP
    n_diff_heads = width // DIFF_VD
    n_tiles = width // LANES
    d_ff = w_gate.shape[2]
    nk = S // ATT_BLOCK
    params = functools.partial(pltpu.CompilerParams, vmem_limit_bytes=VMEM_LIMIT_BYTES)

    pos = np.arange(S, dtype=np.float64)
    freqs = 1.0 / (ROPE_BASE ** (np.arange(0, RET_HD, 2, dtype=np.float64) / RET_HD))
    ang = pos[:, None] * freqs[None, :]
    cos_t = jnp.asarray(np.tile(np.cos(ang), (1, LANES // (RET_HD // 2))), F32)
    sin_half = np.sin(ang)
    sin_t = jnp.asarray(np.tile(np.concatenate([-sin_half, sin_half], axis=-1), (1, HEADS_PER_TILE)), F32)
    grp = np.arange(LANES) // RET_HD
    same_group = (grp[:, None] == grp[None, :]).astype(np.float32)
    group_ones = jnp.asarray(same_group, BF16)
    bdmask = jnp.asarray(same_group, F32)

    log_g = np.log(1.0 - 2.0 ** (-5.0 - np.arange(n_ret_heads, dtype=np.float64)))
    idx = np.arange(RET_CHUNK, dtype=np.float64)
    rel = idx[:, None] - idx[None, :]
    decay_in = np.where(rel[None] >= 0, np.exp(log_g[:, None, None] * np.maximum(rel, 0.0)[None]), 0.0)
    dmat = jnp.asarray(decay_in.reshape(n_tiles, HEADS_PER_TILE * RET_CHUNK, RET_CHUNK), F32)
    by_lane = lambda t: jnp.asarray(np.repeat(t, RET_HD, axis=0).T, F32)
    qdec = by_lane(np.exp(log_g[:, None] * (idx[None, :] + 1.0)))
    kdec = by_lane(np.exp(log_g[:, None] * (RET_CHUNK - 1.0 - idx[None, :])))
    cdec = by_lane(np.exp(log_g * RET_CHUNK)[:, None])

    for l in range(depth):
        row = lambda b, i: (b, i, 0)
        col = lambda b, i: (b, 0, i)
        tab = lambda b, i: (i, 0)
        act = jax.ShapeDtypeStruct((B, S, width), BF16)
        act_t = jax.ShapeDtypeStruct((B, width, S), BF16)
        rq, rk, rv, rg, dqt, dk, dvt = pl.pallas_call(
            functools.partial(_inproj_kernel, width=width, kv_block=ATT_BLOCK),
            grid=(B, S // ROW_TILE),
            in_specs=[
                pl.BlockSpec((1, ROW_TILE, D), row),
                _resident((1, D)),
                _resident((D, in_width)),
                pl.BlockSpec((ROW_TILE, LANES), tab),
                pl.BlockSpec((ROW_TILE, LANES), tab),
                _resident((1, LANES)),
                _resident((1, LANES)),
                _resident((LANES, LANES)),
            ],
            out_specs=[
                pl.BlockSpec((1, ROW_TILE, width), row),
                pl.BlockSpec((1, ROW_TILE, width), row),
                pl.BlockSpec((1, ROW_TILE, width), row),
                pl.BlockSpec((1, ROW_TILE, width), row),
                pl.BlockSpec((1, width, ROW_TILE), col),
                pl.BlockSpec((1, ROW_TILE, width), row),
                pl.BlockSpec((1, ROW_TILE // ATT_BLOCK, width, ATT_BLOCK), lambda b, i: (b, i, 0, 0)),
            ],
            out_shape=[act, act, act, act, act_t, act,
                       jax.ShapeDtypeStruct((B, nk, width, ATT_BLOCK), BF16)],
            compiler_params=params(dimension_semantics=("parallel", "parallel")),
            name="inproj",
        )(x, norm1_g[l][None, :], w_in[l].astype(BF16), cos_t, sin_t,
          jnp.tile(diff_q_norm_g[l], LANES // DIFF_HD)[None, :],
          jnp.tile(diff_k_norm_g[l], LANES // DIFF_HD)[None, :], group_ones)

        seq = pl.BlockSpec((1, RET_TILE, width), row)
        ret = pl.pallas_call(
            functools.partial(_retention_kernel, n_tiles=n_tiles, n_chunks=RET_TILE // RET_CHUNK),
            grid=(B, S // RET_TILE),
            in_specs=[seq, seq, seq, seq,
                      _resident(dmat.shape), _resident(qdec.shape), _resident(kdec.shape),
                      _resident(cdec.shape), _resident(bdmask.shape), _resident(group_ones.shape),
                      _resident((1, width))],
            out_specs=seq,
            out_shape=act,
            scratch_shapes=[pltpu.VMEM((n_tiles, LANES, LANES), F32),
                            pltpu.VMEM((RET_TILE, width), F32)],
            compiler_params=params(dimension_semantics=("arbitrary", "arbitrary")),
            name="retention",
        )(rq, rk, rv, rg, dmat, qdec, kdec, cdec, bdmask, group_ones,
          ret_norm_g[l].reshape(1, width))

        lam_spec = _resident((1, DIFF_HD))
        dif = pl.pallas_call(
            functools.partial(_diffattn_kernel, qblock=ATT_QBLOCK, block=ATT_BLOCK, strip=ATT_STRIP,
                              lead=ATT_LEAD),
            grid=(B, n_diff_heads, S // ATT_QBLOCK),
            in_specs=[
                pl.BlockSpec((1, DIFF_VD, ATT_QBLOCK), lambda b, h, i: (b, h, i)),
                pl.BlockSpec((1, S, DIFF_VD), lambda b, h, i: (b, 0, h)),
                pl.BlockSpec((1, nk, DIFF_VD, ATT_BLOCK), lambda b, h, i: (b, 0, h, 0)),
                lam_spec, lam_spec, lam_spec, lam_spec,
                _resident((1, DIFF_VD)),
            ],
            out_specs=pl.BlockSpec((1, ATT_QBLOCK, DIFF_VD), lambda b, h, i: (b, i, h)),
            out_shape=act,
            scratch_shapes=[pltpu.VMEM((ATT_RING, ATT_BLOCK, ATT_STRIP), F32),
                            pltpu.VMEM((ATT_RING, 1, ATT_STRIP), F32),
                            pltpu.VMEM((2, 1, ATT_QBLOCK), F32),
                            pltpu.VMEM((2, 1, ATT_QBLOCK), F32),
                            pltpu.VMEM((2, DIFF_VD, ATT_QBLOCK), F32)],
            compiler_params=params(dimension_semantics=("parallel", "parallel", "arbitrary")),
            name="diffattn",
        )(dqt, dk, dvt, lambda_q1[l][None, :], lambda_k1[l][None, :], lambda_q2[l][None, :],
          lambda_k2[l][None, :], diff_subln_g[l][None, :])

        tile = pl.BlockSpec((1, ROW_TILE, D), row)
        half = pl.BlockSpec((1, ROW_TILE, width), row)
        x = pl.pallas_call(
            functools.partial(_out_ffn_kernel, ret_width=width, ff_chunks=_ff_chunks(d_ff)),
            grid=(B, S // ROW_TILE),
            in_specs=[tile, half, half,
                      _resident((2 * width, D)), _resident((1, D)),
                      _resident((D, d_ff)), _resident((D, d_ff)), _resident((d_ff, D))],
            out_specs=tile,
            out_shape=jax.ShapeDtypeStruct((B, S, D), F32),
            compiler_params=params(dimension_semantics=("parallel", "parallel")),
            name="out_ffn",
        )(x, ret, dif, w_out[l].astype(BF16), norm2_g[l][None, :],
          w_gate[l].astype(BF16), w_up[l].astype(BF16), w_down[l].astype(BF16))
    return x
```

```python
import functools
import math

import jax
import jax.numpy as jnp
import numpy as np
from jax import lax
from jax.experimental import pallas as pl
from jax.experimental.pallas import tpu as pltpu

F32 = jnp.float32
BF16 = jnp.bfloat16

LANES = 128
VMEM_LIMIT_BYTES = 56 * 1024 * 1024

RET_HD = 64
DIFF_HD = 64
DIFF_VD = 2 * DIFF_HD
HEADS_PER_TILE = LANES // RET_HD
RET_CHUNK = 256
ROPE_BASE = 10000.0
NORM_EPS = 1e-6
LAMBDA_INIT = 0.8 - 0.6 * math.exp(-0.3 * 0)
MASK_VALUE = -1e30
LOG2E = 1.0 / math.log(2.0)

ROW_TILE = 512
RET_TILE = 512
ATT_QBLOCK = 2048
ATT_BLOCK = 512
ATT_STRIP = 256
ATT_RING = 16
ATT_LEAD = 4


def _dot(a, b):
    return jnp.dot(a, b, preferred_element_type=F32)


def _dot_nt(a, b):
    return lax.dot_general(a, b, (((1,), (1,)), ((), ())), preferred_element_type=F32)


def _dot_tn(a, b):
    return lax.dot_general(a, b, (((0,), (0,)), ((), ())), preferred_element_type=F32)


def _group_sum(sq, group_ones):
    return _dot(sq.astype(BF16), group_ones)


def _inproj_kernel(x_ref, g1_ref, w_ref, cos_ref, sin_ref, gq_ref, gk_ref, ones_ref,
                   rq_ref, rk_ref, rv_ref, rg_ref, dqt_ref, dk_ref, dvt_ref, *, width, kv_block):
    x = x_ref[0]
    tm = x.shape[0]
    ms = jnp.mean(x * x, axis=-1, keepdims=True)
    h = (x * lax.rsqrt(ms + NORM_EPS) * g1_ref[...]).astype(BF16)

    def proj(j):
        return _dot(h, w_ref[:, j * width:(j + 1) * width])

    n_tiles = width // LANES
    cos = cos_ref[...]
    sin = sin_ref[...]
    lane = lax.broadcasted_iota(jnp.int32, (tm, LANES), 1)
    first_half = (lane % RET_HD) < (RET_HD // 2)

    def rotary(t):
        partner = jnp.where(first_half, pltpu.roll(t, LANES - RET_HD // 2, 1),
                            pltpu.roll(t, RET_HD // 2, 1))
        return t * cos + partner * sin

    rq = proj(0)
    for t in range(n_tiles):
        sl = slice(t * LANES, (t + 1) * LANES)
        rq_ref[0, :, sl] = rotary(rq[:, sl]).astype(BF16)
    rk = proj(1)
    for t in range(n_tiles):
        sl = slice(t * LANES, (t + 1) * LANES)
        rk_ref[0, :, sl] = (rotary(rk[:, sl]) * (RET_HD ** -0.5)).astype(BF16)
    rv_ref[0] = proj(2).astype(BF16)
    rg_ref[0] = proj(3).astype(BF16)

    ones = ones_ref[...]

    def qk_norm(t, g):
        ssq = _group_sum(t * t, ones)
        return t * lax.rsqrt(ssq * (1.0 / DIFF_HD) + NORM_EPS) * g

    dq = proj(4)
    q_scale = (DIFF_HD ** -0.5) * LOG2E
    for t in range(n_tiles):
        sl = slice(t * LANES, (t + 1) * LANES)
        qn = qk_norm(dq[:, sl], gq_ref[...]) * q_scale
        dqt_ref[0, sl, :] = qn.T.astype(BF16)
    dk = proj(5)
    for t in range(n_tiles):
        sl = slice(t * LANES, (t + 1) * LANES)
        dk_ref[0, :, sl] = qk_norm(dk[:, sl], gk_ref[...]).astype(BF16)
    dv = proj(6)
    for t in range(n_tiles):
        sl = slice(t * LANES, (t + 1) * LANES)
        vt = dv[:, sl].T.astype(BF16)
        for j in range(tm // kv_block):
            dvt_ref[0, j, sl, :] = vt[:, j * kv_block:(j + 1) * kv_block]


def _retention_kernel(q_ref, k_ref, v_ref, gate_ref, dmat_ref, qdec_ref, kdec_ref, cdec_ref,
                      bdmask_ref, ones_ref, gain_ref, o_ref, state_ref, acc_ref, *, n_tiles, n_chunks):
    @pl.when(pl.program_id(1) == 0)
    def _():
        state_ref[...] = jnp.zeros_like(state_ref)

    c = RET_CHUNK
    lane = lax.broadcasted_iota(jnp.int32, (c, LANES), 1)
    head_a = lane < RET_HD
    bdmask = bdmask_ref[...]

    for p in range(n_tiles):
        sl = slice(p * LANES, (p + 1) * LANES)
        kvs = []
        for ci in range(n_chunks):
            rows = slice(ci * c, (ci + 1) * c)
            q2 = q_ref[0, rows, sl]
            k2 = k_ref[0, rows, sl]
            v2 = v_ref[0, rows, sl]
            zero = jnp.zeros_like(q2)
            qs = jnp.concatenate([jnp.where(head_a, q2, zero), jnp.where(head_a, zero, q2)], axis=0)
            scores = _dot_nt(qs, k2)
            inner = (scores * dmat_ref[p]).astype(BF16)
            intra2 = _dot(inner, v2)
            acc_ref[rows, sl] = jnp.where(head_a, intra2[:c], intra2[c:])
            kd = (k2.astype(F32) * kdec_ref[:, sl]).astype(BF16)
            kvs.append(_dot_tn(kd, v2) * bdmask)
        state = state_ref[p]
        for ci in range(n_chunks):
            rows = slice(ci * c, (ci + 1) * c)
            qd = (q_ref[0, rows, sl].astype(F32) * qdec_ref[:, sl]).astype(BF16)
            acc_ref[rows, sl] += _dot(qd, state.astype(BF16))
            state = state * cdec_ref[:, sl] + kvs[ci]
        state_ref[p] = state

        o = acc_ref[:, sl]
        ssq = _group_sum(o * o, ones_ref[...])
        y = o * lax.rsqrt(ssq * (1.0 / RET_HD) + NORM_EPS) * gain_ref[:, sl]
        g = gate_ref[0, :, sl].astype(F32)
        o_ref[0, :, sl] = (y * (g / (1.0 + jnp.exp(-g)))).astype(BF16)


def _diffattn_kernel(qt_ref, k_ref, vt_ref, lq1_ref, lk1_ref, lq2_ref, lk2_ref, gain_ref,
                     o_ref, s_ref, smax_ref, m_ref, l_ref, acc_ref, *, qblock, block, strip, lead):
    qi = pl.program_id(2)
    ring = s_ref.shape[0]
    n_strips = qblock // strip
    chains = [(mi, t) for mi in range(2) for t in range(n_strips)]
    n_chains = len(chains)
    per_q = qblock // block
    qt = qt_ref[0]
    row = lax.broadcasted_iota(jnp.int32, qt.shape, 0)
    zero = jnp.zeros_like(qt)
    q_maps = (jnp.where(row < DIFF_HD, qt, zero), jnp.where(row < DIFF_HD, zero, qt))

    m_ref[...] = jnp.full_like(m_ref, MASK_VALUE)
    l_ref[...] = jnp.zeros_like(l_ref)
    acc_ref[...] = jnp.zeros_like(acc_ref)

    def scores(kj, c, slot, nkeys=block):
        mi, t = chains[c]
        start = pl.multiple_of(kj * block, block)
        k = k_ref[0, pl.ds(start, nkeys), :]
        sc = _dot(k, q_maps[mi][:, t * strip:(t + 1) * strip])
        s_ref[slot, :nkeys, :] = sc
        smax_ref[slot] = jnp.max(sc, axis=0, keepdims=True)

    def softmax_pv(kj, c, slot, nkeys=block, diag=None):
        mi, t = chains[c]
        cols = slice(t * strip, (t + 1) * strip)
        s = s_ref[slot, :nkeys, :]
        if diag is not None:
            kpos = lax.broadcasted_iota(jnp.int32, (nkeys, strip), 0) + diag * block
            qpos = lax.broadcasted_iota(jnp.int32, (nkeys, strip), 1) + t * strip
            s = jnp.where(kpos <= qpos, s, MASK_VALUE)
            s_max = jnp.max(s, axis=0, keepdims=True)
        else:
            s_max = smax_ref[slot]
        m_old = m_ref[mi, :, cols]
        m_new = jnp.maximum(m_old, s_max)
        alpha = jnp.exp2(m_old - m_new)
        p = jnp.exp2(s - m_new)
        l_ref[mi, :, cols] = alpha * l_ref[mi, :, cols] + jnp.sum(p, axis=0, keepdims=True)
        acc_ref[mi, :, cols] = (alpha * acc_ref[mi, :, cols]
                                + _dot(vt_ref[0, kj, :, :nkeys], p.astype(BF16)))
        m_ref[mi, :, cols] = m_new

    for c in range(lead):
        scores(0, c, c)

    trip = 2 * n_chains

    def body(t, carry):
        for j in range(trip):
            nxt = j + lead
            scores(2 * t + nxt // n_chains, nxt % n_chains, nxt % ring)
            softmax_pv(2 * t + j // n_chains, j % n_chains, j % ring)
        return carry

    n_full = qi * per_q
    lax.fori_loop(0, n_full // 2, body, 0)

    tail = []
    for d in range(per_q):
        for c, (mi, t) in enumerate(chains):
            nkeys = min(block, (t + 1) * strip - d * block)
            if nkeys > 0:
                masked = t * strip < d * block + nkeys - 1
                tail.append((d, c, nkeys, d if masked else None))
    for i, (d, c, nkeys, diag) in enumerate(tail):
        if i < lead:
            assert d == 0 and c == i
        if i + lead < len(tail):
            d2, c2, nkeys2, _ = tail[i + lead]
            scores(n_full + d2, c2, (i + lead) % ring, nkeys2)
        softmax_pv(n_full + d, c, i % ring, nkeys, diag)

    lam = (jnp.exp(jnp.sum(lq1_ref[...] * lk1_ref[...], axis=-1, keepdims=True))
           - jnp.exp(jnp.sum(lq2_ref[...] * lk2_ref[...], axis=-1, keepdims=True))
           + LAMBDA_INIT)
    o = acc_ref[0] / l_ref[0] - lam * (acc_ref[1] / l_ref[1])
    ms = jnp.mean(o * o, axis=0, keepdims=True)
    o = o * lax.rsqrt(ms + NORM_EPS)
    o_ref[0] = (o.T * gain_ref[...] * (1.0 - LAMBDA_INIT)).astype(BF16)


def _out_ffn_kernel(x_ref, ret_ref, dif_ref, wo_ref, g2_ref, wg_ref, wu_ref, wd_ref, o_ref, *,
                    ret_width, ff_chunks):
    x1 = (x_ref[0] + _dot(ret_ref[0], wo_ref[:ret_width, :])
          + _dot(dif_ref[0], wo_ref[ret_width:, :]))
    ms = jnp.mean(x1 * x1, axis=-1, keepdims=True)
    h = (x1 * lax.rsqrt(ms + NORM_EPS) * g2_ref[...]).astype(BF16)
    o_ref[0] = x1
    ff = None
    for lo, hi in ff_chunks:
        g = _dot(h, wg_ref[:, lo:hi])
        u = _dot(h, wu_ref[:, lo:hi])
        a = (g / (1.0 + jnp.exp(-g)) * u).astype(BF16)
        d = _dot(a, wd_ref[lo:hi, :])
        ff = d if ff is None else ff + d
    o_ref[0] += ff


def _resident(shape):
    return pl.BlockSpec(shape, lambda *_: (0,) * len(shape), pipeline_mode=pl.Buffered(1))


def _ff_chunks(d_ff, mxu_cols=256, target=1536):
    chunks, lo = [], 0
    while lo < d_ff:
        hi = min(d_ff, lo + target)
        if d_ff - hi and (d_ff - hi) < mxu_cols:
            hi = d_ff
        chunks.append((lo, hi))
        lo = hi
    return tuple(chunks)


def kernel(x, norm1_g, w_in, ret_norm_g, diff_q_norm_g, diff_k_norm_g, lambda_q1, lambda_k1,
           lambda_q2, lambda_k2, diff_subln_g, w_out, norm2_g, w_gate, w_up, w_down):
    B, S, D = x.shape
    depth = w_in.shape[0]
    n_ret_heads = ret_norm_g.shape[1]
    ret_width = n_ret_heads * RET_HD
    in_width = w_in.shape[2]
    width = ret_width
    assert in_width == 7 * width and w_out.shape[1] == 2 * width
    assert width % LANES == 0 and S % ROW_TILE == 0 and S % RET_TILE == 0 and S % ATT_BLOCK == 0
    assert ROW_TILE % ATT_BLOCK == 0 and RET_TILE % RET_CHUNK == 0
    assert S % ATT_QBLOCK == 0 and (ATT_QBLOCK // ATT_BLOCK) % 2 == 0 and ATT_QBLOCK % ATT_STRIP == 0
    assert (4 * ATT_QBLOCK // ATT_STRIP) % ATT_RING == 0 and ATT_LEAD < ATT_RING
    assert ATT_LEAD <= 2 * ATT_QBLOCK // ATT_STRIP
    n_diff_heads = width // DIFF_VD
    n_tiles = width // LANES
    d_ff = w_gate.shape[2]
    nk = S // ATT_BLOCK
    params = functools.partial(pltpu.CompilerParams, vmem_limit_bytes=VMEM_LIMIT_BYTES)

    pos = np.arange(S, dtype=np.float64)
    freqs = 1.0 / (ROPE_BASE ** (np.arange(0, RET_HD, 2, dtype=np.float64) / RET_HD))
    ang = pos[:, None] * freqs[None, :]
    cos_t = jnp.asarray(np.tile(np.cos(ang), (1, LANES // (RET_HD // 2))), F32)
    sin_half = np.sin(ang)
    sin_t = jnp.asarray(np.tile(np.concatenate([-sin_half, sin_half], axis=-1), (1, HEADS_PER_TILE)), F32)
    grp = np.arange(LANES) // RET_HD
    same_group = (grp[:, None] == grp[None, :]).astype(np.float32)
    group_ones = jnp.asarray(same_group, BF16)
    bdmask = jnp.asarray(same_group, F32)

    log_g = np.log(1.0 - 2.0 ** (-5.0 - np.arange(n_ret_heads, dtype=np.float64)))
    idx = np.arange(RET_CHUNK, dtype=np.float64)
    rel = idx[:, None] - idx[None, :]
    decay_in = np.where(rel[None] >= 0, np.exp(log_g[:, None, None] * np.maximum(rel, 0.0)[None]), 0.0)
    dmat = jnp.asarray(decay_in.reshape(n_tiles, HEADS_PER_TILE * RET_CHUNK, RET_CHUNK), F32)
    by_lane = lambda t: jnp.asarray(np.repeat(t, RET_HD, axis=0).T, F32)
    qdec = by_lane(np.exp(log_g[:, None] * (idx[None, :] + 1.0)))
    kdec = by_lane(np.exp(log_g[:, None] * (RET_CHUNK - 1.0 - idx[None, :])))
    cdec = by_lane(np.exp(log_g * RET_CHUNK)[:, None])

    for l in range(depth):
        row = lambda b, i: (b, i, 0)
        col = lambda b, i: (b, 0, i)
        tab = lambda b, i: (i, 0)
        act = jax.ShapeDtypeStruct((B, S, width), BF16)
        act_t = jax.ShapeDtypeStruct((B, width, S), BF16)
        rq, rk, rv, rg, dqt, dk, dvt = pl.pallas_call(
            functools.partial(_inproj_kernel, width=width, kv_block=ATT_BLOCK),
            grid=(B, S // ROW_TILE),
            in_specs=[
                pl.BlockSpec((1, ROW_TILE, D), row),
                _resident((1, D)),
                _resident((D, in_width)),
                pl.BlockSpec((ROW_TILE, LANES), tab),
                pl.BlockSpec((ROW_TILE, LANES), tab),
                _resident((1, LANES)),
                _resident((1, LANES)),
                _resident((LANES, LANES)),
            ],
            out_specs=[
                pl.BlockSpec((1, ROW_TILE, width), row),
                pl.BlockSpec((1, ROW_TILE, width), row),
                pl.BlockSpec((1, ROW_TILE, width), row),
                pl.BlockSpec((1, ROW_TILE, width), row),
                pl.BlockSpec((1, width, ROW_TILE), col),
                pl.BlockSpec((1, ROW_TILE, width), row),
                pl.BlockSpec((1, ROW_TILE // ATT_BLOCK, width, ATT_BLOCK), lambda b, i: (b, i, 0, 0)),
            ],
            out_shape=[act, act, act, act, act_t, act,
                       jax.ShapeDtypeStruct((B, nk, width, ATT_BLOCK), BF16)],
            compiler_params=params(dimension_semantics=("parallel", "parallel")),
            name="inproj",
        )(x, norm1_g[l][None, :], w_in[l].astype(BF16), cos_t, sin_t,
          jnp.tile(diff_q_norm_g[l], LANES // DIFF_HD)[None, :],
          jnp.tile(diff_k_norm_g[l], LANES // DIFF_HD)[None, :], group_ones)

        seq = pl.BlockSpec((1, RET_TILE, width), row)
        ret = pl.pallas_call(
            functools.partial(_retention_kernel, n_tiles=n_tiles, n_chunks=RET_TILE // RET_CHUNK),
            grid=(B, S // RET_TILE),
            in_specs=[seq, seq, seq, seq,
                      _resident(dmat.shape), _resident(qdec.shape), _resident(kdec.shape),
                      _resident(cdec.shape), _resident(bdmask.shape), _resident(group_ones.shape),
                      _resident((1, width))],
            out_specs=seq,
            out_shape=act,
            scratch_shapes=[pltpu.VMEM((n_tiles, LANES, LANES), F32),
                            pltpu.VMEM((RET_TILE, width), F32)],
            compiler_params=params(dimension_semantics=("arbitrary", "arbitrary")),
            name="retention",
        )(rq, rk, rv, rg, dmat, qdec, kdec, cdec, bdmask, group_ones,
          ret_norm_g[l].reshape(1, width))

        lam_spec = _resident((1, DIFF_HD))
        dif = pl.pallas_call(
            functools.partial(_diffattn_kernel, qblock=ATT_QBLOCK, block=ATT_BLOCK, strip=ATT_STRIP,
                              lead=ATT_LEAD),
            grid=(B, n_diff_heads, S // ATT_QBLOCK),
            in_specs=[
                pl.BlockSpec((1, DIFF_VD, ATT_QBLOCK), lambda b, h, i: (b, h, i)),
                pl.BlockSpec((1, S, DIFF_VD), lambda b, h, i: (b, 0, h)),
                pl.BlockSpec((1, nk, DIFF_VD, ATT_BLOCK), lambda b, h, i: (b, 0, h, 0)),
                lam_spec, lam_spec, lam_spec, lam_spec,
                _resident((1, DIFF_VD)),
            ],
            out_specs=pl.BlockSpec((1, ATT_QBLOCK, DIFF_VD), lambda b, h, i: (b, i, h)),
            out_shape=act,
            scratch_shapes=[pltpu.VMEM((ATT_RING, ATT_BLOCK, ATT_STRIP), F32),
                            pltpu.VMEM((ATT_RING, 1, ATT_STRIP), F32),
                            pltpu.VMEM((2, 1, ATT_QBLOCK), F32),
                            pltpu.VMEM((2, 1, ATT_QBLOCK), F32),
                            pltpu.VMEM((2, DIFF_VD, ATT_QBLOCK), F32)],
            compiler_params=params(dimension_semantics=("parallel", "parallel", "arbitrary")),
            name="diffattn",
        )(dqt, dk, dvt, lambda_q1[l][None, :], lambda_k1[l][None, :], lambda_q2[l][None, :],
          lambda_k2[l][None, :], diff_subln_g[l][None, :])

        tile = pl.BlockSpec((1, ROW_TILE, D), row)
        half = pl.BlockSpec((1, ROW_TILE, width), row)
        x = pl.pallas_call(
            functools.partial(_out_ffn_kernel, ret_width=width, ff_chunks=_ff_chunks(d_ff)),
            grid=(B, S // ROW_TILE),
            in_specs=[tile, half, half,
                      _resident((2 * width, D)), _resident((1, D)),
                      _resident((D, d_ff)), _resident((D, d_ff)), _resident((d_ff, D))],
            out_specs=tile,
            out_shape=jax.ShapeDtypeStruct((B, S, D), F32),
            compiler_params=params(dimension_semantics=("parallel", "parallel")),
            name="out_ffn",
        )(x, ret, dif, w_out[l].astype(BF16), norm2_g[l][None, :],
          w_gate[l].astype(BF16), w_up[l].astype(BF16), w_down[l].astype(BF16))
    return x
```

```python
import functools
import math

import jax
import jax.numpy as jnp
import numpy as np
from jax import lax
from jax.experimental import pallas as pl
from jax.experimental.pallas import tpu as pltpu

F32 = jnp.float32
BF16 = jnp.bfloat16

LANES = 128
VMEM_LIMIT_BYTES = 56 * 1024 * 1024

RET_HD = 64
DIFF_HD = 64
DIFF_VD = 2 * DIFF_HD
HEADS_PER_TILE = LANES // RET_HD
RET_CHUNK = 256
ROPE_BASE = 10000.0
NORM_EPS = 1e-6
LAMBDA_INIT = 0.8 - 0.6 * math.exp(-0.3 * 0)
MASK_VALUE = -1e30
LOG2E = 1.0 / math.log(2.0)

ROW_TILE = 512
ATT_QBLOCK = 2048
ATT_BLOCK = 512
ATT_STRIP = 256
ATT_RING = 16
ATT_LEAD = 4


def _dot(a, b):
    return jnp.dot(a, b, preferred_element_type=F32)


def _dot_nt(a, b):
    return lax.dot_general(a, b, (((1,), (1,)), ((), ())), preferred_element_type=F32)


def _dot_tn(a, b):
    return lax.dot_general(a, b, (((0,), (0,)), ((), ())), preferred_element_type=F32)


def _group_sum(sq, group_ones):
    return _dot(sq.astype(BF16), group_ones)


def _inproj_retention_kernel(x_ref, g1_ref, w_ref, cos_ref, sin_ref, gq_ref, gk_ref, ones_ref,
                             dmat_ref, qdec_ref, kdec_ref, cdec_ref, bdmask_ref, gain_ref,
                             ret_ref, dqt_ref, dk_ref, dvt_ref,
                             state_ref, rq_ref, rk_ref, rv_ref, rg_ref, acc_ref, *, width, kv_block):
    @pl.when(pl.program_id(1) == 0)
    def _():
        state_ref[...] = jnp.zeros_like(state_ref)

    x = x_ref[0]
    tm = x.shape[0]
    ms = jnp.mean(x * x, axis=-1, keepdims=True)
    h = (x * lax.rsqrt(ms + NORM_EPS) * g1_ref[...]).astype(BF16)

    def proj(j):
        return _dot(h, w_ref[:, j * width:(j + 1) * width])

    n_tiles = width // LANES
    tiles = [slice(t * LANES, (t + 1) * LANES) for t in range(n_tiles)]
    cos = cos_ref[...]
    sin = sin_ref[...]
    lane = lax.broadcasted_iota(jnp.int32, (tm, LANES), 1)
    first_half = (lane % RET_HD) < (RET_HD // 2)

    def rotary(t):
        partner = jnp.where(first_half, pltpu.roll(t, LANES - RET_HD // 2, 1),
                            pltpu.roll(t, RET_HD // 2, 1))
        return t * cos + partner * sin

    rq = proj(0)
    for sl in tiles:
        rq_ref[:, sl] = rotary(rq[:, sl]).astype(BF16)
    rk = proj(1)
    for sl in tiles:
        rk_ref[:, sl] = (rotary(rk[:, sl]) * (RET_HD ** -0.5)).astype(BF16)
    rv_ref[...] = proj(2).astype(BF16)
    rg_ref[...] = proj(3).astype(BF16)

    c = RET_CHUNK
    n_chunks = tm // c
    head_a = lax.broadcasted_iota(jnp.int32, (c, LANES), 1) < RET_HD
    bdmask = bdmask_ref[...]
    ones = ones_ref[...]

    def retention(p):
        sl = tiles[p]
        kvs = []
        for ci in range(n_chunks):
            rows = slice(ci * c, (ci + 1) * c)
            q2 = rq_ref[rows, sl]
            k2 = rk_ref[rows, sl]
            v2 = rv_ref[rows, sl]
            zero = jnp.zeros_like(q2)
            qs = jnp.concatenate([jnp.where(head_a, q2, zero), jnp.where(head_a, zero, q2)], axis=0)
            scores = _dot_nt(qs, k2)
            inner = (scores * dmat_ref[p]).astype(BF16)
            intra2 = _dot(inner, v2)
            acc_ref[rows, sl] = jnp.where(head_a, intra2[:c], intra2[c:])
            kd = (k2.astype(F32) * kdec_ref[:, sl]).astype(BF16)
            kvs.append(_dot_tn(kd, v2) * bdmask)
        state = state_ref[p]
        for ci in range(n_chunks):
            rows = slice(ci * c, (ci + 1) * c)
            qd = (rq_ref[rows, sl].astype(F32) * qdec_ref[:, sl]).astype(BF16)
            acc_ref[rows, sl] += _dot(qd, state.astype(BF16))
            state = state * cdec_ref[:, sl] + kvs[ci]
        state_ref[p] = state

        o = acc_ref[:, sl]
        ssq = _group_sum(o * o, ones)
        y = o * lax.rsqrt(ssq * (1.0 / RET_HD) + NORM_EPS) * gain_ref[:, sl]
        g = rg_ref[:, sl].astype(F32)
        ret_ref[0, :, sl] = (y * (g / (1.0 + jnp.exp(-g)))).astype(BF16)

    def qk_norm(t, g):
        ssq = _group_sum(t * t, ones)
        return t * lax.rsqrt(ssq * (1.0 / DIFF_HD) + NORM_EPS) * g

    def diff_q():
        dq = proj(4)
        q_scale = (DIFF_HD ** -0.5) * LOG2E
        for sl in tiles:
            qn = qk_norm(dq[:, sl], gq_ref[...]) * q_scale
            dqt_ref[0, sl, :] = qn.T.astype(BF16)

    def diff_k():
        dk = proj(5)
        for sl in tiles:
            dk_ref[0, :, sl] = qk_norm(dk[:, sl], gk_ref[...]).astype(BF16)

    def diff_v():
        dv = proj(6)
        for sl in tiles:
            vt = dv[:, sl].T.astype(BF16)
            for j in range(tm // kv_block):
                dvt_ref[0, j, sl, :] = vt[:, j * kv_block:(j + 1) * kv_block]

    others = [diff_q, diff_k, diff_v]
    for p in range(n_tiles):
        retention(p)
        if p < len(others):
            others[p]()
    for f in others[n_tiles:]:
        f()


def _diffattn_kernel(qt_ref, k_ref, vt_ref, lq1_ref, lk1_ref, lq2_ref, lk2_ref, gain_ref,
                     o_ref, s_ref, smax_ref, m_ref, l_ref, acc_ref, *, qblock, block, strip, lead):
    qi = pl.program_id(2)
    ring = s_ref.shape[0]
    n_strips = qblock // strip
    chains = [(mi, t) for mi in range(2) for t in range(n_strips)]
    n_chains = len(chains)
    per_q = qblock // block
    qt = qt_ref[0]
    row = lax.broadcasted_iota(jnp.int32, qt.shape, 0)
    zero = jnp.zeros_like(qt)
    q_maps = (jnp.where(row < DIFF_HD, qt, zero), jnp.where(row < DIFF_HD, zero, qt))

    m_ref[...] = jnp.full_like(m_ref, MASK_VALUE)
    l_ref[...] = jnp.zeros_like(l_ref)
    acc_ref[...] = jnp.zeros_like(acc_ref)

    def scores(kj, c, slot, nkeys=block):
        mi, t = chains[c]
        start = pl.multiple_of(kj * block, block)
        k = k_ref[0, pl.ds(start, nkeys), :]
        sc = _dot(k, q_maps[mi][:, t * strip:(t + 1) * strip])
        s_ref[slot, :nkeys, :] = sc
        smax_ref[slot] = jnp.max(sc, axis=0, keepdims=True)

    def softmax_pv(kj, c, slot, nkeys=block, diag=None):
        mi, t = chains[c]
        cols = slice(t * strip, (t + 1) * strip)
        s = s_ref[slot, :nkeys, :]
        if diag is not None:
            kpos = lax.broadcasted_iota(jnp.int32, (nkeys, strip), 0) + diag * block
            qpos = lax.broadcasted_iota(jnp.int32, (nkeys, strip), 1) + t * strip
            s = jnp.where(kpos <= qpos, s, MASK_VALUE)
            s_max = jnp.max(s, axis=0, keepdims=True)
        else:
            s_max = smax_ref[slot]
        m_old = m_ref[mi, :, cols]
        m_new = jnp.maximum(m_old, s_max)
        alpha = jnp.exp2(m_old - m_new)
        p = jnp.exp2(s - m_new)
        l_ref[mi, :, cols] = alpha * l_ref[mi, :, cols] + jnp.sum(p, axis=0, keepdims=True)
        acc_ref[mi, :, cols] = (alpha * acc_ref[mi, :, cols]
                                + _dot(vt_ref[0, kj, :, :nkeys], p.astype(BF16)))
        m_ref[mi, :, cols] = m_new

    for c in range(lead):
        scores(0, c, c)

    trip = 2 * n_chains

    def body(t, carry):
        for j in range(trip):
            nxt = j + lead
            scores(2 * t + nxt // n_chains, nxt % n_chains, nxt % ring)
            softmax_pv(2 * t + j // n_chains, j % n_chains, j % ring)
        return carry

    n_full = qi * per_q
    lax.fori_loop(0, n_full // 2, body, 0)

    tail = []
    for d in range(per_q):
        for c, (mi, t) in enumerate(chains):
            nkeys = min(block, (t + 1) * strip - d * block)
            if nkeys > 0:
                masked = t * strip < d * block + nkeys - 1
                tail.append((d, c, nkeys, d if masked else None))
    for i, (d, c, nkeys, diag) in enumerate(tail):
        if i < lead:
            assert d == 0 and c == i
        if i + lead < len(tail):
            d2, c2, nkeys2, _ = tail[i + lead]
            scores(n_full + d2, c2, (i + lead) % ring, nkeys2)
        softmax_pv(n_full + d, c, i % ring, nkeys, diag)

    lam = (jnp.exp(jnp.sum(lq1_ref[...] * lk1_ref[...], axis=-1, keepdims=True))
           - jnp.exp(jnp.sum(lq2_ref[...] * lk2_ref[...], axis=-1, keepdims=True))
           + LAMBDA_INIT)
    o = acc_ref[0] / l_ref[0] - lam * (acc_ref[1] / l_ref[1])
    ms = jnp.mean(o * o, axis=0, keepdims=True)
    o = o * lax.rsqrt(ms + NORM_EPS)
    o_ref[0] = (o.T * gain_ref[...] * (1.0 - LAMBDA_INIT)).astype(BF16)


def _out_ffn_kernel(x_ref, ret_ref, dif_ref, wo_ref, g2_ref, wg_ref, wu_ref, wd_ref, o_ref, *,
                    ret_width, ff_chunks):
    x1 = (x_ref[0] + _dot(ret_ref[0], wo_ref[:ret_width, :])
          + _dot(dif_ref[0], wo_ref[ret_width:, :]))
    ms = jnp.mean(x1 * x1, axis=-1, keepdims=True)
    h = (x1 * lax.rsqrt(ms + NORM_EPS) * g2_ref[...]).astype(BF16)
    o_ref[0] = x1
    ff = None
    for lo, hi in ff_chunks:
        g = _dot(h, wg_ref[:, lo:hi])
        u = _dot(h, wu_ref[:, lo:hi])
        a = (g / (1.0 + jnp.exp(-g)) * u).astype(BF16)
        d = _dot(a, wd_ref[lo:hi, :])
        ff = d if ff is None else ff + d
    o_ref[0] += ff


def _resident(shape):
    return pl.BlockSpec(shape, lambda *_: (0,) * len(shape), pipeline_mode=pl.Buffered(1))


def _ff_chunks(d_ff, mxu_cols=256, target=1536):
    chunks, lo = [], 0
    while lo < d_ff:
        hi = min(d_ff, lo + target)
        if d_ff - hi and (d_ff - hi) < mxu_cols:
            hi = d_ff
        chunks.append((lo, hi))
        lo = hi
    return tuple(chunks)


def kernel(x, norm1_g, w_in, ret_norm_g, diff_q_norm_g, diff_k_norm_g, lambda_q1, lambda_k1,
           lambda_q2, lambda_k2, diff_subln_g, w_out, norm2_g, w_gate, w_up, w_down):
    B, S, D = x.shape
    depth = w_in.shape[0]
    n_ret_heads = ret_norm_g.shape[1]
    ret_width = n_ret_heads * RET_HD
    in_width = w_in.shape[2]
    width = ret_width
    assert in_width == 7 * width and w_out.shape[1] == 2 * width
    assert width % LANES == 0 and S % ROW_TILE == 0 and S % ATT_BLOCK == 0
    assert ROW_TILE % ATT_BLOCK == 0 and ROW_TILE % RET_CHUNK == 0
    assert S % ATT_QBLOCK == 0 and (ATT_QBLOCK // ATT_BLOCK) % 2 == 0 and ATT_QBLOCK % ATT_STRIP == 0
    assert (4 * ATT_QBLOCK // ATT_STRIP) % ATT_RING == 0 and ATT_LEAD < ATT_RING
    assert ATT_LEAD <= 2 * ATT_QBLOCK // ATT_STRIP
    n_diff_heads = width // DIFF_VD
    n_tiles = width // LANES
    d_ff = w_gate.shape[2]
    nk = S // ATT_BLOCK
    params = functools.partial(pltpu.CompilerParams, vmem_limit_bytes=VMEM_LIMIT_BYTES)

    pos = np.arange(S, dtype=np.float64)
    freqs = 1.0 / (ROPE_BASE ** (np.arange(0, RET_HD, 2, dtype=np.float64) / RET_HD))
    ang = pos[:, None] * freqs[None, :]
    cos_t = jnp.asarray(np.tile(np.cos(ang), (1, LANES // (RET_HD // 2))), F32)
    sin_half = np.sin(ang)
    sin_t = jnp.asarray(np.tile(np.concatenate([-sin_half, sin_half], axis=-1), (1, HEADS_PER_TILE)), F32)
    grp = np.arange(LANES) // RET_HD
    same_group = (grp[:, None] == grp[None, :]).astype(np.float32)
    group_ones = jnp.asarray(same_group, BF16)
    bdmask = jnp.asarray(same_group, F32)

    log_g = np.log(1.0 - 2.0 ** (-5.0 - np.arange(n_ret_heads, dtype=np.float64)))
    idx = np.arange(RET_CHUNK, dtype=np.float64)
    rel = idx[:, None] - idx[None, :]
    decay_in = np.where(rel[None] >= 0, np.exp(log_g[:, None, None] * np.maximum(rel, 0.0)[None]), 0.0)
    dmat = jnp.asarray(decay_in.reshape(n_tiles, HEADS_PER_TILE * RET_CHUNK, RET_CHUNK), F32)
    by_lane = lambda t: jnp.asarray(np.repeat(t, RET_HD, axis=0).T, F32)
    qdec = by_lane(np.exp(log_g[:, None] * (idx[None, :] + 1.0)))
    kdec = by_lane(np.exp(log_g[:, None] * (RET_CHUNK - 1.0 - idx[None, :])))
    cdec = by_lane(np.exp(log_g * RET_CHUNK)[:, None])

    for l in range(depth):
        row = lambda b, i: (b, i, 0)
        col = lambda b, i: (b, 0, i)
        tab = lambda b, i: (i, 0)
        act = jax.ShapeDtypeStruct((B, S, width), BF16)
        act_t = jax.ShapeDtypeStruct((B, width, S), BF16)
        tile_bf16 = pltpu.VMEM((ROW_TILE, width), BF16)
        ret, dqt, dk, dvt = pl.pallas_call(
            functools.partial(_inproj_retention_kernel, width=width, kv_block=ATT_BLOCK),
            grid=(B, S // ROW_TILE),
            in_specs=[
                pl.BlockSpec((1, ROW_TILE, D), row),
                _resident((1, D)),
                _resident((D, in_width)),
                pl.BlockSpec((ROW_TILE, LANES), tab),
                pl.BlockSpec((ROW_TILE, LANES), tab),
                _resident((1, LANES)),
                _resident((1, LANES)),
                _resident((LANES, LANES)),
                _resident(dmat.shape), _resident(qdec.shape), _resident(kdec.shape),
                _resident(cdec.shape), _resident(bdmask.shape), _resident((1, width)),
            ],
            out_specs=[
                pl.BlockSpec((1, ROW_TILE, width), row),
                pl.BlockSpec((1, width, ROW_TILE), col),
                pl.BlockSpec((1, ROW_TILE, width), row),
                pl.BlockSpec((1, ROW_TILE // ATT_BLOCK, width, ATT_BLOCK), lambda b, i: (b, i, 0, 0)),
            ],
            out_shape=[act, act_t, act, jax.ShapeDtypeStruct((B, nk, width, ATT_BLOCK), BF16)],
            scratch_shapes=[pltpu.VMEM((n_tiles, LANES, LANES), F32),
                            tile_bf16, tile_bf16, tile_bf16, tile_bf16,
                            pltpu.VMEM((ROW_TILE, width), F32)],
            compiler_params=params(dimension_semantics=("arbitrary", "arbitrary")),
            name="inproj_retention",
        )(x, norm1_g[l][None, :], w_in[l].astype(BF16), cos_t, sin_t,
          jnp.tile(diff_q_norm_g[l], LANES // DIFF_HD)[None, :],
          jnp.tile(diff_k_norm_g[l], LANES // DIFF_HD)[None, :], group_ones,
          dmat, qdec, kdec, cdec, bdmask, ret_norm_g[l].reshape(1, width))

        lam_spec = _resident((1, DIFF_HD))
        dif = pl.pallas_call(
            functools.partial(_diffattn_kernel, qblock=ATT_QBLOCK, block=ATT_BLOCK, strip=ATT_STRIP,
                              lead=ATT_LEAD),
            grid=(B, n_diff_heads, S // ATT_QBLOCK),
            in_specs=[
                pl.BlockSpec((1, DIFF_VD, ATT_QBLOCK), lambda b, h, i: (b, h, i)),
                pl.BlockSpec((1, S, DIFF_VD), lambda b, h, i: (b, 0, h)),
                pl.BlockSpec((1, nk, DIFF_VD, ATT_BLOCK), lambda b, h, i: (b, 0, h, 0)),
                lam_spec, lam_spec, lam_spec, lam_spec,
                _resident((1, DIFF_VD)),
            ],
            out_specs=pl.BlockSpec((1, ATT_QBLOCK, DIFF_VD), lambda b, h, i: (b, i, h)),
            out_shape=act,
            scratch_shapes=[pltpu.VMEM((ATT_RING, ATT_BLOCK, ATT_STRIP), F32),
                            pltpu.VMEM((ATT_RING, 1, ATT_STRIP), F32),
                            pltpu.VMEM((2, 1, ATT_QBLOCK), F32),
                            pltpu.VMEM((2, 1, ATT_QBLOCK), F32),
                            pltpu.VMEM((2, DIFF_VD, ATT_QBLOCK), F32)],
            compiler_params=params(dimension_semantics=("parallel", "parallel", "arbitrary")),
            name="diffattn",
        )(dqt, dk, dvt, lambda_q1[l][None, :], lambda_k1[l][None, :], lambda_q2[l][None, :],
          lambda_k2[l][None, :], diff_subln_g[l][None, :])

        tile = pl.BlockSpec((1, ROW_TILE, D), row)
        half = pl.BlockSpec((1, ROW_TILE, width), row)
        x = pl.pallas_call(
            functools.partial(_out_ffn_kernel, ret_width=width, ff_chunks=_ff_chunks(d_ff)),
            grid=(B, S // ROW_TILE),
            in_specs=[tile, half, half,
                      _resident((2 * width, D)), _resident((1, D)),
                      _resident((D, d_ff)), _resident((D, d_ff)), _resident((d_ff, D))],
            out_specs=tile,
            out_shape=jax.ShapeDtypeStruct((B, S, D), F32),
            compiler_params=params(dimension_semantics=("parallel", "parallel")),
            name="out_ffn",
        )(x, ret, dif, w_out[l].astype(BF16), norm2_g[l][None, :],
          w_gate[l].astype(BF16), w_up[l].astype(BF16), w_down[l].astype(BF16))
    return x
```

```python
import functools
import math

import jax
import jax.numpy as jnp
import numpy as np
from jax import lax
from jax.experimental import pallas as pl
from jax.experimental.pallas import tpu as pltpu

F32 = jnp.float32
BF16 = jnp.bfloat16

LANES = 128
VMEM_LIMIT_BYTES = 56 * 1024 * 1024

RET_HD = 64
DIFF_HD = 64
DIFF_VD = 2 * DIFF_HD
HEADS_PER_TILE = LANES // RET_HD
RET_CHUNK = 256
ROPE_BASE = 10000.0
NORM_EPS = 1e-6
LAMBDA_INIT = 0.8 - 0.6 * math.exp(-0.3 * 0)
MASK_VALUE = -1e30
LOG2E = 1.0 / math.log(2.0)

ROW_TILE = 512
ATT_QBLOCK = 2048
ATT_BLOCK = 512
ATT_STRIP = 256
ATT_RING = 16
ATT_LEAD = 4


def _dot(a, b):
    return jnp.dot(a, b, preferred_element_type=F32)


def _dot_nt(a, b):
    return lax.dot_general(a, b, (((1,), (1,)), ((), ())), preferred_element_type=F32)


def _dot_tn(a, b):
    return lax.dot_general(a, b, (((0,), (0,)), ((), ())), preferred_element_type=F32)


def _group_sum(sq, group_ones):
    return _dot(sq.astype(BF16), group_ones)


def _inproj_retention_kernel(x_ref, g1_ref, w_ref, cos_ref, sin_ref, gq_ref, gk_ref, ones_ref,
                             dmat_ref, qdec_ref, kdec_ref, cdec_ref, bdmask_ref, gain_ref,
                             *rest, width, kv_block, n_cast):
    cast_in, rest = rest[:n_cast], rest[n_cast:]
    (ret_ref, dqt_ref, dk_ref, dvt_ref), rest = rest[:4], rest[4:]
    cast_out, rest = rest[:n_cast], rest[n_cast:]
    state_ref, rq_ref, rk_ref, rv_ref, rg_ref, acc_ref = rest

    @pl.when(pl.program_id(1) == 0)
    def _():
        state_ref[...] = jnp.zeros_like(state_ref)

    for src_ref, dst_ref in zip(cast_in, cast_out):
        dst_ref[...] = src_ref[...].astype(BF16)

    x = x_ref[0]
    tm = x.shape[0]
    ms = jnp.mean(x * x, axis=-1, keepdims=True)
    h = (x * lax.rsqrt(ms + NORM_EPS) * g1_ref[...]).astype(BF16)

    def proj(j):
        return _dot(h, w_ref[:, j * width:(j + 1) * width])

    n_tiles = width // LANES
    tiles = [slice(t * LANES, (t + 1) * LANES) for t in range(n_tiles)]
    cos = cos_ref[...]
    sin = sin_ref[...]
    lane = lax.broadcasted_iota(jnp.int32, (tm, LANES), 1)
    first_half = (lane % RET_HD) < (RET_HD // 2)

    def rotary(t):
        partner = jnp.where(first_half, pltpu.roll(t, LANES - RET_HD // 2, 1),
                            pltpu.roll(t, RET_HD // 2, 1))
        return t * cos + partner * sin

    rq = proj(0)
    for sl in tiles:
        rq_ref[:, sl] = rotary(rq[:, sl]).astype(BF16)
    rk = proj(1)
    for sl in tiles:
        rk_ref[:, sl] = (rotary(rk[:, sl]) * (RET_HD ** -0.5)).astype(BF16)
    rv_ref[...] = proj(2).astype(BF16)
    rg_ref[...] = proj(3).astype(BF16)

    c = RET_CHUNK
    n_chunks = tm // c
    head_a = lax.broadcasted_iota(jnp.int32, (c, LANES), 1) < RET_HD
    bdmask = bdmask_ref[...]
    ones = ones_ref[...]

    def retention(p):
        sl = tiles[p]
        kvs = []
        for ci in range(n_chunks):
            rows = slice(ci * c, (ci + 1) * c)
            q2 = rq_ref[rows, sl]
            k2 = rk_ref[rows, sl]
            v2 = rv_ref[rows, sl]
            zero = jnp.zeros_like(q2)
            qs = jnp.concatenate([jnp.where(head_a, q2, zero), jnp.where(head_a, zero, q2)], axis=0)
            scores = _dot_nt(qs, k2)
            inner = (scores * dmat_ref[p]).astype(BF16)
            intra2 = _dot(inner, v2)
            acc_ref[rows, sl] = jnp.where(head_a, intra2[:c], intra2[c:])
            kd = (k2.astype(F32) * kdec_ref[:, sl]).astype(BF16)
            kvs.append(_dot_tn(kd, v2) * bdmask)
        state = state_ref[p]
        for ci in range(n_chunks):
            rows = slice(ci * c, (ci + 1) * c)
            qd = (rq_ref[rows, sl].astype(F32) * qdec_ref[:, sl]).astype(BF16)
            acc_ref[rows, sl] += _dot(qd, state.astype(BF16))
            state = state * cdec_ref[:, sl] + kvs[ci]
        state_ref[p] = state

        o = acc_ref[:, sl]
        ssq = _group_sum(o * o, ones)
        y = o * lax.rsqrt(ssq * (1.0 / RET_HD) + NORM_EPS) * gain_ref[:, sl]
        g = rg_ref[:, sl].astype(F32)
        ret_ref[0, :, sl] = (y * (g / (1.0 + jnp.exp(-g)))).astype(BF16)

    def qk_norm(t, g):
        ssq = _group_sum(t * t, ones)
        return t * lax.rsqrt(ssq * (1.0 / DIFF_HD) + NORM_EPS) * g

    def diff_q():
        dq = proj(4)
        q_scale = (DIFF_HD ** -0.5) * LOG2E
        for sl in tiles:
            qn = qk_norm(dq[:, sl], gq_ref[...]) * q_scale
            dqt_ref[0, sl, :] = qn.T.astype(BF16)

    def diff_k():
        dk = proj(5)
        for sl in tiles:
            dk_ref[0, :, sl] = qk_norm(dk[:, sl], gk_ref[...]).astype(BF16)

    def diff_v():
        dv = proj(6)
        for sl in tiles:
            vt = dv[:, sl].T.astype(BF16)
            for j in range(tm // kv_block):
                dvt_ref[0, j, sl, :] = vt[:, j * kv_block:(j + 1) * kv_block]

    others = [diff_q, diff_k, diff_v]
    for p in range(n_tiles):
        retention(p)
        if p < len(others):
            others[p]()
    for f in others[n_tiles:]:
        f()


def _diffattn_kernel(qt_ref, k_ref, vt_ref, lq1_ref, lk1_ref, lq2_ref, lk2_ref, gain_ref,
                     o_ref, s_ref, smax_ref, m_ref, l_ref, acc_ref, *, qblock, block, strip, lead):
    qi = pl.program_id(2)
    ring = s_ref.shape[0]
    n_strips = qblock // strip
    chains = [(mi, t) for mi in range(2) for t in range(n_strips)]
    n_chains = len(chains)
    per_q = qblock // block
    qt = qt_ref[0]
    row = lax.broadcasted_iota(jnp.int32, qt.shape, 0)
    zero = jnp.zeros_like(qt)
    q_maps = (jnp.where(row < DIFF_HD, qt, zero), jnp.where(row < DIFF_HD, zero, qt))

    m_ref[...] = jnp.full_like(m_ref, MASK_VALUE)
    l_ref[...] = jnp.zeros_like(l_ref)
    acc_ref[...] = jnp.zeros_like(acc_ref)

    def scores(kj, c, slot, nkeys=block):
        mi, t = chains[c]
        start = pl.multiple_of(kj * block, block)
        k = k_ref[0, pl.ds(start, nkeys), :]
        sc = _dot(k, q_maps[mi][:, t * strip:(t + 1) * strip])
        s_ref[slot, :nkeys, :] = sc
        smax_ref[slot] = jnp.max(sc, axis=0, keepdims=True)

    def softmax_pv(kj, c, slot, nkeys=block, diag=None):
        mi, t = chains[c]
        cols = slice(t * strip, (t + 1) * strip)
        s = s_ref[slot, :nkeys, :]
        if diag is not None:
            kpos = lax.broadcasted_iota(jnp.int32, (nkeys, strip), 0) + diag * block
            qpos = lax.broadcasted_iota(jnp.int32, (nkeys, strip), 1) + t * strip
            s = jnp.where(kpos <= qpos, s, MASK_VALUE)
            s_max = jnp.max(s, axis=0, keepdims=True)
        else:
            s_max = smax_ref[slot]
        m_old = m_ref[mi, :, cols]
        m_new = jnp.maximum(m_old, s_max)
        alpha = jnp.exp2(m_old - m_new)
        p = jnp.exp2(s - m_new)
        l_ref[mi, :, cols] = alpha * l_ref[mi, :, cols] + jnp.sum(p, axis=0, keepdims=True)
        acc_ref[mi, :, cols] = (alpha * acc_ref[mi, :, cols]
                                + _dot(vt_ref[0, kj, :, :nkeys], p.astype(BF16)))
        m_ref[mi, :, cols] = m_new

    for c in range(lead):
        scores(0, c, c)

    trip = 2 * n_chains

    def body(t, carry):
        for j in range(trip):
            nxt = j + lead
            scores(2 * t + nxt // n_chains, nxt % n_chains, nxt % ring)
            softmax_pv(2 * t + j // n_chains, j % n_chains, j % ring)
        return carry

    n_full = qi * per_q
    lax.fori_loop(0, n_full // 2, body, 0)

    tail = []
    for d in range(per_q):
        for c, (mi, t) in enumerate(chains):
            nkeys = min(block, (t + 1) * strip - d * block)
            if nkeys > 0:
                masked = t * strip < d * block + nkeys - 1
                tail.append((d, c, nkeys, d if masked else None))
    for i, (d, c, nkeys, diag) in enumerate(tail):
        if i < lead:
            assert d == 0 and c == i
        if i + lead < len(tail):
            d2, c2, nkeys2, _ = tail[i + lead]
            scores(n_full + d2, c2, (i + lead) % ring, nkeys2)
        softmax_pv(n_full + d, c, i % ring, nkeys, diag)

    lam = (jnp.exp(jnp.sum(lq1_ref[...] * lk1_ref[...], axis=-1, keepdims=True))
           - jnp.exp(jnp.sum(lq2_ref[...] * lk2_ref[...], axis=-1, keepdims=True))
           + LAMBDA_INIT)
    o = acc_ref[0] / l_ref[0] - lam * (acc_ref[1] / l_ref[1])
    ms = jnp.mean(o * o, axis=0, keepdims=True)
    o = o * lax.rsqrt(ms + NORM_EPS)
    o_ref[0] = (o.T * gain_ref[...] * (1.0 - LAMBDA_INIT)).astype(BF16)


def _out_ffn_kernel(x_ref, ret_ref, dif_ref, wo_ref, g2_ref, wg_ref, wu_ref, wd_ref, o_ref, *,
                    ret_width, ff_chunks):
    x1 = (x_ref[0] + _dot(ret_ref[0], wo_ref[:ret_width, :])
          + _dot(dif_ref[0], wo_ref[ret_width:, :]))
    ms = jnp.mean(x1 * x1, axis=-1, keepdims=True)
    h = (x1 * lax.rsqrt(ms + NORM_EPS) * g2_ref[...]).astype(BF16)
    o_ref[0] = x1
    ff = None
    for lo, hi in ff_chunks:
        g = _dot(h, wg_ref[:, lo:hi])
        u = _dot(h, wu_ref[:, lo:hi])
        a = (g / (1.0 + jnp.exp(-g)) * u).astype(BF16)
        d = _dot(a, wd_ref[lo:hi, :])
        ff = d if ff is None else ff + d
    o_ref[0] += ff


def _resident(shape):
    return pl.BlockSpec(shape, lambda *_: (0,) * len(shape), pipeline_mode=pl.Buffered(1))


def _ff_chunks(d_ff, mxu_cols=256, target=1536):
    chunks, lo = [], 0
    while lo < d_ff:
        hi = min(d_ff, lo + target)
        if d_ff - hi and (d_ff - hi) < mxu_cols:
            hi = d_ff
        chunks.append((lo, hi))
        lo = hi
    return tuple(chunks)


def kernel(x, norm1_g, w_in, ret_norm_g, diff_q_norm_g, diff_k_norm_g, lambda_q1, lambda_k1,
           lambda_q2, lambda_k2, diff_subln_g, w_out, norm2_g, w_gate, w_up, w_down):
    B, S, D = x.shape
    depth = w_in.shape[0]
    n_ret_heads = ret_norm_g.shape[1]
    ret_width = n_ret_heads * RET_HD
    in_width = w_in.shape[2]
    width = ret_width
    assert in_width == 7 * width and w_out.shape[1] == 2 * width
    assert width % LANES == 0 and S % ROW_TILE == 0 and S % ATT_BLOCK == 0
    assert ROW_TILE % ATT_BLOCK == 0 and ROW_TILE % RET_CHUNK == 0
    assert all(w.shape[1] % (B * (S // ROW_TILE)) == 0 for w in (w_out, w_gate, w_up, w_down))
    assert S % ATT_QBLOCK == 0 and (ATT_QBLOCK // ATT_BLOCK) % 2 == 0 and ATT_QBLOCK % ATT_STRIP == 0
    assert (4 * ATT_QBLOCK // ATT_STRIP) % ATT_RING == 0 and ATT_LEAD < ATT_RING
    assert ATT_LEAD <= 2 * ATT_QBLOCK // ATT_STRIP
    n_diff_heads = width // DIFF_VD
    n_tiles = width // LANES
    d_ff = w_gate.shape[2]
    nk = S // ATT_BLOCK
    params = functools.partial(pltpu.CompilerParams, vmem_limit_bytes=VMEM_LIMIT_BYTES)

    pos = np.arange(S, dtype=np.float64)
    freqs = 1.0 / (ROPE_BASE ** (np.arange(0, RET_HD, 2, dtype=np.float64) / RET_HD))
    ang = pos[:, None] * freqs[None, :]
    cos_t = jnp.asarray(np.tile(np.cos(ang), (1, LANES // (RET_HD // 2))), F32)
    sin_half = np.sin(ang)
    sin_t = jnp.asarray(np.tile(np.concatenate([-sin_half, sin_half], axis=-1), (1, HEADS_PER_TILE)), F32)
    grp = np.arange(LANES) // RET_HD
    same_group = (grp[:, None] == grp[None, :]).astype(np.float32)
    group_ones = jnp.asarray(same_group, BF16)
    bdmask = jnp.asarray(same_group, F32)

    log_g = np.log(1.0 - 2.0 ** (-5.0 - np.arange(n_ret_heads, dtype=np.float64)))
    idx = np.arange(RET_CHUNK, dtype=np.float64)
    rel = idx[:, None] - idx[None, :]
    decay_in = np.where(rel[None] >= 0, np.exp(log_g[:, None, None] * np.maximum(rel, 0.0)[None]), 0.0)
    dmat = jnp.asarray(decay_in.reshape(n_tiles, HEADS_PER_TILE * RET_CHUNK, RET_CHUNK), F32)
    by_lane = lambda t: jnp.asarray(np.repeat(t, RET_HD, axis=0).T, F32)
    qdec = by_lane(np.exp(log_g[:, None] * (idx[None, :] + 1.0)))
    kdec = by_lane(np.exp(log_g[:, None] * (RET_CHUNK - 1.0 - idx[None, :])))
    cdec = by_lane(np.exp(log_g * RET_CHUNK)[:, None])

    for l in range(depth):
        row = lambda b, i: (b, i, 0)
        col = lambda b, i: (b, 0, i)
        tab = lambda b, i: (i, 0)
        act = jax.ShapeDtypeStruct((B, S, width), BF16)
        act_t = jax.ShapeDtypeStruct((B, width, S), BF16)
        tile_bf16 = pltpu.VMEM((ROW_TILE, width), BF16)
        n_steps = B * (S // ROW_TILE)
        later = [w_out[l], w_gate[l], w_up[l], w_down[l]]
        later = [w.reshape(n_steps, w.shape[0] // n_steps, w.shape[1]) for w in later]
        step = lambda b, i: (b * (S // ROW_TILE) + i, 0, 0)
        cast_specs = [pl.BlockSpec((1,) + w.shape[1:], step) for w in later]
        ret, dqt, dk, dvt, wo_b, wg_b, wu_b, wd_b = pl.pallas_call(
            functools.partial(_inproj_retention_kernel, width=width, kv_block=ATT_BLOCK,
                              n_cast=len(later)),
            grid=(B, S // ROW_TILE),
            in_specs=[
                pl.BlockSpec((1, ROW_TILE, D), row),
                _resident((1, D)),
                _resident((D, in_width)),
                pl.BlockSpec((ROW_TILE, LANES), tab),
                pl.BlockSpec((ROW_TILE, LANES), tab),
                _resident((1, LANES)),
                _resident((1, LANES)),
                _resident((LANES, LANES)),
                _resident(dmat.shape), _resident(qdec.shape), _resident(kdec.shape),
                _resident(cdec.shape), _resident(bdmask.shape), _resident((1, width)),
            ] + cast_specs,
            out_specs=[
                pl.BlockSpec((1, ROW_TILE, width), row),
                pl.BlockSpec((1, width, ROW_TILE), col),
                pl.BlockSpec((1, ROW_TILE, width), row),
                pl.BlockSpec((1, ROW_TILE // ATT_BLOCK, width, ATT_BLOCK), lambda b, i: (b, i, 0, 0)),
            ] + cast_specs,
            out_shape=[act, act_t, act, jax.ShapeDtypeStruct((B, nk, width, ATT_BLOCK), BF16)]
                      + [jax.ShapeDtypeStruct(w.shape, BF16) for w in later],
            scratch_shapes=[pltpu.VMEM((n_tiles, LANES, LANES), F32),
                            tile_bf16, tile_bf16, tile_bf16, tile_bf16,
                            pltpu.VMEM((ROW_TILE, width), F32)],
            compiler_params=params(dimension_semantics=("arbitrary", "arbitrary")),
            name="inproj_retention",
        )(x, norm1_g[l][None, :], w_in[l].astype(BF16), cos_t, sin_t,
          jnp.tile(diff_q_norm_g[l], LANES // DIFF_HD)[None, :],
          jnp.tile(diff_k_norm_g[l], LANES // DIFF_HD)[None, :], group_ones,
          dmat, qdec, kdec, cdec, bdmask, ret_norm_g[l].reshape(1, width), *later)
        wo_b, wg_b, wu_b, wd_b = (b.reshape(w.shape) for b, w in
                                  zip((wo_b, wg_b, wu_b, wd_b), (w_out[l], w_gate[l], w_up[l], w_down[l])))

        lam_spec = _resident((1, DIFF_HD))
        dif = pl.pallas_call(
            functools.partial(_diffattn_kernel, qblock=ATT_QBLOCK, block=ATT_BLOCK, strip=ATT_STRIP,
                              lead=ATT_LEAD),
            grid=(B, n_diff_heads, S // ATT_QBLOCK),
            in_specs=[
                pl.BlockSpec((1, DIFF_VD, ATT_QBLOCK), lambda b, h, i: (b, h, i)),
                pl.BlockSpec((1, S, DIFF_VD), lambda b, h, i: (b, 0, h)),
                pl.BlockSpec((1, nk, DIFF_VD, ATT_BLOCK), lambda b, h, i: (b, 0, h, 0)),
                lam_spec, lam_spec, lam_spec, lam_spec,
                _resident((1, DIFF_VD)),
            ],
            out_specs=pl.BlockSpec((1, ATT_QBLOCK, DIFF_VD), lambda b, h, i: (b, i, h)),
            out_shape=act,
            scratch_shapes=[pltpu.VMEM((ATT_RING, ATT_BLOCK, ATT_STRIP), F32),
                            pltpu.VMEM((ATT_RING, 1, ATT_STRIP), F32),
                            pltpu.VMEM((2, 1, ATT_QBLOCK), F32),
                            pltpu.VMEM((2, 1, ATT_QBLOCK), F32),
                            pltpu.VMEM((2, DIFF_VD, ATT_QBLOCK), F32)],
            compiler_params=params(dimension_semantics=("parallel", "parallel", "arbitrary")),
            name="diffattn",
        )(dqt, dk, dvt, lambda_q1[l][None, :], lambda_k1[l][None, :], lambda_q2[l][None, :],
          lambda_k2[l][None, :], diff_subln_g[l][None, :])

        tile = pl.BlockSpec((1, ROW_TILE, D), row)
        half = pl.BlockSpec((1, ROW_TILE, width), row)
        x = pl.pallas_call(
            functools.partial(_out_ffn_kernel, ret_width=width, ff_chunks=_ff_chunks(d_ff)),
            grid=(B, S // ROW_TILE),
            in_specs=[tile, half, half,
                      _resident((2 * width, D)), _resident((1, D)),
                      _resident((D, d_ff)), _resident((D, d_ff)), _resident((d_ff, D))],
            out_specs=tile,
            out_shape=jax.ShapeDtypeStruct((B, S, D), F32),
            compiler_params=params(dimension_semantics=("parallel", "parallel")),
            name="out_ffn",
        )(x, ret, dif, wo_b, norm2_g[l][None, :], wg_b, wu_b, wd_b)
    return x
```

```python
import functools
import math

import jax
import jax.numpy as jnp
import numpy as np
from jax import lax
from jax.experimental import pallas as pl
from jax.experimental.pallas import tpu as pltpu

F32 = jnp.float32
BF16 = jnp.bfloat16

LANES = 128
VMEM_LIMIT_BYTES = 56 * 1024 * 1024

RET_HD = 64
DIFF_HD = 64
DIFF_VD = 2 * DIFF_HD
HEADS_PER_TILE = LANES // RET_HD
RET_CHUNK = 256
ROPE_BASE = 10000.0
NORM_EPS = 1e-6
LAMBDA_INIT = 0.8 - 0.6 * math.exp(-0.3 * 0)
MASK_VALUE = -1e30
LOG2E = 1.0 / math.log(2.0)

ROW_TILE = 512
ATT_QBLOCK = 2048
ATT_BLOCK = 512
ATT_STRIP = 256
ATT_RING = 16
ATT_LEAD = 4


def _dot(a, b):
    return jnp.dot(a, b, preferred_element_type=F32)


def _dot_nt(a, b):
    return lax.dot_general(a, b, (((1,), (1,)), ((), ())), preferred_element_type=F32)


def _dot_tn(a, b):
    return lax.dot_general(a, b, (((0,), (0,)), ((), ())), preferred_element_type=F32)


def _group_sum(sq, group_ones):
    return _dot(sq.astype(BF16), group_ones)


def _inproj_retention_kernel(x_ref, g1_ref, w32_ref, cos_ref, sin_ref, gq_ref, gk_ref, ones_ref,
                             dmat_ref, qdec_ref, kdec_ref, cdec_ref, bdmask_ref, gain_ref,
                             *rest, width, kv_block, n_cast):
    cast_in, rest = rest[:n_cast], rest[n_cast:]
    (ret_ref, dqt_ref, dk_ref, dvt_ref), rest = rest[:4], rest[4:]
    cast_out, rest = rest[:n_cast], rest[n_cast:]
    w_ref, state_ref, rq_ref, rk_ref, rv_ref, rg_ref, acc_ref = rest

    @pl.when((pl.program_id(0) == 0) & (pl.program_id(1) == 0))
    def _():
        w_ref[...] = w32_ref[...].astype(BF16)

    @pl.when(pl.program_id(1) == 0)
    def _():
        state_ref[...] = jnp.zeros_like(state_ref)

    for src_ref, dst_ref in zip(cast_in, cast_out):
        dst_ref[...] = src_ref[...].astype(BF16)

    x = x_ref[0]
    tm = x.shape[0]
    ms = jnp.mean(x * x, axis=-1, keepdims=True)
    h = (x * lax.rsqrt(ms + NORM_EPS) * g1_ref[...]).astype(BF16)

    def proj(j):
        return _dot(h, w_ref[:, j * width:(j + 1) * width])

    n_tiles = width // LANES
    tiles = [slice(t * LANES, (t + 1) * LANES) for t in range(n_tiles)]
    cos = cos_ref[...]
    sin = sin_ref[...]
    lane = lax.broadcasted_iota(jnp.int32, (tm, LANES), 1)
    first_half = (lane % RET_HD) < (RET_HD // 2)

    def rotary(t):
        partner = jnp.where(first_half, pltpu.roll(t, LANES - RET_HD // 2, 1),
                            pltpu.roll(t, RET_HD // 2, 1))
        return t * cos + partner * sin

    rq = proj(0)
    for sl in tiles:
        rq_ref[:, sl] = rotary(rq[:, sl]).astype(BF16)
    rk = proj(1)
    for sl in tiles:
        rk_ref[:, sl] = (rotary(rk[:, sl]) * (RET_HD ** -0.5)).astype(BF16)
    rv_ref[...] = proj(2).astype(BF16)
    rg_ref[...] = proj(3).astype(BF16)

    c = RET_CHUNK
    n_chunks = tm // c
    head_a = lax.broadcasted_iota(jnp.int32, (c, LANES), 1) < RET_HD
    bdmask = bdmask_ref[...]
    ones = ones_ref[...]

    def retention(p):
        sl = tiles[p]
        kvs = []
        for ci in range(n_chunks):
            rows = slice(ci * c, (ci + 1) * c)
            q2 = rq_ref[rows, sl]
            k2 = rk_ref[rows, sl]
            v2 = rv_ref[rows, sl]
            zero = jnp.zeros_like(q2)
            qs = jnp.concatenate([jnp.where(head_a, q2, zero), jnp.where(head_a, zero, q2)], axis=0)
            scores = _dot_nt(qs, k2)
            inner = (scores * dmat_ref[p]).astype(BF16)
            intra2 = _dot(inner, v2)
            acc_ref[rows, sl] = jnp.where(head_a, intra2[:c], intra2[c:])
            kd = (k2.astype(F32) * kdec_ref[:, sl]).astype(BF16)
            kvs.append(_dot_tn(kd, v2) * bdmask)
        state = state_ref[p]
        for ci in range(n_chunks):
            rows = slice(ci * c, (ci + 1) * c)
            qd = (rq_ref[rows, sl].astype(F32) * qdec_ref[:, sl]).astype(BF16)
            acc_ref[rows, sl] += _dot(qd, state.astype(BF16))
            state = state * cdec_ref[:, sl] + kvs[ci]
        state_ref[p] = state

        o = acc_ref[:, sl]
        ssq = _group_sum(o * o, ones)
        y = o * lax.rsqrt(ssq * (1.0 / RET_HD) + NORM_EPS) * gain_ref[:, sl]
        g = rg_ref[:, sl].astype(F32)
        ret_ref[0, :, sl] = (y * (g / (1.0 + jnp.exp(-g)))).astype(BF16)

    def qk_norm(t, g):
        ssq = _group_sum(t * t, ones)
        return t * lax.rsqrt(ssq * (1.0 / DIFF_HD) + NORM_EPS) * g

    def diff_q():
        dq = proj(4)
        q_scale = (DIFF_HD ** -0.5) * LOG2E
        for sl in tiles:
            qn = qk_norm(dq[:, sl], gq_ref[...]) * q_scale
            dqt_ref[0, sl, :] = qn.T.astype(BF16)

    def diff_k():
        dk = proj(5)
        for sl in tiles:
            dk_ref[0, :, sl] = qk_norm(dk[:, sl], gk_ref[...]).astype(BF16)

    def diff_v():
        dv = proj(6)
        for sl in tiles:
            vt = dv[:, sl].T.astype(BF16)
            for j in range(tm // kv_block):
                dvt_ref[0, j, sl, :] = vt[:, j * kv_block:(j + 1) * kv_block]

    others = [diff_q, diff_k, diff_v]
    for p in range(n_tiles):
        retention(p)
        if p < len(others):
            others[p]()
    for f in others[n_tiles:]:
        f()


def _diffattn_kernel(qt_ref, k_ref, vt_ref, lq1_ref, lk1_ref, lq2_ref, lk2_ref, gain_ref,
                     o_ref, s_ref, smax_ref, tri_ref, m_ref, l_ref, acc_ref, *, qblock, block, strip, lead):
    qi = pl.program_id(2)
    ring = s_ref.shape[0]
    n_strips = qblock // strip
    chains = [(mi, t) for mi in range(2) for t in reversed(range(n_strips))]
    n_chains = len(chains)
    per_q = qblock // block
    qt = qt_ref[0]
    row = lax.broadcasted_iota(jnp.int32, qt.shape, 0)
    zero = jnp.zeros_like(qt)
    q_maps = (jnp.where(row < DIFF_HD, qt, zero), jnp.where(row < DIFF_HD, zero, qt))

    kpos = lax.broadcasted_iota(jnp.int32, (strip, strip), 0)
    qpos = lax.broadcasted_iota(jnp.int32, (strip, strip), 1)
    tri_ref[...] = jnp.where(kpos <= qpos, 0.0, MASK_VALUE)

    def scores(kj, c, slot, nkeys=block, tri_rows=None):
        mi, t = chains[c]
        start = pl.multiple_of(kj * block, block)
        k = k_ref[0, pl.ds(start, nkeys), :]
        sc = _dot(k, q_maps[mi][:, t * strip:(t + 1) * strip])
        if tri_rows is not None:
            own = sc[tri_rows:] + tri_ref[...]
            sc = own if tri_rows == 0 else jnp.concatenate([sc[:tri_rows], own], axis=0)
        s_ref[slot, :nkeys, :] = sc
        smax_ref[slot] = jnp.max(sc, axis=0, keepdims=True)

    def softmax_pv(kj, c, slot, nkeys=block):
        mi, t = chains[c]
        cols = slice(t * strip, (t + 1) * strip)
        s = s_ref[slot, :nkeys, :]
        m_old = m_ref[mi, :, cols]
        m_new = jnp.maximum(m_old, smax_ref[slot])
        alpha = jnp.exp2(m_old - m_new)
        p = jnp.exp2(s - m_new)
        l_ref[mi, :, cols] = alpha * l_ref[mi, :, cols] + jnp.sum(p, axis=0, keepdims=True)
        acc_ref[mi, :, cols] = (alpha * acc_ref[mi, :, cols]
                                + _dot(vt_ref[0, kj, :, :nkeys], p.astype(BF16)))
        m_ref[mi, :, cols] = m_new

    for c in range(lead):
        scores(0, c, c)
    m_ref[...] = jnp.full_like(m_ref, MASK_VALUE)
    l_ref[...] = jnp.zeros_like(l_ref)
    acc_ref[...] = jnp.zeros_like(acc_ref)

    trip = 2 * n_chains

    def body(t, carry):
        for j in range(trip):
            nxt = j + lead
            scores(2 * t + nxt // n_chains, nxt % n_chains, nxt % ring)
            softmax_pv(2 * t + j // n_chains, j % n_chains, j % ring)
        return carry

    n_full = qi * per_q
    lax.fori_loop(0, n_full // 2, body, 0)

    tail = []
    for d in range(per_q):
        for c, (mi, t) in enumerate(chains):
            nkeys = min(block, (t + 1) * strip - d * block)
            if nkeys > 0:
                own = t * strip - d * block
                tail.append((d, c, nkeys, own if own < nkeys else None))
    lam = (jnp.exp(jnp.sum(lq1_ref[...] * lk1_ref[...], axis=-1, keepdims=True))
           - jnp.exp(jnp.sum(lq2_ref[...] * lk2_ref[...], axis=-1, keepdims=True))
           + LAMBDA_INIT)

    def finish(t):
        cols = slice(t * strip, (t + 1) * strip)
        o = (acc_ref[0, :, cols] / l_ref[0, :, cols]
             - lam * (acc_ref[1, :, cols] / l_ref[1, :, cols]))
        ms = jnp.mean(o * o, axis=0, keepdims=True)
        o = o * lax.rsqrt(ms + NORM_EPS)
        o_ref[0, cols, :] = (o.T * gain_ref[...] * (1.0 - LAMBDA_INIT)).astype(BF16)

    last_use = {chains[c][1]: i for i, (_, c, _, _) in enumerate(tail)}
    for i, (d, c, nkeys, tri_rows) in enumerate(tail):
        if i < lead:
            assert d == 0 and c == i and nkeys == block and tri_rows is None
        else:
            assert tri_rows is None or (tri_rows >= 0 and tri_rows + strip == nkeys)
        if i + lead < len(tail):
            d2, c2, nkeys2, tri2 = tail[i + lead]
            scores(n_full + d2, c2, (i + lead) % ring, nkeys2, tri2)
        softmax_pv(n_full + d, c, i % ring, nkeys)
        if i == last_use[chains[c][1]]:
            finish(chains[c][1])


def _out_ffn_kernel(x_ref, ret_ref, dif_ref, wo_ref, g2_ref, wg_ref, wu_ref, wd_ref, o_ref, *,
                    ret_width, ff_chunks):
    x1 = (x_ref[0] + _dot(ret_ref[0], wo_ref[:ret_width, :])
          + _dot(dif_ref[0], wo_ref[ret_width:, :]))
    ms = jnp.mean(x1 * x1, axis=-1, keepdims=True)
    h = (x1 * lax.rsqrt(ms + NORM_EPS) * g2_ref[...]).astype(BF16)
    o_ref[0] = x1
    ff = None
    for lo, hi in ff_chunks:
        g = _dot(h, wg_ref[:, lo:hi])
        u = _dot(h, wu_ref[:, lo:hi])
        a = (g / (1.0 + jnp.exp(-g)) * u).astype(BF16)
        d = _dot(a, wd_ref[lo:hi, :])
        ff = d if ff is None else ff + d
    o_ref[0] += ff


def _resident(shape):
    return pl.BlockSpec(shape, lambda *_: (0,) * len(shape), pipeline_mode=pl.Buffered(1))


def _ff_chunks(d_ff, mxu_cols=256, target=1536):
    chunks, lo = [], 0
    while lo < d_ff:
        hi = min(d_ff, lo + target)
        if d_ff - hi and (d_ff - hi) < mxu_cols:
            hi = d_ff
        chunks.append((lo, hi))
        lo = hi
    return tuple(chunks)


def kernel(x, norm1_g, w_in, ret_norm_g, diff_q_norm_g, diff_k_norm_g, lambda_q1, lambda_k1,
           lambda_q2, lambda_k2, diff_subln_g, w_out, norm2_g, w_gate, w_up, w_down):
    B, S, D = x.shape
    depth = w_in.shape[0]
    n_ret_heads = ret_norm_g.shape[1]
    ret_width = n_ret_heads * RET_HD
    in_width = w_in.shape[2]
    width = ret_width
    assert in_width == 7 * width and w_out.shape[1] == 2 * width
    assert width % LANES == 0 and S % ROW_TILE == 0 and S % ATT_BLOCK == 0
    assert ROW_TILE % ATT_BLOCK == 0 and ROW_TILE % RET_CHUNK == 0
    assert all(w.shape[1] % (B * (S // ROW_TILE)) == 0 for w in (w_out, w_gate, w_up, w_down))
    assert S % ATT_QBLOCK == 0 and (ATT_QBLOCK // ATT_BLOCK) % 2 == 0 and ATT_QBLOCK % ATT_STRIP == 0
    assert (4 * ATT_QBLOCK // ATT_STRIP) % ATT_RING == 0 and ATT_LEAD < ATT_RING
    assert ATT_LEAD <= 2 * ATT_QBLOCK // ATT_STRIP
    n_diff_heads = width // DIFF_VD
    n_tiles = width // LANES
    d_ff = w_gate.shape[2]
    nk = S // ATT_BLOCK
    params = functools.partial(pltpu.CompilerParams, vmem_limit_bytes=VMEM_LIMIT_BYTES)

    pos = np.arange(S, dtype=np.float64)
    freqs = 1.0 / (ROPE_BASE ** (np.arange(0, RET_HD, 2, dtype=np.float64) / RET_HD))
    ang = pos[:, None] * freqs[None, :]
    cos_t = jnp.asarray(np.tile(np.cos(ang), (1, LANES // (RET_HD // 2))), F32)
    sin_half = np.sin(ang)
    sin_t = jnp.asarray(np.tile(np.concatenate([-sin_half, sin_half], axis=-1), (1, HEADS_PER_TILE)), F32)
    grp = np.arange(LANES) // RET_HD
    same_group = (grp[:, None] == grp[None, :]).astype(np.float32)
    group_ones = jnp.asarray(same_group, BF16)
    bdmask = jnp.asarray(same_group, F32)

    log_g = np.log(1.0 - 2.0 ** (-5.0 - np.arange(n_ret_heads, dtype=np.float64)))
    idx = np.arange(RET_CHUNK, dtype=np.float64)
    rel = idx[:, None] - idx[None, :]
    decay_in = np.where(rel[None] >= 0, np.exp(log_g[:, None, None] * np.maximum(rel, 0.0)[None]), 0.0)
    dmat = jnp.asarray(decay_in.reshape(n_tiles, HEADS_PER_TILE * RET_CHUNK, RET_CHUNK), F32)
    by_lane = lambda t: jnp.asarray(np.repeat(t, RET_HD, axis=0).T, F32)
    qdec = by_lane(np.exp(log_g[:, None] * (idx[None, :] + 1.0)))
    kdec = by_lane(np.exp(log_g[:, None] * (RET_CHUNK - 1.0 - idx[None, :])))
    cdec = by_lane(np.exp(log_g * RET_CHUNK)[:, None])

    for l in range(depth):
        row = lambda b, i: (b, i, 0)
        col = lambda b, i: (b, 0, i)
        tab = lambda b, i: (i, 0)
        act = jax.ShapeDtypeStruct((B, S, width), BF16)
        act_t = jax.ShapeDtypeStruct((B, width, S), BF16)
        tile_bf16 = pltpu.VMEM((ROW_TILE, width), BF16)
        n_steps = B * (S // ROW_TILE)
        later = [w_out[l], w_gate[l], w_up[l], w_down[l]]
        later = [w.reshape(n_steps, w.shape[0] // n_steps, w.shape[1]) for w in later]
        step = lambda b, i: (b * (S // ROW_TILE) + i, 0, 0)
        cast_specs = [pl.BlockSpec((1,) + w.shape[1:], step) for w in later]
        ret, dqt, dk, dvt, wo_b, wg_b, wu_b, wd_b = pl.pallas_call(
            functools.partial(_inproj_retention_kernel, width=width, kv_block=ATT_BLOCK,
                              n_cast=len(later)),
            grid=(B, S // ROW_TILE),
            in_specs=[
                pl.BlockSpec((1, ROW_TILE, D), row),
                _resident((1, D)),
                _resident((D, in_width)),
                pl.BlockSpec((ROW_TILE, LANES), tab),
                pl.BlockSpec((ROW_TILE, LANES), tab),
                _resident((1, LANES)),
                _resident((1, LANES)),
                _resident((LANES, LANES)),
                _resident(dmat.shape), _resident(qdec.shape), _resident(kdec.shape),
                _resident(cdec.shape), _resident(bdmask.shape), _resident((1, width)),
            ] + cast_specs,
            out_specs=[
                pl.BlockSpec((1, ROW_TILE, width), row),
                pl.BlockSpec((1, width, ROW_TILE), col),
                pl.BlockSpec((1, ROW_TILE, width), row),
                pl.BlockSpec((1, ROW_TILE // ATT_BLOCK, width, ATT_BLOCK), lambda b, i: (b, i, 0, 0)),
            ] + cast_specs,
            out_shape=[act, act_t, act, jax.ShapeDtypeStruct((B, nk, width, ATT_BLOCK), BF16)]
                      + [jax.ShapeDtypeStruct(w.shape, BF16) for w in later],
            scratch_shapes=[pltpu.VMEM((D, in_width), BF16),
                            pltpu.VMEM((n_tiles, LANES, LANES), F32),
                            tile_bf16, tile_bf16, tile_bf16, tile_bf16,
                            pltpu.VMEM((ROW_TILE, width), F32)],
            compiler_params=params(dimension_semantics=("arbitrary", "arbitrary")),
            name="inproj_retention",
        )(x, norm1_g[l][None, :], w_in[l], cos_t, sin_t,
          jnp.tile(diff_q_norm_g[l], LANES // DIFF_HD)[None, :],
          jnp.tile(diff_k_norm_g[l], LANES // DIFF_HD)[None, :], group_ones,
          dmat, qdec, kdec, cdec, bdmask, ret_norm_g[l].reshape(1, width), *later)
        wo_b, wg_b, wu_b, wd_b = (b.reshape(w.shape) for b, w in
                                  zip((wo_b, wg_b, wu_b, wd_b), (w_out[l], w_gate[l], w_up[l], w_down[l])))

        lam_spec = _resident((1, DIFF_HD))
        dif = pl.pallas_call(
            functools.partial(_diffattn_kernel, qblock=ATT_QBLOCK, block=ATT_BLOCK, strip=ATT_STRIP,
                              lead=ATT_LEAD),
            grid=(B, n_diff_heads, S // ATT_QBLOCK),
            in_specs=[
                pl.BlockSpec((1, DIFF_VD, ATT_QBLOCK), lambda b, h, i: (b, h, i)),
                pl.BlockSpec((1, S, DIFF_VD), lambda b, h, i: (b, 0, h)),
                pl.BlockSpec((1, nk, DIFF_VD, ATT_BLOCK), lambda b, h, i: (b, 0, h, 0)),
                lam_spec, lam_spec, lam_spec, lam_spec,
                _resident((1, DIFF_VD)),
            ],
            out_specs=pl.BlockSpec((1, ATT_QBLOCK, DIFF_VD), lambda b, h, i: (b, i, h)),
            out_shape=act,
            scratch_shapes=[pltpu.VMEM((ATT_RING, ATT_BLOCK, ATT_STRIP), F32),
                            pltpu.VMEM((ATT_RING, 1, ATT_STRIP), F32),
                            pltpu.VMEM((ATT_STRIP, ATT_STRIP), F32),
                            pltpu.VMEM((2, 1, ATT_QBLOCK), F32),
                            pltpu.VMEM((2, 1, ATT_QBLOCK), F32),
                            pltpu.VMEM((2, DIFF_VD, ATT_QBLOCK), F32)],
            compiler_params=params(dimension_semantics=("parallel", "parallel", "arbitrary")),
            name="diffattn",
        )(dqt, dk, dvt, lambda_q1[l][None, :], lambda_k1[l][None, :], lambda_q2[l][None, :],
          lambda_k2[l][None, :], diff_subln_g[l][None, :])

        tile = pl.BlockSpec((1, ROW_TILE, D), row)
        half = pl.BlockSpec((1, ROW_TILE, width), row)
        x = pl.pallas_call(
            functools.partial(_out_ffn_kernel, ret_width=width, ff_chunks=_ff_chunks(d_ff)),
            grid=(B, S // ROW_TILE),
            in_specs=[tile, half, half,
                      _resident((2 * width, D)), _resident((1, D)),
                      _resident((D, d_ff)), _resident((D, d_ff)), _resident((d_ff, D))],
            out_specs=tile,
            out_shape=jax.ShapeDtypeStruct((B, S, D), F32),
            compiler_params=params(dimension_semantics=("parallel", "parallel")),
            name="out_ffn",
        )(x, ret, dif, wo_b, norm2_g[l][None, :], wg_b, wu_b, wd_b)
    return x
```

```python
import functools
import math

import jax
import jax.numpy as jnp
import numpy as np
from jax import lax
from jax.experimental import pallas as pl
from jax.experimental.pallas import tpu as pltpu

F32 = jnp.float32
BF16 = jnp.bfloat16

LANES = 128
VMEM_LIMIT_BYTES = 56 * 1024 * 1024

RET_HD = 64
DIFF_HD = 64
DIFF_VD = 2 * DIFF_HD
HEADS_PER_TILE = LANES // RET_HD
RET_CHUNK = 256
ROPE_BASE = 10000.0
NORM_EPS = 1e-6
LAMBDA_INIT = 0.8 - 0.6 * math.exp(-0.3 * 0)
MASK_VALUE = -1e30
LOG2E = 1.0 / math.log(2.0)

ROW_TILE = 512
ATT_QBLOCK = 2048
ATT_BLOCK = 512
ATT_STRIP = 256
ATT_RING = 16
ATT_LEAD = 4


def _dot(a, b):
    return jnp.dot(a, b, preferred_element_type=F32)


def _dot_nt(a, b):
    return lax.dot_general(a, b, (((1,), (1,)), ((), ())), preferred_element_type=F32)


def _dot_tn(a, b):
    return lax.dot_general(a, b, (((0,), (0,)), ((), ())), preferred_element_type=F32)


def _half_sums(sq):
    first = lax.broadcasted_iota(jnp.int32, sq.shape, 1) < (LANES // 2)
    a = jnp.sum(jnp.where(first, sq, 0.0), axis=-1, keepdims=True)
    b = jnp.sum(jnp.where(first, 0.0, sq), axis=-1, keepdims=True)
    return jnp.where(first, a, b)


def _inproj_retention_kernel(x_ref, g1_ref, w32_ref, cos_ref, sin_ref, gq_ref, gk_ref,
                             dmat_ref, qdec_ref, kdec_ref, cdec_ref, bdmask_ref, gain_ref,
                             *rest, width, kv_block, n_cast):
    cast_in, rest = rest[:n_cast], rest[n_cast:]
    (ret_ref, dqt_ref, dk_ref, dvt_ref), rest = rest[:4], rest[4:]
    cast_out, rest = rest[:n_cast], rest[n_cast:]
    w_ref, state_ref, rq_ref, rk_ref, rv_ref, rg_ref, acc_ref = rest

    @pl.when((pl.program_id(0) == 0) & (pl.program_id(1) == 0))
    def _():
        w_ref[...] = w32_ref[...].astype(BF16)

    @pl.when(pl.program_id(1) == 0)
    def _():
        state_ref[...] = jnp.zeros_like(state_ref)

    for src_ref, dst_ref in zip(cast_in, cast_out):
        dst_ref[...] = src_ref[...].astype(BF16)

    x = x_ref[0]
    tm = x.shape[0]
    ms = jnp.mean(x * x, axis=-1, keepdims=True)
    h = (x * lax.rsqrt(ms + NORM_EPS) * g1_ref[...]).astype(BF16)

    def proj(j):
        return _dot(h, w_ref[:, j * width:(j + 1) * width])

    n_tiles = width // LANES
    tiles = [slice(t * LANES, (t + 1) * LANES) for t in range(n_tiles)]
    cos = cos_ref[...]
    sin = sin_ref[...]
    lane = lax.broadcasted_iota(jnp.int32, (tm, LANES), 1)
    first_half = (lane % RET_HD) < (RET_HD // 2)

    def rotary(t):
        partner = jnp.where(first_half, pltpu.roll(t, LANES - RET_HD // 2, 1),
                            pltpu.roll(t, RET_HD // 2, 1))
        return t * cos + partner * sin

    rq = proj(0)
    for sl in tiles:
        rq_ref[:, sl] = rotary(rq[:, sl]).astype(BF16)
    rk = proj(1)
    for sl in tiles:
        rk_ref[:, sl] = (rotary(rk[:, sl]) * (RET_HD ** -0.5)).astype(BF16)
    rv_ref[...] = proj(2).astype(BF16)
    rg_ref[...] = proj(3).astype(BF16)

    c = RET_CHUNK
    n_chunks = tm // c
    head_a = lax.broadcasted_iota(jnp.int32, (c, LANES), 1) < RET_HD
    bdmask = bdmask_ref[...]

    def retention(p):
        sl = tiles[p]
        kvs = []
        for ci in range(n_chunks):
            rows = slice(ci * c, (ci + 1) * c)
            q2 = rq_ref[rows, sl]
            k2 = rk_ref[rows, sl]
            v2 = rv_ref[rows, sl]
            zero = jnp.zeros_like(q2)
            qs = jnp.concatenate([jnp.where(head_a, q2, zero), jnp.where(head_a, zero, q2)], axis=0)
            scores = _dot_nt(qs, k2)
            inner = (scores * dmat_ref[p]).astype(BF16)
            intra2 = _dot(inner, v2)
            acc_ref[rows, sl] = jnp.where(head_a, intra2[:c], intra2[c:])
            kd = (k2.astype(F32) * kdec_ref[:, sl]).astype(BF16)
            kvs.append(_dot_tn(kd, v2) * bdmask)
        state = state_ref[p]
        for ci in range(n_chunks):
            rows = slice(ci * c, (ci + 1) * c)
            qd = (rq_ref[rows, sl].astype(F32) * qdec_ref[:, sl]).astype(BF16)
            acc_ref[rows, sl] += _dot(qd, state.astype(BF16))
            state = state * cdec_ref[:, sl] + kvs[ci]
        state_ref[p] = state

        o = acc_ref[:, sl]
        ssq = _half_sums(o * o)
        y = o * lax.rsqrt(ssq * (1.0 / RET_HD) + NORM_EPS) * gain_ref[:, sl]
        g = rg_ref[:, sl].astype(F32)
        ret_ref[0, :, sl] = (y * (g / (1.0 + jnp.exp(-g)))).astype(BF16)

    def qk_norm(t, g):
        ssq = _half_sums(t * t)
        return t * lax.rsqrt(ssq * (1.0 / DIFF_HD) + NORM_EPS) * g

    def diff_q():
        dq = proj(4)
        q_scale = (DIFF_HD ** -0.5) * LOG2E
        for sl in tiles:
            qn = qk_norm(dq[:, sl], gq_ref[...]) * q_scale
            dqt_ref[0, sl, :] = qn.T.astype(BF16)

    def diff_k():
        dk = proj(5)
        for sl in tiles:
            dk_ref[0, :, sl] = qk_norm(dk[:, sl], gk_ref[...]).astype(BF16)

    def diff_v():
        dv = proj(6)
        for sl in tiles:
            vt = dv[:, sl].T.astype(BF16)
            for j in range(tm // kv_block):
                dvt_ref[0, j, sl, :] = vt[:, j * kv_block:(j + 1) * kv_block]

    others = [diff_q, diff_k, diff_v]
    for p in range(n_tiles):
        retention(p)
        if p < len(others):
            others[p]()
    for f in others[n_tiles:]:
        f()


def _diffattn_kernel(qt_ref, k_ref, vt_ref, lq1_ref, lk1_ref, lq2_ref, lk2_ref, gain_ref,
                     o_ref, s_ref, smax_ref, tri_ref, m_ref, l_ref, acc_ref, *, qblock, block, strip, lead):
    qi = pl.program_id(2)
    ring = s_ref.shape[0]
    n_strips = qblock // strip
    chains = [(mi, t) for mi in range(2) for t in reversed(range(n_strips))]
    n_chains = len(chains)
    per_q = qblock // block
    qt = qt_ref[0]
    row = lax.broadcasted_iota(jnp.int32, qt.shape, 0)
    zero = jnp.zeros_like(qt)
    q_maps = (jnp.where(row < DIFF_HD, qt, zero), jnp.where(row < DIFF_HD, zero, qt))

    kpos = lax.broadcasted_iota(jnp.int32, (strip, strip), 0)
    qpos = lax.broadcasted_iota(jnp.int32, (strip, strip), 1)
    tri_ref[...] = jnp.where(kpos <= qpos, 0.0, MASK_VALUE)

    def scores(kj, c, slot, nkeys=block, tri_rows=None):
        mi, t = chains[c]
        start = pl.multiple_of(kj * block, block)
        k = k_ref[0, pl.ds(start, nkeys), :]
        sc = _dot(k, q_maps[mi][:, t * strip:(t + 1) * strip])
        if tri_rows is not None:
            own = sc[tri_rows:] + tri_ref[...]
            sc = own if tri_rows == 0 else jnp.concatenate([sc[:tri_rows], own], axis=0)
        s_ref[slot, :nkeys, :] = sc
        smax_ref[slot] = jnp.max(sc, axis=0, keepdims=True)

    def softmax_pv(kj, c, slot, nkeys=block):
        mi, t = chains[c]
        cols = slice(t * strip, (t + 1) * strip)
        s = s_ref[slot, :nkeys, :]
        m_old = m_ref[mi, :, cols]
        m_new = jnp.maximum(m_old, smax_ref[slot])
        alpha = jnp.exp2(m_old - m_new)
        p = jnp.exp2(s - m_new)
        l_ref[mi, :, cols] = alpha * l_ref[mi, :, cols] + jnp.sum(p, axis=0, keepdims=True)
        acc_ref[mi, :, cols] = (alpha * acc_ref[mi, :, cols]
                                + _dot(vt_ref[0, kj, :, :nkeys], p.astype(BF16)))
        m_ref[mi, :, cols] = m_new

    for c in range(lead):
        scores(0, c, c)
    m_ref[...] = jnp.full_like(m_ref, MASK_VALUE)
    l_ref[...] = jnp.zeros_like(l_ref)
    acc_ref[...] = jnp.zeros_like(acc_ref)

    trip = 2 * n_chains

    def body(t, carry):
        for j in range(trip):
            nxt = j + lead
            scores(2 * t + nxt // n_chains, nxt % n_chains, nxt % ring)
            softmax_pv(2 * t + j // n_chains, j % n_chains, j % ring)
        return carry

    n_full = qi * per_q
    lax.fori_loop(0, n_full // 2, body, 0)

    tail = []
    for d in range(per_q):
        for c, (mi, t) in enumerate(chains):
            nkeys = min(block, (t + 1) * strip - d * block)
            if nkeys > 0:
                own = t * strip - d * block
                tail.append((d, c, nkeys, own if own < nkeys else None))
    lam = (jnp.exp(jnp.sum(lq1_ref[...] * lk1_ref[...], axis=-1, keepdims=True))
           - jnp.exp(jnp.sum(lq2_ref[...] * lk2_ref[...], axis=-1, keepdims=True))
           + LAMBDA_INIT)

    def finish(t):
        cols = slice(t * strip, (t + 1) * strip)
        o = (acc_ref[0, :, cols] / l_ref[0, :, cols]
             - lam * (acc_ref[1, :, cols] / l_ref[1, :, cols]))
        ms = jnp.mean(o * o, axis=0, keepdims=True)
        o = o * lax.rsqrt(ms + NORM_EPS)
        o_ref[0, cols, :] = (o.T * gain_ref[...] * (1.0 - LAMBDA_INIT)).astype(BF16)

    last_use = {chains[c][1]: i for i, (_, c, _, _) in enumerate(tail)}
    for i, (d, c, nkeys, tri_rows) in enumerate(tail):
        if i < lead:
            assert d == 0 and c == i and nkeys == block and tri_rows is None
        else:
            assert tri_rows is None or (tri_rows >= 0 and tri_rows + strip == nkeys)
        if i + lead < len(tail):
            d2, c2, nkeys2, tri2 = tail[i + lead]
            scores(n_full + d2, c2, (i + lead) % ring, nkeys2, tri2)
        softmax_pv(n_full + d, c, i % ring, nkeys)
        if i == last_use[chains[c][1]]:
            finish(chains[c][1])


def _out_ffn_kernel(x_ref, ret_ref, dif_ref, wo_ref, g2_ref, wg_ref, wu_ref, wd_ref, o_ref, *,
                    ret_width, ff_chunks):
    x1 = (x_ref[0] + _dot(ret_ref[0], wo_ref[:ret_width, :])
          + _dot(dif_ref[0], wo_ref[ret_width:, :]))
    ms = jnp.mean(x1 * x1, axis=-1, keepdims=True)
    h = (x1 * lax.rsqrt(ms + NORM_EPS) * g2_ref[...]).astype(BF16)
    o_ref[0] = x1
    ff = None
    for lo, hi in ff_chunks:
        g = _dot(h, wg_ref[:, lo:hi])
        u = _dot(h, wu_ref[:, lo:hi])
        a = (g / (1.0 + jnp.exp(-g)) * u).astype(BF16)
        d = _dot(a, wd_ref[lo:hi, :])
        ff = d if ff is None else ff + d
    o_ref[0] += ff


def _resident(shape):
    return pl.BlockSpec(shape, lambda *_: (0,) * len(shape), pipeline_mode=pl.Buffered(1))


def _ff_chunks(d_ff, mxu_cols=256, target=1536):
    chunks, lo = [], 0
    while lo < d_ff:
        hi = min(d_ff, lo + target)
        if d_ff - hi and (d_ff - hi) < mxu_cols:
            hi = d_ff
        chunks.append((lo, hi))
        lo = hi
    return tuple(chunks)


def kernel(x, norm1_g, w_in, ret_norm_g, diff_q_norm_g, diff_k_norm_g, lambda_q1, lambda_k1,
           lambda_q2, lambda_k2, diff_subln_g, w_out, norm2_g, w_gate, w_up, w_down):
    B, S, D = x.shape
    depth = w_in.shape[0]
    n_ret_heads = ret_norm_g.shape[1]
    ret_width = n_ret_heads * RET_HD
    in_width = w_in.shape[2]
    width = ret_width
    assert in_width == 7 * width and w_out.shape[1] == 2 * width
    assert width % LANES == 0 and S % ROW_TILE == 0 and S % ATT_BLOCK == 0
    assert ROW_TILE % ATT_BLOCK == 0 and ROW_TILE % RET_CHUNK == 0
    assert all(w.shape[1] % (B * (S // ROW_TILE)) == 0 for w in (w_out, w_gate, w_up, w_down))
    assert S % ATT_QBLOCK == 0 and (ATT_QBLOCK // ATT_BLOCK) % 2 == 0 and ATT_QBLOCK % ATT_STRIP == 0
    assert (4 * ATT_QBLOCK // ATT_STRIP) % ATT_RING == 0 and ATT_LEAD < ATT_RING
    assert ATT_LEAD <= 2 * ATT_QBLOCK // ATT_STRIP
    n_diff_heads = width // DIFF_VD
    n_tiles = width // LANES
    d_ff = w_gate.shape[2]
    nk = S // ATT_BLOCK
    params = functools.partial(pltpu.CompilerParams, vmem_limit_bytes=VMEM_LIMIT_BYTES)

    pos = np.arange(S, dtype=np.float64)
    freqs = 1.0 / (ROPE_BASE ** (np.arange(0, RET_HD, 2, dtype=np.float64) / RET_HD))
    ang = pos[:, None] * freqs[None, :]
    cos_t = jnp.asarray(np.tile(np.cos(ang), (1, LANES // (RET_HD // 2))), F32)
    sin_half = np.sin(ang)
    sin_t = jnp.asarray(np.tile(np.concatenate([-sin_half, sin_half], axis=-1), (1, HEADS_PER_TILE)), F32)
    grp = np.arange(LANES) // RET_HD
    same_group = (grp[:, None] == grp[None, :]).astype(np.float32)
    bdmask = jnp.asarray(same_group, F32)

    log_g = np.log(1.0 - 2.0 ** (-5.0 - np.arange(n_ret_heads, dtype=np.float64)))
    idx = np.arange(RET_CHUNK, dtype=np.float64)
    rel = idx[:, None] - idx[None, :]
    decay_in = np.where(rel[None] >= 0, np.exp(log_g[:, None, None] * np.maximum(rel, 0.0)[None]), 0.0)
    dmat = jnp.asarray(decay_in.reshape(n_tiles, HEADS_PER_TILE * RET_CHUNK, RET_CHUNK), F32)
    by_lane = lambda t: jnp.asarray(np.repeat(t, RET_HD, axis=0).T, F32)
    qdec = by_lane(np.exp(log_g[:, None] * (idx[None, :] + 1.0)))
    kdec = by_lane(np.exp(log_g[:, None] * (RET_CHUNK - 1.0 - idx[None, :])))
    cdec = by_lane(np.exp(log_g * RET_CHUNK)[:, None])

    for l in range(depth):
        row = lambda b, i: (b, i, 0)
        col = lambda b, i: (b, 0, i)
        tab = lambda b, i: (i, 0)
        act = jax.ShapeDtypeStruct((B, S, width), BF16)
        act_t = jax.ShapeDtypeStruct((B, width, S), BF16)
        tile_bf16 = pltpu.VMEM((ROW_TILE, width), BF16)
        n_steps = B * (S // ROW_TILE)
        later = [w_out[l], w_gate[l], w_up[l], w_down[l]]
        later = [w.reshape(n_steps, w.shape[0] // n_steps, w.shape[1]) for w in later]
        step = lambda b, i: (b * (S // ROW_TILE) + i, 0, 0)
        cast_specs = [pl.BlockSpec((1,) + w.shape[1:], step) for w in later]
        ret, dqt, dk, dvt, wo_b, wg_b, wu_b, wd_b = pl.pallas_call(
            functools.partial(_inproj_retention_kernel, width=width, kv_block=ATT_BLOCK,
                              n_cast=len(later)),
            grid=(B, S // ROW_TILE),
            in_specs=[
                pl.BlockSpec((1, ROW_TILE, D), row),
                _resident((1, D)),
                _resident((D, in_width)),
                pl.BlockSpec((ROW_TILE, LANES), tab),
                pl.BlockSpec((ROW_TILE, LANES), tab),
                _resident((1, LANES)),
                _resident((1, LANES)),
                _resident(dmat.shape), _resident(qdec.shape), _resident(kdec.shape),
                _resident(cdec.shape), _resident(bdmask.shape), _resident((1, width)),
            ] + cast_specs,
            out_specs=[
                pl.BlockSpec((1, ROW_TILE, width), row),
                pl.BlockSpec((1, width, ROW_TILE), col),
                pl.BlockSpec((1, ROW_TILE, width), row),
                pl.BlockSpec((1, ROW_TILE // ATT_BLOCK, width, ATT_BLOCK), lambda b, i: (b, i, 0, 0)),
            ] + cast_specs,
            out_shape=[act, act_t, act, jax.ShapeDtypeStruct((B, nk, width, ATT_BLOCK), BF16)]
                      + [jax.ShapeDtypeStruct(w.shape, BF16) for w in later],
            scratch_shapes=[pltpu.VMEM((D, in_width), BF16),
                            pltpu.VMEM((n_tiles, LANES, LANES), F32),
                            tile_bf16, tile_bf16, tile_bf16, tile_bf16,
                            pltpu.VMEM((ROW_TILE, width), F32)],
            compiler_params=params(dimension_semantics=("arbitrary", "arbitrary")),
            name="inproj_retention",
        )(x, norm1_g[l][None, :], w_in[l], cos_t, sin_t,
          jnp.tile(diff_q_norm_g[l], LANES // DIFF_HD)[None, :],
          jnp.tile(diff_k_norm_g[l], LANES // DIFF_HD)[None, :],
          dmat, qdec, kdec, cdec, bdmask, ret_norm_g[l].reshape(1, width), *later)
        wo_b, wg_b, wu_b, wd_b = (b.reshape(w.shape) for b, w in
                                  zip((wo_b, wg_b, wu_b, wd_b), (w_out[l], w_gate[l], w_up[l], w_down[l])))

        lam_spec = _resident((1, DIFF_HD))
        dif = pl.pallas_call(
            functools.partial(_diffattn_kernel, qblock=ATT_QBLOCK, block=ATT_BLOCK, strip=ATT_STRIP,
                              lead=ATT_LEAD),
            grid=(B, n_diff_heads, S // ATT_QBLOCK),
            in_specs=[
                pl.BlockSpec((1, DIFF_VD, ATT_QBLOCK), lambda b, h, i: (b, h, i)),
                pl.BlockSpec((1, S, DIFF_VD), lambda b, h, i: (b, 0, h)),
                pl.BlockSpec((1, nk, DIFF_VD, ATT_BLOCK), lambda b, h, i: (b, 0, h, 0)),
                lam_spec, lam_spec, lam_spec, lam_spec,
                _resident((1, DIFF_VD)),
            ],
            out_specs=pl.BlockSpec((1, ATT_QBLOCK, DIFF_VD), lambda b, h, i: (b, i, h)),
            out_shape=act,
            scratch_shapes=[pltpu.VMEM((ATT_RING, ATT_BLOCK, ATT_STRIP), F32),
                            pltpu.VMEM((ATT_RING, 1, ATT_STRIP), F32),
                            pltpu.VMEM((ATT_STRIP, ATT_STRIP), F32),
                            pltpu.VMEM((2, 1, ATT_QBLOCK), F32),
                            pltpu.VMEM((2, 1, ATT_QBLOCK), F32),
                            pltpu.VMEM((2, DIFF_VD, ATT_QBLOCK), F32)],
            compiler_params=params(dimension_semantics=("parallel", "parallel", "arbitrary")),
            name="diffattn",
        )(dqt, dk, dvt, lambda_q1[l][None, :], lambda_k1[l][None, :], lambda_q2[l][None, :],
          lambda_k2[l][None, :], diff_subln_g[l][None, :])

        tile = pl.BlockSpec((1, ROW_TILE, D), row)
        half = pl.BlockSpec((1, ROW_TILE, width), row)
        x = pl.pallas_call(
            functools.partial(_out_ffn_kernel, ret_width=width, ff_chunks=_ff_chunks(d_ff)),
            grid=(B, S // ROW_TILE),
            in_specs=[tile, half, half,
                      _resident((2 * width, D)), _resident((1, D)),
                      _resident((D, d_ff)), _resident((D, d_ff)), _resident((d_ff, D))],
            out_specs=tile,
            out_shape=jax.ShapeDtypeStruct((B, S, D), F32),
            compiler_params=params(dimension_semantics=("parallel", "parallel")),
            name="out_ffn",
        )(x, ret, dif, wo_b, norm2_g[l][None, :], wg_b, wu_b, wd_b)
    return x
```

```python
import functools
import math

import jax
import jax.numpy as jnp
import numpy as np
from jax import lax
from jax.experimental import pallas as pl
from jax.experimental.pallas import tpu as pltpu

F32 = jnp.float32
BF16 = jnp.bfloat16

LANES = 128
VMEM_LIMIT_BYTES = 56 * 1024 * 1024

RET_HD = 64
DIFF_HD = 64
DIFF_VD = 2 * DIFF_HD
HEADS_PER_TILE = LANES // RET_HD
RET_CHUNK = 256
ROPE_BASE = 10000.0
NORM_EPS = 1e-6
LAMBDA_INIT = 0.8 - 0.6 * math.exp(-0.3 * 0)
MASK_VALUE = -1e30
LOG2E = 1.0 / math.log(2.0)

ROW_TILE = 512
ATT_QBLOCK = 4096
ATT_BLOCK = 512
ATT_STRIP = 256
ATT_RING = 16
ATT_LEAD = 4


def _dot(a, b):
    return jnp.dot(a, b, preferred_element_type=F32)


def _dot_nt(a, b):
    return lax.dot_general(a, b, (((1,), (1,)), ((), ())), preferred_element_type=F32)


def _dot_tn(a, b):
    return lax.dot_general(a, b, (((0,), (0,)), ((), ())), preferred_element_type=F32)


def _half_sums(sq):
    first = lax.broadcasted_iota(jnp.int32, sq.shape, 1) < (LANES // 2)
    a = jnp.sum(jnp.where(first, sq, 0.0), axis=-1, keepdims=True)
    b = jnp.sum(jnp.where(first, 0.0, sq), axis=-1, keepdims=True)
    return jnp.where(first, a, b)


def _inproj_retention_kernel(x_ref, g1_ref, w32_ref, cos_ref, sin_ref, gq_ref, gk_ref,
                             dmat_ref, qdec_ref, kdec_ref, cdec_ref, bdmask_ref, gain_ref,
                             *rest, width, kv_block, n_cast):
    cast_in, rest = rest[:n_cast], rest[n_cast:]
    (ret_ref, dqt_ref, dk_ref, dvt_ref), rest = rest[:4], rest[4:]
    cast_out, rest = rest[:n_cast], rest[n_cast:]
    w_ref, state_ref, rq_ref, rk_ref, rv_ref, rg_ref, acc_ref = rest

    @pl.when((pl.program_id(0) == 0) & (pl.program_id(1) == 0))
    def _():
        w_ref[...] = w32_ref[...].astype(BF16)

    @pl.when(pl.program_id(1) == 0)
    def _():
        state_ref[...] = jnp.zeros_like(state_ref)

    for src_ref, dst_ref in zip(cast_in, cast_out):
        dst_ref[...] = src_ref[...].astype(BF16)

    x = x_ref[0]
    tm = x.shape[0]
    ms = jnp.mean(x * x, axis=-1, keepdims=True)
    h = (x * lax.rsqrt(ms + NORM_EPS) * g1_ref[...]).astype(BF16)

    def proj(j):
        return _dot(h, w_ref[:, j * width:(j + 1) * width])

    n_tiles = width // LANES
    tiles = [slice(t * LANES, (t + 1) * LANES) for t in range(n_tiles)]
    cos = cos_ref[...]
    sin = sin_ref[...]
    lane = lax.broadcasted_iota(jnp.int32, (tm, LANES), 1)
    first_half = (lane % RET_HD) < (RET_HD // 2)

    def rotary(t):
        partner = jnp.where(first_half, pltpu.roll(t, LANES - RET_HD // 2, 1),
                            pltpu.roll(t, RET_HD // 2, 1))
        return t * cos + partner * sin

    rq = proj(0)
    for sl in tiles:
        rq_ref[:, sl] = rotary(rq[:, sl]).astype(BF16)
    rk = proj(1)
    for sl in tiles:
        rk_ref[:, sl] = (rotary(rk[:, sl]) * (RET_HD ** -0.5)).astype(BF16)
    rv_ref[...] = proj(2).astype(BF16)
    rg_ref[...] = proj(3).astype(BF16)

    c = RET_CHUNK
    n_chunks = tm // c
    head_a = lax.broadcasted_iota(jnp.int32, (c, LANES), 1) < RET_HD
    bdmask = bdmask_ref[...]

    def retention(p):
        sl = tiles[p]
        kvs = []
        for ci in range(n_chunks):
            rows = slice(ci * c, (ci + 1) * c)
            q2 = rq_ref[rows, sl]
            k2 = rk_ref[rows, sl]
            v2 = rv_ref[rows, sl]
            zero = jnp.zeros_like(q2)
            qs = jnp.concatenate([jnp.where(head_a, q2, zero), jnp.where(head_a, zero, q2)], axis=0)
            scores = _dot_nt(qs, k2)
            inner = (scores * dmat_ref[p]).astype(BF16)
            intra2 = _dot(inner, v2)
            acc_ref[rows, sl] = jnp.where(head_a, intra2[:c], intra2[c:])
            kd = (k2.astype(F32) * kdec_ref[:, sl]).astype(BF16)
            kvs.append(_dot_tn(kd, v2) * bdmask)
        state = state_ref[p]
        for ci in range(n_chunks):
            rows = slice(ci * c, (ci + 1) * c)
            qd = (rq_ref[rows, sl].astype(F32) * qdec_ref[:, sl]).astype(BF16)
            acc_ref[rows, sl] += _dot(qd, state.astype(BF16))
            state = state * cdec_ref[:, sl] + kvs[ci]
        state_ref[p] = state

        o = acc_ref[:, sl]
        ssq = _half_sums(o * o)
        y = o * lax.rsqrt(ssq * (1.0 / RET_HD) + NORM_EPS) * gain_ref[:, sl]
        g = rg_ref[:, sl].astype(F32)
        ret_ref[0, :, sl] = (y * (g / (1.0 + jnp.exp(-g)))).astype(BF16)

    def qk_norm(t, g):
        ssq = _half_sums(t * t)
        return t * lax.rsqrt(ssq * (1.0 / DIFF_HD) + NORM_EPS) * g

    def diff_q():
        dq = proj(4)
        q_scale = (DIFF_HD ** -0.5) * LOG2E
        for sl in tiles:
            qn = qk_norm(dq[:, sl], gq_ref[...]) * q_scale
            dqt_ref[0, sl, :] = qn.T.astype(BF16)

    def diff_k():
        dk = proj(5)
        for sl in tiles:
            dk_ref[0, :, sl] = qk_norm(dk[:, sl], gk_ref[...]).astype(BF16)

    def diff_v():
        dv = proj(6)
        for sl in tiles:
            vt = dv[:, sl].T.astype(BF16)
            for j in range(tm // kv_block):
                dvt_ref[0, j, sl, :] = vt[:, j * kv_block:(j + 1) * kv_block]

    others = [diff_q, diff_k, diff_v]
    for p in range(n_tiles):
        retention(p)
        if p < len(others):
            others[p]()
    for f in others[n_tiles:]:
        f()


def _diffattn_kernel(qt_ref, k_ref, vt_ref, lq1_ref, lk1_ref, lq2_ref, lk2_ref, gain_ref,
                     o_ref, s_ref, smax_ref, tri_ref, m_ref, l_ref, acc_ref, *, qblock, block, strip, lead):
    qi = pl.program_id(2)
    ring = s_ref.shape[0]
    n_strips = qblock // strip
    chains = [(mi, t) for mi in range(2) for t in reversed(range(n_strips))]
    n_chains = len(chains)
    per_q = qblock // block
    qt = qt_ref[0]
    row = lax.broadcasted_iota(jnp.int32, qt.shape, 0)
    zero = jnp.zeros_like(qt)
    q_maps = (jnp.where(row < DIFF_HD, qt, zero), jnp.where(row < DIFF_HD, zero, qt))

    kpos = lax.broadcasted_iota(jnp.int32, (strip, strip), 0)
    qpos = lax.broadcasted_iota(jnp.int32, (strip, strip), 1)
    tri_ref[...] = jnp.where(kpos <= qpos, 0.0, MASK_VALUE)

    def scores(kj, c, slot, nkeys=block, tri_rows=None):
        mi, t = chains[c]
        start = pl.multiple_of(kj * block, block)
        k = k_ref[0, pl.ds(start, nkeys), :]
        sc = _dot(k, q_maps[mi][:, t * strip:(t + 1) * strip])
        if tri_rows is not None:
            own = sc[tri_rows:] + tri_ref[...]
            sc = own if tri_rows == 0 else jnp.concatenate([sc[:tri_rows], own], axis=0)
        s_ref[slot, :nkeys, :] = sc
        smax_ref[slot] = jnp.max(sc, axis=0, keepdims=True)

    def softmax_pv(kj, c, slot, nkeys=block):
        mi, t = chains[c]
        cols = slice(t * strip, (t + 1) * strip)
        s = s_ref[slot, :nkeys, :]
        m_old = m_ref[mi, :, cols]
        m_new = jnp.maximum(m_old, smax_ref[slot])
        alpha = jnp.exp2(m_old - m_new)
        p = jnp.exp2(s - m_new)
        l_ref[mi, :, cols] = alpha * l_ref[mi, :, cols] + jnp.sum(p, axis=0, keepdims=True)
        acc_ref[mi, :, cols] = (alpha * acc_ref[mi, :, cols]
                                + _dot(vt_ref[0, kj, :, :nkeys], p.astype(BF16)))
        m_ref[mi, :, cols] = m_new

    for c in range(lead):
        scores(0, c, c)
    m_ref[...] = jnp.full_like(m_ref, MASK_VALUE)
    l_ref[...] = jnp.zeros_like(l_ref)
    acc_ref[...] = jnp.zeros_like(acc_ref)

    trip = 2 * n_chains

    def body(t, carry):
        for j in range(trip):
            nxt = j + lead
            scores(2 * t + nxt // n_chains, nxt % n_chains, nxt % ring)
            softmax_pv(2 * t + j // n_chains, j % n_chains, j % ring)
        return carry

    n_full = qi * per_q
    lax.fori_loop(0, n_full // 2, body, 0)

    tail = []
    for d in range(per_q):
        for c, (mi, t) in enumerate(chains):
            nkeys = min(block, (t + 1) * strip - d * block)
            if nkeys > 0:
                own = t * strip - d * block
                tail.append((d, c, nkeys, own if own < nkeys else None))
    lam = (jnp.exp(jnp.sum(lq1_ref[...] * lk1_ref[...], axis=-1, keepdims=True))
           - jnp.exp(jnp.sum(lq2_ref[...] * lk2_ref[...], axis=-1, keepdims=True))
           + LAMBDA_INIT)

    def finish(t):
        cols = slice(t * strip, (t + 1) * strip)
        o = (acc_ref[0, :, cols] / l_ref[0, :, cols]
             - lam * (acc_ref[1, :, cols] / l_ref[1, :, cols]))
        ms = jnp.mean(o * o, axis=0, keepdims=True)
        o = o * lax.rsqrt(ms + NORM_EPS)
        o_ref[0, cols, :] = (o.T * gain_ref[...] * (1.0 - LAMBDA_INIT)).astype(BF16)

    last_use = {chains[c][1]: i for i, (_, c, _, _) in enumerate(tail)}
    for i, (d, c, nkeys, tri_rows) in enumerate(tail):
        if i < lead:
            assert d == 0 and c == i and nkeys == block and tri_rows is None
        else:
            assert tri_rows is None or (tri_rows >= 0 and tri_rows + strip == nkeys)
        if i + lead < len(tail):
            d2, c2, nkeys2, tri2 = tail[i + lead]
            scores(n_full + d2, c2, (i + lead) % ring, nkeys2, tri2)
        softmax_pv(n_full + d, c, i % ring, nkeys)
        if i == last_use[chains[c][1]]:
            finish(chains[c][1])


def _out_ffn_kernel(x_ref, ret_ref, dif_ref, wo_ref, g2_ref, wg_ref, wu_ref, wd_ref, o_ref, *,
                    ret_width, ff_chunks):
    x1 = (x_ref[0] + _dot(ret_ref[0], wo_ref[:ret_width, :])
          + _dot(dif_ref[0], wo_ref[ret_width:, :]))
    ms = jnp.mean(x1 * x1, axis=-1, keepdims=True)
    h = (x1 * lax.rsqrt(ms + NORM_EPS) * g2_ref[...]).astype(BF16)
    o_ref[0] = x1
    ff = None
    for lo, hi in ff_chunks:
        g = _dot(h, wg_ref[:, lo:hi])
        u = _dot(h, wu_ref[:, lo:hi])
        a = (g / (1.0 + jnp.exp(-g)) * u).astype(BF16)
        d = _dot(a, wd_ref[lo:hi, :])
        ff = d if ff is None else ff + d
    o_ref[0] += ff


def _resident(shape):
    return pl.BlockSpec(shape, lambda *_: (0,) * len(shape), pipeline_mode=pl.Buffered(1))


def _ff_chunks(d_ff, mxu_cols=256, target=1536):
    chunks, lo = [], 0
    while lo < d_ff:
        hi = min(d_ff, lo + target)
        if d_ff - hi and (d_ff - hi) < mxu_cols:
            hi = d_ff
        chunks.append((lo, hi))
        lo = hi
    return tuple(chunks)


def kernel(x, norm1_g, w_in, ret_norm_g, diff_q_norm_g, diff_k_norm_g, lambda_q1, lambda_k1,
           lambda_q2, lambda_k2, diff_subln_g, w_out, norm2_g, w_gate, w_up, w_down):
    B, S, D = x.shape
    depth = w_in.shape[0]
    n_ret_heads = ret_norm_g.shape[1]
    ret_width = n_ret_heads * RET_HD
    in_width = w_in.shape[2]
    width = ret_width
    assert in_width == 7 * width and w_out.shape[1] == 2 * width
    assert width % LANES == 0 and S % ROW_TILE == 0 and S % ATT_BLOCK == 0
    assert ROW_TILE % ATT_BLOCK == 0 and ROW_TILE % RET_CHUNK == 0
    assert all(w.shape[1] % (B * (S // ROW_TILE)) == 0 for w in (w_out, w_gate, w_up, w_down))
    assert S % ATT_QBLOCK == 0 and (ATT_QBLOCK // ATT_BLOCK) % 2 == 0 and ATT_QBLOCK % ATT_STRIP == 0
    assert (4 * ATT_QBLOCK // ATT_STRIP) % ATT_RING == 0 and ATT_LEAD < ATT_RING
    assert ATT_LEAD <= 2 * ATT_QBLOCK // ATT_STRIP
    n_diff_heads = width // DIFF_VD
    n_tiles = width // LANES
    d_ff = w_gate.shape[2]
    nk = S // ATT_BLOCK
    params = functools.partial(pltpu.CompilerParams, vmem_limit_bytes=VMEM_LIMIT_BYTES)

    pos = np.arange(S, dtype=np.float64)
    freqs = 1.0 / (ROPE_BASE ** (np.arange(0, RET_HD, 2, dtype=np.float64) / RET_HD))
    ang = pos[:, None] * freqs[None, :]
    cos_t = jnp.asarray(np.tile(np.cos(ang), (1, LANES // (RET_HD // 2))), F32)
    sin_half = np.sin(ang)
    sin_t = jnp.asarray(np.tile(np.concatenate([-sin_half, sin_half], axis=-1), (1, HEADS_PER_TILE)), F32)
    grp = np.arange(LANES) // RET_HD
    same_group = (grp[:, None] == grp[None, :]).astype(np.float32)
    bdmask = jnp.asarray(same_group, F32)

    log_g = np.log(1.0 - 2.0 ** (-5.0 - np.arange(n_ret_heads, dtype=np.float64)))
    idx = np.arange(RET_CHUNK, dtype=np.float64)
    rel = idx[:, None] - idx[None, :]
    decay_in = np.where(rel[None] >= 0, np.exp(log_g[:, None, None] * np.maximum(rel, 0.0)[None]), 0.0)
    dmat = jnp.asarray(decay_in.reshape(n_tiles, HEADS_PER_TILE * RET_CHUNK, RET_CHUNK), F32)
    by_lane = lambda t: jnp.asarray(np.repeat(t, RET_HD, axis=0).T, F32)
    qdec = by_lane(np.exp(log_g[:, None] * (idx[None, :] + 1.0)))
    kdec = by_lane(np.exp(log_g[:, None] * (RET_CHUNK - 1.0 - idx[None, :])))
    cdec = by_lane(np.exp(log_g * RET_CHUNK)[:, None])

    for l in range(depth):
        row = lambda b, i: (b, i, 0)
        col = lambda b, i: (b, 0, i)
        tab = lambda b, i: (i, 0)
        act = jax.ShapeDtypeStruct((B, S, width), BF16)
        act_t = jax.ShapeDtypeStruct((B, width, S), BF16)
        tile_bf16 = pltpu.VMEM((ROW_TILE, width), BF16)
        n_steps = B * (S // ROW_TILE)
        later = [w_out[l], w_gate[l], w_up[l], w_down[l]]
        later = [w.reshape(n_steps, w.shape[0] // n_steps, w.shape[1]) for w in later]
        step = lambda b, i: (b * (S // ROW_TILE) + i, 0, 0)
        cast_specs = [pl.BlockSpec((1,) + w.shape[1:], step) for w in later]
        ret, dqt, dk, dvt, wo_b, wg_b, wu_b, wd_b = pl.pallas_call(
            functools.partial(_inproj_retention_kernel, width=width, kv_block=ATT_BLOCK,
                              n_cast=len(later)),
            grid=(B, S // ROW_TILE),
            in_specs=[
                pl.BlockSpec((1, ROW_TILE, D), row),
                _resident((1, D)),
                _resident((D, in_width)),
                pl.BlockSpec((ROW_TILE, LANES), tab),
                pl.BlockSpec((ROW_TILE, LANES), tab),
                _resident((1, LANES)),
                _resident((1, LANES)),
                _resident(dmat.shape), _resident(qdec.shape), _resident(kdec.shape),
                _resident(cdec.shape), _resident(bdmask.shape), _resident((1, width)),
            ] + cast_specs,
            out_specs=[
                pl.BlockSpec((1, ROW_TILE, width), row),
                pl.BlockSpec((1, width, ROW_TILE), col),
                pl.BlockSpec((1, ROW_TILE, width), row),
                pl.BlockSpec((1, ROW_TILE // ATT_BLOCK, width, ATT_BLOCK), lambda b, i: (b, i, 0, 0)),
            ] + cast_specs,
            out_shape=[act, act_t, act, jax.ShapeDtypeStruct((B, nk, width, ATT_BLOCK), BF16)]
                      + [jax.ShapeDtypeStruct(w.shape, BF16) for w in later],
            scratch_shapes=[pltpu.VMEM((D, in_width), BF16),
                            pltpu.VMEM((n_tiles, LANES, LANES), F32),
                            tile_bf16, tile_bf16, tile_bf16, tile_bf16,
                            pltpu.VMEM((ROW_TILE, width), F32)],
            compiler_params=params(dimension_semantics=("arbitrary", "arbitrary")),
            name="inproj_retention",
        )(x, norm1_g[l][None, :], w_in[l], cos_t, sin_t,
          jnp.tile(diff_q_norm_g[l], LANES // DIFF_HD)[None, :],
          jnp.tile(diff_k_norm_g[l], LANES // DIFF_HD)[None, :],
          dmat, qdec, kdec, cdec, bdmask, ret_norm_g[l].reshape(1, width), *later)
        wo_b, wg_b, wu_b, wd_b = (b.reshape(w.shape) for b, w in
                                  zip((wo_b, wg_b, wu_b, wd_b), (w_out[l], w_gate[l], w_up[l], w_down[l])))

        lam_spec = _resident((1, DIFF_HD))
        dif = pl.pallas_call(
            functools.partial(_diffattn_kernel, qblock=ATT_QBLOCK, block=ATT_BLOCK, strip=ATT_STRIP,
                              lead=ATT_LEAD),
            grid=(B, n_diff_heads, S // ATT_QBLOCK),
            in_specs=[
                pl.BlockSpec((1, DIFF_VD, ATT_QBLOCK), lambda b, h, i: (b, h, i)),
                pl.BlockSpec((1, S, DIFF_VD), lambda b, h, i: (b, 0, h)),
                pl.BlockSpec((1, nk, DIFF_VD, ATT_BLOCK), lambda b, h, i: (b, 0, h, 0)),
                lam_spec, lam_spec, lam_spec, lam_spec,
                _resident((1, DIFF_VD)),
            ],
            out_specs=pl.BlockSpec((1, ATT_QBLOCK, DIFF_VD), lambda b, h, i: (b, i, h)),
            out_shape=act,
            scratch_shapes=[pltpu.VMEM((ATT_RING, ATT_BLOCK, ATT_STRIP), F32),
                            pltpu.VMEM((ATT_RING, 1, ATT_STRIP), F32),
                            pltpu.VMEM((ATT_STRIP, ATT_STRIP), F32),
                            pltpu.VMEM((2, 1, ATT_QBLOCK), F32),
                            pltpu.VMEM((2, 1, ATT_QBLOCK), F32),
                            pltpu.VMEM((2, DIFF_VD, ATT_QBLOCK), F32)],
            compiler_params=params(dimension_semantics=("parallel", "parallel", "arbitrary")),
            name="diffattn",
        )(dqt, dk, dvt, lambda_q1[l][None, :], lambda_k1[l][None, :], lambda_q2[l][None, :],
          lambda_k2[l][None, :], diff_subln_g[l][None, :])

        tile = pl.BlockSpec((1, ROW_TILE, D), row)
        half = pl.BlockSpec((1, ROW_TILE, width), row)
        x = pl.pallas_call(
            functools.partial(_out_ffn_kernel, ret_width=width, ff_chunks=_ff_chunks(d_ff)),
            grid=(B, S // ROW_TILE),
            in_specs=[tile, half, half,
                      _resident((2 * width, D)), _resident((1, D)),
                      _resident((D, d_ff)), _resident((D, d_ff)), _resident((d_ff, D))],
            out_specs=tile,
            out_shape=jax.ShapeDtypeStruct((B, S, D), F32),
            compiler_params=params(dimension_semantics=("parallel", "parallel")),
            name="out_ffn",
        )(x, ret, dif, wo_b, norm2_g[l][None, :], wg_b, wu_b, wd_b)
    return x
```

```python
import functools
import math

import jax
import jax.numpy as jnp
import numpy as np
from jax import lax
from jax.experimental import pallas as pl
from jax.experimental.pallas import tpu as pltpu

F32 = jnp.float32
BF16 = jnp.bfloat16

LANES = 128
VMEM_LIMIT_BYTES = 56 * 1024 * 1024

RET_HD = 64
DIFF_HD = 64
DIFF_VD = 2 * DIFF_HD
HEADS_PER_TILE = LANES // RET_HD
RET_CHUNK = 256
ROPE_BASE = 10000.0
NORM_EPS = 1e-6
LAMBDA_INIT = 0.8 - 0.6 * math.exp(-0.3 * 0)
MASK_VALUE = -1e30
LOG2E = 1.0 / math.log(2.0)

ROW_TILE = 512
ATT_QBLOCK = 4096
ATT_BLOCK = 512
ATT_STRIP = 256
ATT_RING = 16
ATT_LEAD = 4


def _dot(a, b):
    return jnp.dot(a, b, preferred_element_type=F32)


def _dot_nt(a, b):
    return lax.dot_general(a, b, (((1,), (1,)), ((), ())), preferred_element_type=F32)


def _dot_tn(a, b):
    return lax.dot_general(a, b, (((0,), (0,)), ((), ())), preferred_element_type=F32)


def _half_sums(sq):
    first = lax.broadcasted_iota(jnp.int32, sq.shape, 1) < (LANES // 2)
    a = jnp.sum(jnp.where(first, sq, 0.0), axis=-1, keepdims=True)
    b = jnp.sum(jnp.where(first, 0.0, sq), axis=-1, keepdims=True)
    return jnp.where(first, a, b)


def _inproj_retention_kernel(x_ref, g1_ref, w32_ref, cos_ref, sin_ref, gq_ref, gk_ref,
                             dmat_ref, qdec_ref, kdec_ref, cdec_ref, bdmask_ref, gain_ref,
                             *rest, width, kv_block, n_cast):
    cast_in, rest = rest[:n_cast], rest[n_cast:]
    (ret_ref, dqt_ref, dk_ref, dvt_ref), rest = rest[:4], rest[4:]
    cast_out, rest = rest[:n_cast], rest[n_cast:]
    w_ref, state_ref, rq_ref, rk_ref, rv_ref, rg_ref, acc_ref = rest

    @pl.when((pl.program_id(0) == 0) & (pl.program_id(1) == 0))
    def _():
        w_ref[...] = w32_ref[...].astype(BF16)

    @pl.when(pl.program_id(1) == 0)
    def _():
        state_ref[...] = jnp.zeros_like(state_ref)

    for src_ref, dst_ref in zip(cast_in, cast_out):
        dst_ref[...] = src_ref[...].astype(BF16)

    x = x_ref[0]
    tm = x.shape[0]
    ms = jnp.mean(x * x, axis=-1, keepdims=True)
    h = (x * lax.rsqrt(ms + NORM_EPS) * g1_ref[...]).astype(BF16)

    def proj(j):
        return _dot(h, w_ref[:, j * width:(j + 1) * width])

    n_tiles = width // LANES
    tiles = [slice(t * LANES, (t + 1) * LANES) for t in range(n_tiles)]
    cos = cos_ref[...]
    sin = sin_ref[...]
    lane = lax.broadcasted_iota(jnp.int32, (tm, LANES), 1)
    first_half = (lane % RET_HD) < (RET_HD // 2)

    def rotary(t):
        partner = jnp.where(first_half, pltpu.roll(t, LANES - RET_HD // 2, 1),
                            pltpu.roll(t, RET_HD // 2, 1))
        return t * cos + partner * sin

    rq = proj(0)
    for sl in tiles:
        rq_ref[:, sl] = rotary(rq[:, sl]).astype(BF16)
    rk = proj(1)
    for sl in tiles:
        rk_ref[:, sl] = (rotary(rk[:, sl]) * (RET_HD ** -0.5)).astype(BF16)
    rv_ref[...] = proj(2).astype(BF16)
    rg_ref[...] = proj(3).astype(BF16)

    c = RET_CHUNK
    n_chunks = tm // c
    head_a = lax.broadcasted_iota(jnp.int32, (c, LANES), 1) < RET_HD
    bdmask = bdmask_ref[...]

    def retention(p):
        sl = tiles[p]
        kvs = []
        for ci in range(n_chunks):
            rows = slice(ci * c, (ci + 1) * c)
            q2 = rq_ref[rows, sl]
            k2 = rk_ref[rows, sl]
            v2 = rv_ref[rows, sl]
            zero = jnp.zeros_like(q2)
            qs = jnp.concatenate([jnp.where(head_a, q2, zero), jnp.where(head_a, zero, q2)], axis=0)
            scores = _dot_nt(qs, k2)
            inner = (scores * dmat_ref[p]).astype(BF16)
            intra2 = _dot(inner, v2)
            acc_ref[rows, sl] = jnp.where(head_a, intra2[:c], intra2[c:])
            kd = (k2.astype(F32) * kdec_ref[:, sl]).astype(BF16)
            kvs.append(_dot_tn(kd, v2) * bdmask)
        state = state_ref[p]
        for ci in range(n_chunks):
            rows = slice(ci * c, (ci + 1) * c)
            qd = (rq_ref[rows, sl].astype(F32) * qdec_ref[:, sl]).astype(BF16)
            acc_ref[rows, sl] += _dot(qd, state.astype(BF16))
            state = state * cdec_ref[:, sl] + kvs[ci]
        state_ref[p] = state

        o = acc_ref[:, sl]
        ssq = _half_sums(o * o)
        y = o * lax.rsqrt(ssq * (1.0 / RET_HD) + NORM_EPS) * gain_ref[:, sl]
        g = rg_ref[:, sl].astype(F32)
        ret_ref[0, :, sl] = (y * (g / (1.0 + jnp.exp(-g)))).astype(BF16)

    def qk_norm(t, g):
        ssq = _half_sums(t * t)
        return t * lax.rsqrt(ssq * (1.0 / DIFF_HD) + NORM_EPS) * g

    def diff_q():
        dq = proj(4)
        q_scale = (DIFF_HD ** -0.5) * LOG2E
        for sl in tiles:
            qn = qk_norm(dq[:, sl], gq_ref[...]) * q_scale
            dqt_ref[0, sl, :] = qn.T.astype(BF16)

    def diff_k():
        dk = proj(5)
        for sl in tiles:
            dk_ref[0, :, sl] = qk_norm(dk[:, sl], gk_ref[...]).astype(BF16)

    def diff_v():
        dv = proj(6)
        for sl in tiles:
            vt = dv[:, sl].T.astype(BF16)
            for j in range(tm // kv_block):
                dvt_ref[0, j, sl, :] = vt[:, j * kv_block:(j + 1) * kv_block]

    others = [diff_q, diff_k, diff_v]
    for p in range(n_tiles):
        retention(p)
        if p < len(others):
            others[p]()
    for f in others[n_tiles:]:
        f()


def _diffattn_kernel(qt_ref, k_ref, vt_ref, lq1_ref, lk1_ref, lq2_ref, lk2_ref, gain_ref,
                     o_ref, s_ref, smax_ref, tri_ref, m_ref, l_ref, acc_ref, *, qblock, block, strip, lead):
    qi = pl.program_id(2)
    ring = s_ref.shape[0]
    n_strips = qblock // strip
    chains = [(mi, t) for mi in range(2) for t in reversed(range(n_strips))]
    n_chains = len(chains)
    per_q = qblock // block
    qt = qt_ref[0]
    row = lax.broadcasted_iota(jnp.int32, qt.shape, 0)
    zero = jnp.zeros_like(qt)
    q_maps = (jnp.where(row < DIFF_HD, qt, zero), jnp.where(row < DIFF_HD, zero, qt))

    kpos = lax.broadcasted_iota(jnp.int32, (strip, strip), 0)
    qpos = lax.broadcasted_iota(jnp.int32, (strip, strip), 1)
    tri_ref[...] = jnp.where(kpos <= qpos, 0.0, MASK_VALUE)

    def scores(kj, c, slot, nkeys=block, tri_rows=None):
        mi, t = chains[c]
        start = pl.multiple_of(kj * block, block)
        k = k_ref[0, pl.ds(start, nkeys), :]
        sc = _dot(k, q_maps[mi][:, t * strip:(t + 1) * strip])
        if tri_rows is not None:
            own = sc[tri_rows:] + tri_ref[...]
            sc = own if tri_rows == 0 else jnp.concatenate([sc[:tri_rows], own], axis=0)
        s_ref[slot, :nkeys, :] = sc
        smax_ref[slot] = jnp.max(sc, axis=0, keepdims=True)

    def softmax_pv(kj, c, slot, nkeys=block, sum_on_mxu=False):
        mi, t = chains[c]
        cols = slice(t * strip, (t + 1) * strip)
        s = s_ref[slot, :nkeys, :]
        m_old = m_ref[mi, :, cols]
        m_new = jnp.maximum(m_old, smax_ref[slot])
        alpha = jnp.exp2(m_old - m_new)
        p = jnp.exp2(s - m_new)
        vt = vt_ref[0, kj, :, :nkeys]
        if sum_on_mxu:
            vt = jnp.concatenate([vt, jnp.ones((16, nkeys), BF16)], axis=0)
            pv = _dot(vt, p.astype(BF16))
            p_sum, pv = pv[DIFF_VD:DIFF_VD + 1], pv[:DIFF_VD]
        else:
            p_sum, pv = jnp.sum(p, axis=0, keepdims=True), _dot(vt, p.astype(BF16))
        l_ref[mi, :, cols] = alpha * l_ref[mi, :, cols] + p_sum
        acc_ref[mi, :, cols] = alpha * acc_ref[mi, :, cols] + pv
        m_ref[mi, :, cols] = m_new

    for c in range(lead):
        scores(0, c, c)
    m_ref[...] = jnp.full_like(m_ref, MASK_VALUE)
    l_ref[...] = jnp.zeros_like(l_ref)
    acc_ref[...] = jnp.zeros_like(acc_ref)

    trip = 2 * n_chains

    def body(t, carry):
        for j in range(trip):
            nxt = j + lead
            scores(2 * t + nxt // n_chains, nxt % n_chains, nxt % ring)
            softmax_pv(2 * t + j // n_chains, j % n_chains, j % ring)
        return carry

    n_full = qi * per_q
    lax.fori_loop(0, n_full // 2, body, 0)

    tail = []
    for d in range(per_q):
        for c, (mi, t) in enumerate(chains):
            nkeys = min(block, (t + 1) * strip - d * block)
            if nkeys > 0:
                own = t * strip - d * block
                tail.append((d, c, nkeys, own if own < nkeys else None))
    lam = (jnp.exp(jnp.sum(lq1_ref[...] * lk1_ref[...], axis=-1, keepdims=True))
           - jnp.exp(jnp.sum(lq2_ref[...] * lk2_ref[...], axis=-1, keepdims=True))
           + LAMBDA_INIT)

    def finish(t):
        cols = slice(t * strip, (t + 1) * strip)
        o = (acc_ref[0, :, cols] / l_ref[0, :, cols]
             - lam * (acc_ref[1, :, cols] / l_ref[1, :, cols]))
        ms = jnp.mean(o * o, axis=0, keepdims=True)
        o = o * lax.rsqrt(ms + NORM_EPS)
        o_ref[0, cols, :] = (o.T * gain_ref[...] * (1.0 - LAMBDA_INIT)).astype(BF16)

    last_use = {chains[c][1]: i for i, (_, c, _, _) in enumerate(tail)}
    for i, (d, c, nkeys, tri_rows) in enumerate(tail):
        if i < lead:
            assert d == 0 and c == i and nkeys == block and tri_rows is None
        else:
            assert tri_rows is None or (tri_rows >= 0 and tri_rows + strip == nkeys)
        if i + lead < len(tail):
            d2, c2, nkeys2, tri2 = tail[i + lead]
            scores(n_full + d2, c2, (i + lead) % ring, nkeys2, tri2)
        softmax_pv(n_full + d, c, i % ring, nkeys, sum_on_mxu=True)
        if i == last_use[chains[c][1]]:
            finish(chains[c][1])


def _out_ffn_kernel(x_ref, ret_ref, dif_ref, wo_ref, g2_ref, wg_ref, wu_ref, wd_ref, o_ref, *,
                    ret_width, ff_chunks):
    x1 = (x_ref[0] + _dot(ret_ref[0], wo_ref[:ret_width, :])
          + _dot(dif_ref[0], wo_ref[ret_width:, :]))
    ms = jnp.mean(x1 * x1, axis=-1, keepdims=True)
    h = (x1 * lax.rsqrt(ms + NORM_EPS) * g2_ref[...]).astype(BF16)
    o_ref[0] = x1
    ff = None
    for lo, hi in ff_chunks:
        g = _dot(h, wg_ref[:, lo:hi])
        u = _dot(h, wu_ref[:, lo:hi])
        a = (g / (1.0 + jnp.exp(-g)) * u).astype(BF16)
        d = _dot(a, wd_ref[lo:hi, :])
        ff = d if ff is None else ff + d
    o_ref[0] += ff


def _resident(shape):
    return pl.BlockSpec(shape, lambda *_: (0,) * len(shape), pipeline_mode=pl.Buffered(1))


def _ff_chunks(d_ff, mxu_cols=256, target=1536):
    chunks, lo = [], 0
    while lo < d_ff:
        hi = min(d_ff, lo + target)
        if d_ff - hi and (d_ff - hi) < mxu_cols:
            hi = d_ff
        chunks.append((lo, hi))
        lo = hi
    return tuple(chunks)


def kernel(x, norm1_g, w_in, ret_norm_g, diff_q_norm_g, diff_k_norm_g, lambda_q1, lambda_k1,
           lambda_q2, lambda_k2, diff_subln_g, w_out, norm2_g, w_gate, w_up, w_down):
    B, S, D = x.shape
    depth = w_in.shape[0]
    n_ret_heads = ret_norm_g.shape[1]
    ret_width = n_ret_heads * RET_HD
    in_width = w_in.shape[2]
    width = ret_width
    assert in_width == 7 * width and w_out.shape[1] == 2 * width
    assert width % LANES == 0 and S % ROW_TILE == 0 and S % ATT_BLOCK == 0
    assert ROW_TILE % ATT_BLOCK == 0 and ROW_TILE % RET_CHUNK == 0
    assert all(w.shape[1] % (B * (S // ROW_TILE)) == 0 for w in (w_out, w_gate, w_up, w_down))
    assert S % ATT_QBLOCK == 0 and (ATT_QBLOCK // ATT_BLOCK) % 2 == 0 and ATT_QBLOCK % ATT_STRIP == 0
    assert (4 * ATT_QBLOCK // ATT_STRIP) % ATT_RING == 0 and ATT_LEAD < ATT_RING
    assert ATT_LEAD <= 2 * ATT_QBLOCK // ATT_STRIP
    n_diff_heads = width // DIFF_VD
    n_tiles = width // LANES
    d_ff = w_gate.shape[2]
    nk = S // ATT_BLOCK
    params = functools.partial(pltpu.CompilerParams, vmem_limit_bytes=VMEM_LIMIT_BYTES)

    pos = np.arange(S, dtype=np.float64)
    freqs = 1.0 / (ROPE_BASE ** (np.arange(0, RET_HD, 2, dtype=np.float64) / RET_HD))
    ang = pos[:, None] * freqs[None, :]
    cos_t = jnp.asarray(np.tile(np.cos(ang), (1, LANES // (RET_HD // 2))), F32)
    sin_half = np.sin(ang)
    sin_t = jnp.asarray(np.tile(np.concatenate([-sin_half, sin_half], axis=-1), (1, HEADS_PER_TILE)), F32)
    grp = np.arange(LANES) // RET_HD
    same_group = (grp[:, None] == grp[None, :]).astype(np.float32)
    bdmask = jnp.asarray(same_group, F32)

    log_g = np.log(1.0 - 2.0 ** (-5.0 - np.arange(n_ret_heads, dtype=np.float64)))
    idx = np.arange(RET_CHUNK, dtype=np.float64)
    rel = idx[:, None] - idx[None, :]
    decay_in = np.where(rel[None] >= 0, np.exp(log_g[:, None, None] * np.maximum(rel, 0.0)[None]), 0.0)
    dmat = jnp.asarray(decay_in.reshape(n_tiles, HEADS_PER_TILE * RET_CHUNK, RET_CHUNK), F32)
    by_lane = lambda t: jnp.asarray(np.repeat(t, RET_HD, axis=0).T, F32)
    qdec = by_lane(np.exp(log_g[:, None] * (idx[None, :] + 1.0)))
    kdec = by_lane(np.exp(log_g[:, None] * (RET_CHUNK - 1.0 - idx[None, :])))
    cdec = by_lane(np.exp(log_g * RET_CHUNK)[:, None])

    for l in range(depth):
        row = lambda b, i: (b, i, 0)
        col = lambda b, i: (b, 0, i)
        tab = lambda b, i: (i, 0)
        act = jax.ShapeDtypeStruct((B, S, width), BF16)
        act_t = jax.ShapeDtypeStruct((B, width, S), BF16)
        tile_bf16 = pltpu.VMEM((ROW_TILE, width), BF16)
        n_steps = B * (S // ROW_TILE)
        later = [w_out[l], w_gate[l], w_up[l], w_down[l]]
        later = [w.reshape(n_steps, w.shape[0] // n_steps, w.shape[1]) for w in later]
        step = lambda b, i: (b * (S // ROW_TILE) + i, 0, 0)
        cast_specs = [pl.BlockSpec((1,) + w.shape[1:], step) for w in later]
        ret, dqt, dk, dvt, wo_b, wg_b, wu_b, wd_b = pl.pallas_call(
            functools.partial(_inproj_retention_kernel, width=width, kv_block=ATT_BLOCK,
                              n_cast=len(later)),
            grid=(B, S // ROW_TILE),
            in_specs=[
                pl.BlockSpec((1, ROW_TILE, D), row),
                _resident((1, D)),
                _resident((D, in_width)),
                pl.BlockSpec((ROW_TILE, LANES), tab),
                pl.BlockSpec((ROW_TILE, LANES), tab),
                _resident((1, LANES)),
                _resident((1, LANES)),
                _resident(dmat.shape), _resident(qdec.shape), _resident(kdec.shape),
                _resident(cdec.shape), _resident(bdmask.shape), _resident((1, width)),
            ] + cast_specs,
            out_specs=[
                pl.BlockSpec((1, ROW_TILE, width), row),
                pl.BlockSpec((1, width, ROW_TILE), col),
                pl.BlockSpec((1, ROW_TILE, width), row),
                pl.BlockSpec((1, ROW_TILE // ATT_BLOCK, width, ATT_BLOCK), lambda b, i: (b, i, 0, 0)),
            ] + cast_specs,
            out_shape=[act, act_t, act, jax.ShapeDtypeStruct((B, nk, width, ATT_BLOCK), BF16)]
                      + [jax.ShapeDtypeStruct(w.shape, BF16) for w in later],
            scratch_shapes=[pltpu.VMEM((D, in_width), BF16),
                            pltpu.VMEM((n_tiles, LANES, LANES), F32),
                            tile_bf16, tile_bf16, tile_bf16, tile_bf16,
                            pltpu.VMEM((ROW_TILE, width), F32)],
            compiler_params=params(dimension_semantics=("arbitrary", "arbitrary")),
            name="inproj_retention",
        )(x, norm1_g[l][None, :], w_in[l], cos_t, sin_t,
          jnp.tile(diff_q_norm_g[l], LANES // DIFF_HD)[None, :],
          jnp.tile(diff_k_norm_g[l], LANES // DIFF_HD)[None, :],
          dmat, qdec, kdec, cdec, bdmask, ret_norm_g[l].reshape(1, width), *later)
        wo_b, wg_b, wu_b, wd_b = (b.reshape(w.shape) for b, w in
                                  zip((wo_b, wg_b, wu_b, wd_b), (w_out[l], w_gate[l], w_up[l], w_down[l])))

        lam_spec = _resident((1, DIFF_HD))
        dif = pl.pallas_call(
            functools.partial(_diffattn_kernel, qblock=ATT_QBLOCK, block=ATT_BLOCK, strip=ATT_STRIP,
                              lead=ATT_LEAD),
            grid=(B, n_diff_heads, S // ATT_QBLOCK),
            in_specs=[
                pl.BlockSpec((1, DIFF_VD, ATT_QBLOCK), lambda b, h, i: (b, h, i)),
                pl.BlockSpec((1, S, DIFF_VD), lambda b, h, i: (b, 0, h)),
                pl.BlockSpec((1, nk, DIFF_VD, ATT_BLOCK), lambda b, h, i: (b, 0, h, 0)),
                lam_spec, lam_spec, lam_spec, lam_spec,
                _resident((1, DIFF_VD)),
            ],
            out_specs=pl.BlockSpec((1, ATT_QBLOCK, DIFF_VD), lambda b, h, i: (b, i, h)),
            out_shape=act,
            scratch_shapes=[pltpu.VMEM((ATT_RING, ATT_BLOCK, ATT_STRIP), F32),
                            pltpu.VMEM((ATT_RING, 1, ATT_STRIP), F32),
                            pltpu.VMEM((ATT_STRIP, ATT_STRIP), F32),
                            pltpu.VMEM((2, 1, ATT_QBLOCK), F32),
                            pltpu.VMEM((2, 1, ATT_QBLOCK), F32),
                            pltpu.VMEM((2, DIFF_VD, ATT_QBLOCK), F32)],
            compiler_params=params(dimension_semantics=("parallel", "parallel", "arbitrary")),
            name="diffattn",
        )(dqt, dk, dvt, lambda_q1[l][None, :], lambda_k1[l][None, :], lambda_q2[l][None, :],
          lambda_k2[l][None, :], diff_subln_g[l][None, :])

        tile = pl.BlockSpec((1, ROW_TILE, D), row)
        half = pl.BlockSpec((1, ROW_TILE, width), row)
        x = pl.pallas_call(
            functools.partial(_out_ffn_kernel, ret_width=width, ff_chunks=_ff_chunks(d_ff)),
            grid=(B, S // ROW_TILE),
            in_specs=[tile, half, half,
                      _resident((2 * width, D)), _resident((1, D)),
                      _resident((D, d_ff)), _resident((D, d_ff)), _resident((d_ff, D))],
            out_specs=tile,
            out_shape=jax.ShapeDtypeStruct((B, S, D), F32),
            compiler_params=params(dimension_semantics=("parallel", "parallel")),
            name="out_ffn",
        )(x, ret, dif, wo_b, norm2_g[l][None, :], wg_b, wu_b, wd_b)
    return x
```

```python
import functools
import math

import jax
import jax.numpy as jnp
import numpy as np
from jax import lax
from jax.experimental import pallas as pl
from jax.experimental.pallas import tpu as pltpu

F32 = jnp.float32
BF16 = jnp.bfloat16
F8 = jnp.float8_e4m3fn

LANES = 128
VMEM_LIMIT_BYTES = 56 * 1024 * 1024

RET_HD = 64
DIFF_HD = 64
DIFF_VD = 2 * DIFF_HD
QK_TERMS = 4
DIFF_QK = QK_TERMS * DIFF_HD
ONES_ROWS = 16
HEADS_PER_TILE = LANES // RET_HD
RET_CHUNK = 256
ROPE_BASE = 10000.0
NORM_EPS = 1e-6
LAMBDA_INIT = 0.8 - 0.6 * math.exp(-0.3 * 0)
MASK_VALUE = -1e30
LOG2E = 1.0 / math.log(2.0)

ROW_TILE = 512
ATT_QBLOCK = 4096
ATT_BLOCK = 512
ATT_STRIP = 256
ATT_RING = 16
ATT_LEAD = 4


def _dot(a, b):
    return jnp.dot(a, b, preferred_element_type=F32)


def _dot_nt(a, b):
    return lax.dot_general(a, b, (((1,), (1,)), ((), ())), preferred_element_type=F32)


def _dot_tn(a, b):
    return lax.dot_general(a, b, (((0,), (0,)), ((), ())), preferred_element_type=F32)


def _half_sums(sq):
    first = lax.broadcasted_iota(jnp.int32, sq.shape, 1) < (LANES // 2)
    a = jnp.sum(jnp.where(first, sq, 0.0), axis=-1, keepdims=True)
    b = jnp.sum(jnp.where(first, 0.0, sq), axis=-1, keepdims=True)
    return jnp.where(first, a, b)


def _inproj_retention_kernel(x_ref, g1_ref, w32_ref, cos_ref, sin_ref, gq_ref, gk_ref,
                             dmat_ref, qdec_ref, kdec_ref, cdec_ref, bdmask_ref, gain_ref,
                             *rest, width, kv_block, n_cast):
    cast_in, rest = rest[:n_cast], rest[n_cast:]
    (ret_ref, dqt_ref, dk_ref, dvt_ref), rest = rest[:4], rest[4:]
    cast_out, rest = rest[:n_cast], rest[n_cast:]
    w_ref, state_ref, rq_ref, rk_ref, rv_ref, rg_ref, acc_ref = rest

    @pl.when((pl.program_id(0) == 0) & (pl.program_id(1) == 0))
    def _():
        w_ref[...] = w32_ref[...].astype(BF16)

    @pl.when(pl.program_id(1) == 0)
    def _():
        state_ref[...] = jnp.zeros_like(state_ref)

    for src_ref, dst_ref in zip(cast_in, cast_out):
        dst_ref[...] = src_ref[...].astype(BF16)

    x = x_ref[0]
    tm = x.shape[0]
    ms = jnp.mean(x * x, axis=-1, keepdims=True)
    h = (x * lax.rsqrt(ms + NORM_EPS) * g1_ref[...]).astype(BF16)

    def proj(j):
        return _dot(h, w_ref[:, j * width:(j + 1) * width])

    n_tiles = width // LANES
    tiles = [slice(t * LANES, (t + 1) * LANES) for t in range(n_tiles)]
    cos = cos_ref[...]
    sin = sin_ref[...]
    lane = lax.broadcasted_iota(jnp.int32, (tm, LANES), 1)
    first_half = (lane % RET_HD) < (RET_HD // 2)

    def rotary(t):
        partner = jnp.where(first_half, pltpu.roll(t, LANES - RET_HD // 2, 1),
                            pltpu.roll(t, RET_HD // 2, 1))
        return t * cos + partner * sin

    rq = proj(0)
    for sl in tiles:
        rq_ref[:, sl] = rotary(rq[:, sl]).astype(BF16)
    rk = proj(1)
    for sl in tiles:
        rk_ref[:, sl] = (rotary(rk[:, sl]) * (RET_HD ** -0.5)).astype(BF16)
    rv_ref[...] = proj(2).astype(BF16)
    rg_ref[...] = proj(3).astype(BF16)

    c = RET_CHUNK
    n_chunks = tm // c
    head_a = lax.broadcasted_iota(jnp.int32, (c, LANES), 1) < RET_HD
    bdmask = bdmask_ref[...]

    def retention(p):
        sl = tiles[p]
        kvs = []
        for ci in range(n_chunks):
            rows = slice(ci * c, (ci + 1) * c)
            q2 = rq_ref[rows, sl]
            k2 = rk_ref[rows, sl]
            v2 = rv_ref[rows, sl]
            zero = jnp.zeros_like(q2)
            qs = jnp.concatenate([jnp.where(head_a, q2, zero), jnp.where(head_a, zero, q2)], axis=0)
            scores = _dot_nt(qs, k2)
            inner = (scores * dmat_ref[p]).astype(BF16)
            intra2 = _dot(inner, v2)
            acc_ref[rows, sl] = jnp.where(head_a, intra2[:c], intra2[c:])
            kd = (k2.astype(F32) * kdec_ref[:, sl]).astype(BF16)
            kvs.append(_dot_tn(kd, v2) * bdmask)
        state = state_ref[p]
        for ci in range(n_chunks):
            rows = slice(ci * c, (ci + 1) * c)
            qd = (rq_ref[rows, sl].astype(F32) * qdec_ref[:, sl]).astype(BF16)
            acc_ref[rows, sl] += _dot(qd, state.astype(BF16))
            state = state * cdec_ref[:, sl] + kvs[ci]
        state_ref[p] = state

        o = acc_ref[:, sl]
        ssq = _half_sums(o * o)
        y = o * lax.rsqrt(ssq * (1.0 / RET_HD) + NORM_EPS) * gain_ref[:, sl]
        g = rg_ref[:, sl].astype(F32)
        ret_ref[0, :, sl] = (y * (g / (1.0 + jnp.exp(-g)))).astype(BF16)

    def qk_norm(t, g):
        ssq = _half_sums(t * t)
        return t * lax.rsqrt(ssq * (1.0 / DIFF_HD) + NORM_EPS) * g

    def split8(t):
        hi = t.astype(F8)
        return hi, (t - hi.astype(F32)).astype(F8)

    def diff_q():
        dq = proj(4)
        for hd, sl in enumerate(tiles):
            qt = qk_norm(dq[:, sl], gq_ref[...]).T
            hi, lo = split8(qt)
            for m in range(LANES // DIFF_HD):
                rows = slice(m * DIFF_HD, (m + 1) * DIFF_HD)
                base = (hd * (LANES // DIFF_HD) + m) * DIFF_QK
                dqt_ref[0, base:base + DIFF_QK, :] = jnp.concatenate(
                    [hi[rows], lo[rows], hi[rows], lo[rows]], axis=0)

    def diff_k():
        dk = proj(5)
        first = lax.broadcasted_iota(jnp.int32, (tm, LANES), 1) < DIFF_HD
        for hd, sl in enumerate(tiles):
            kn = qk_norm(dk[:, sl], gk_ref[...])
            swapped = pltpu.roll(kn, DIFF_HD, 1)
            for m, twice in enumerate((jnp.where(first, kn, swapped), jnp.where(first, swapped, kn))):
                hi, lo = split8(twice)
                base = (hd * (LANES // DIFF_HD) + m) * DIFF_QK
                dk_ref[0, :, base:base + DIFF_QK] = jnp.concatenate([hi, lo], axis=1)

    def diff_v():
        dv = proj(6)
        for sl in tiles:
            vt = dv[:, sl].T.astype(BF16)
            for j in range(tm // kv_block):
                dvt_ref[0, j, sl, :] = vt[:, j * kv_block:(j + 1) * kv_block]

    others = [diff_q, diff_k, diff_v]
    for p in range(n_tiles):
        retention(p)
        if p < len(others):
            others[p]()
    for f in others[n_tiles:]:
        f()


def _diffattn_kernel(qt_ref, k_ref, vt_ref, gamma_ref, lq1_ref, lk1_ref, lq2_ref, lk2_ref, gain_ref,
                     o_ref, s_ref, smax_ref, tri_ref, m_ref, l_ref, acc_ref, *, qblock, block, strip, lead):
    qi = pl.program_id(2)
    ring = s_ref.shape[0]
    n_strips = qblock // strip
    chains = [(mi, t) for mi in range(2) for t in reversed(range(n_strips))]
    n_chains = len(chains)
    per_q = qblock // block
    gamma = gamma_ref[...]

    kpos = lax.broadcasted_iota(jnp.int32, (strip, strip), 0)
    qpos = lax.broadcasted_iota(jnp.int32, (strip, strip), 1)
    tri_ref[...] = jnp.where(kpos <= qpos, 0.0, MASK_VALUE)

    def scores(kj, c, slot, nkeys=block, tri_rows=None):
        mi, t = chains[c]
        start = pl.multiple_of(kj * block, block)
        maps = slice(mi * DIFF_QK, (mi + 1) * DIFF_QK)
        k = k_ref[0, pl.ds(start, nkeys), maps]
        q = qt_ref[0, maps, t * strip:(t + 1) * strip]
        sc = _dot(k, q) * gamma
        if tri_rows is not None:
            own = sc[tri_rows:] + tri_ref[...]
            sc = own if tri_rows == 0 else jnp.concatenate([sc[:tri_rows], own], axis=0)
        s_ref[slot, :nkeys, :] = sc
        smax_ref[slot] = jnp.max(sc, axis=0, keepdims=True)

    def softmax_pv(kj, c, slot, nkeys=block):
        mi, t = chains[c]
        cols = slice(t * strip, (t + 1) * strip)
        s = s_ref[slot, :nkeys, :]
        m_old = m_ref[mi, :, cols]
        m_new = jnp.maximum(m_old, smax_ref[slot])
        alpha = jnp.exp2(m_old - m_new)
        p = jnp.exp2(s - m_new)
        vt1 = jnp.concatenate([vt_ref[0, kj, :, :nkeys], jnp.ones((ONES_ROWS, nkeys), BF16)], axis=0)
        pv = _dot(vt1, p.astype(BF16))
        l_ref[mi, :, cols] = alpha * l_ref[mi, :, cols] + pv[DIFF_VD:DIFF_VD + 1]
        acc_ref[mi, :, cols] = alpha * acc_ref[mi, :, cols] + pv[:DIFF_VD]
        m_ref[mi, :, cols] = m_new

    for c in range(lead):
        scores(0, c, c)
    m_ref[...] = jnp.full_like(m_ref, MASK_VALUE)
    l_ref[...] = jnp.zeros_like(l_ref)
    acc_ref[...] = jnp.zeros_like(acc_ref)

    trip = 2 * n_chains

    def body(t, carry):
        for j in range(trip):
            nxt = j + lead
            scores(2 * t + nxt // n_chains, nxt % n_chains, nxt % ring)
            softmax_pv(2 * t + j // n_chains, j % n_chains, j % ring)
        return carry

    n_full = qi * per_q
    lax.fori_loop(0, n_full // 2, body, 0)

    tail = []
    for d in range(per_q):
        for c, (mi, t) in enumerate(chains):
            nkeys = min(block, (t + 1) * strip - d * block)
            if nkeys > 0:
                own = t * strip - d * block
                tail.append((d, c, nkeys, own if own < nkeys else None))
    lam = (jnp.exp(jnp.sum(lq1_ref[...] * lk1_ref[...], axis=-1, keepdims=True))
           - jnp.exp(jnp.sum(lq2_ref[...] * lk2_ref[...], axis=-1, keepdims=True))
           + LAMBDA_INIT)

    def finish(t):
        cols = slice(t * strip, (t + 1) * strip)
        o = (acc_ref[0, :, cols] / l_ref[0, :, cols]
             - lam * (acc_ref[1, :, cols] / l_ref[1, :, cols]))
        ms = jnp.mean(o * o, axis=0, keepdims=True)
        o = o * lax.rsqrt(ms + NORM_EPS)
        o_ref[0, cols, :] = (o.T * gain_ref[...] * (1.0 - LAMBDA_INIT)).astype(BF16)

    last_use = {chains[c][1]: i for i, (_, c, _, _) in enumerate(tail)}
    for i, (d, c, nkeys, tri_rows) in enumerate(tail):
        if i < lead:
            assert d == 0 and c == i and nkeys == block and tri_rows is None
        else:
            assert tri_rows is None or (tri_rows >= 0 and tri_rows + strip == nkeys)
        if i + lead < len(tail):
            d2, c2, nkeys2, tri2 = tail[i + lead]
            scores(n_full + d2, c2, (i + lead) % ring, nkeys2, tri2)
        softmax_pv(n_full + d, c, i % ring, nkeys)
        if i == last_use[chains[c][1]]:
            finish(chains[c][1])


def _out_ffn_kernel(x_ref, ret_ref, dif_ref, wo_ref, g2_ref, wg_ref, wu_ref, wd_ref, o_ref, *,
                    ret_width, ff_chunks):
    x1 = (x_ref[0] + _dot(ret_ref[0], wo_ref[:ret_width, :])
          + _dot(dif_ref[0], wo_ref[ret_width:, :]))
    ms = jnp.mean(x1 * x1, axis=-1, keepdims=True)
    h = (x1 * lax.rsqrt(ms + NORM_EPS) * g2_ref[...]).astype(BF16)
    o_ref[0] = x1
    ff = None
    for lo, hi in ff_chunks:
        g = _dot(h, wg_ref[:, lo:hi])
        u = _dot(h, wu_ref[:, lo:hi])
        a = (g / (1.0 + jnp.exp(-g)) * u).astype(BF16)
        d = _dot(a, wd_ref[lo:hi, :])
        ff = d if ff is None else ff + d
    o_ref[0] += ff


def _pow2_normalised(g):
    top = jnp.max(jnp.abs(g))
    e = jnp.where(top > 0, jnp.floor(jnp.log2(jnp.where(top > 0, top, 1.0))), 0.0)
    return g * jnp.exp2(-e), e


def _resident(shape):
    return pl.BlockSpec(shape, lambda *_: (0,) * len(shape), pipeline_mode=pl.Buffered(1))


def _ff_chunks(d_ff, mxu_cols=256, target=1536):
    chunks, lo = [], 0
    while lo < d_ff:
        hi = min(d_ff, lo + target)
        if d_ff - hi and (d_ff - hi) < mxu_cols:
            hi = d_ff
        chunks.append((lo, hi))
        lo = hi
    return tuple(chunks)


def kernel(x, norm1_g, w_in, ret_norm_g, diff_q_norm_g, diff_k_norm_g, lambda_q1, lambda_k1,
           lambda_q2, lambda_k2, diff_subln_g, w_out, norm2_g, w_gate, w_up, w_down):
    B, S, D = x.shape
    depth = w_in.shape[0]
    n_ret_heads = ret_norm_g.shape[1]
    ret_width = n_ret_heads * RET_HD
    in_width = w_in.shape[2]
    width = ret_width
    assert in_width == 7 * width and w_out.shape[1] == 2 * width
    assert width % LANES == 0 and S % ROW_TILE == 0 and S % ATT_BLOCK == 0
    assert ROW_TILE % ATT_BLOCK == 0 and ROW_TILE % RET_CHUNK == 0
    assert all(w.shape[1] % (B * (S // ROW_TILE)) == 0 for w in (w_out, w_gate, w_up, w_down))
    assert S % ATT_QBLOCK == 0 and (ATT_QBLOCK // ATT_BLOCK) % 2 == 0 and ATT_QBLOCK % ATT_STRIP == 0
    assert (4 * ATT_QBLOCK // ATT_STRIP) % ATT_RING == 0 and ATT_LEAD < ATT_RING
    assert ATT_LEAD <= 2 * ATT_QBLOCK // ATT_STRIP
    n_diff_heads = width // DIFF_VD
    n_tiles = width // LANES
    d_ff = w_gate.shape[2]
    nk = S // ATT_BLOCK
    params = functools.partial(pltpu.CompilerParams, vmem_limit_bytes=VMEM_LIMIT_BYTES)

    pos = np.arange(S, dtype=np.float64)
    freqs = 1.0 / (ROPE_BASE ** (np.arange(0, RET_HD, 2, dtype=np.float64) / RET_HD))
    ang = pos[:, None] * freqs[None, :]
    cos_t = jnp.asarray(np.tile(np.cos(ang), (1, LANES // (RET_HD // 2))), F32)
    sin_half = np.sin(ang)
    sin_t = jnp.asarray(np.tile(np.concatenate([-sin_half, sin_half], axis=-1), (1, HEADS_PER_TILE)), F32)
    grp = np.arange(LANES) // RET_HD
    same_group = (grp[:, None] == grp[None, :]).astype(np.float32)
    bdmask = jnp.asarray(same_group, F32)

    log_g = np.log(1.0 - 2.0 ** (-5.0 - np.arange(n_ret_heads, dtype=np.float64)))
    idx = np.arange(RET_CHUNK, dtype=np.float64)
    rel = idx[:, None] - idx[None, :]
    decay_in = np.where(rel[None] >= 0, np.exp(log_g[:, None, None] * np.maximum(rel, 0.0)[None]), 0.0)
    dmat = jnp.asarray(decay_in.reshape(n_tiles, HEADS_PER_TILE * RET_CHUNK, RET_CHUNK), F32)
    by_lane = lambda t: jnp.asarray(np.repeat(t, RET_HD, axis=0).T, F32)
    qdec = by_lane(np.exp(log_g[:, None] * (idx[None, :] + 1.0)))
    kdec = by_lane(np.exp(log_g[:, None] * (RET_CHUNK - 1.0 - idx[None, :])))
    cdec = by_lane(np.exp(log_g * RET_CHUNK)[:, None])

    for l in range(depth):
        gq8, eq = _pow2_normalised(diff_q_norm_g[l].astype(F32))
        gk8, ek = _pow2_normalised(diff_k_norm_g[l].astype(F32))
        gamma = (jnp.exp2(eq + ek) * ((DIFF_HD ** -0.5) * LOG2E)).reshape(1, 1)

        row = lambda b, i: (b, i, 0)
        col = lambda b, i: (b, 0, i)
        tab = lambda b, i: (i, 0)
        act = jax.ShapeDtypeStruct((B, S, width), BF16)
        n_maps = 2 * n_diff_heads
        tile_bf16 = pltpu.VMEM((ROW_TILE, width), BF16)
        n_steps = B * (S // ROW_TILE)
        later = [w_out[l], w_gate[l], w_up[l], w_down[l]]
        later = [w.reshape(n_steps, w.shape[0] // n_steps, w.shape[1]) for w in later]
        step = lambda b, i: (b * (S // ROW_TILE) + i, 0, 0)
        cast_specs = [pl.BlockSpec((1,) + w.shape[1:], step) for w in later]
        ret, dqt, dk, dvt, wo_b, wg_b, wu_b, wd_b = pl.pallas_call(
            functools.partial(_inproj_retention_kernel, width=width, kv_block=ATT_BLOCK,
                              n_cast=len(later)),
            grid=(B, S // ROW_TILE),
            in_specs=[
                pl.BlockSpec((1, ROW_TILE, D), row),
                _resident((1, D)),
                _resident((D, in_width)),
                pl.BlockSpec((ROW_TILE, LANES), tab),
                pl.BlockSpec((ROW_TILE, LANES), tab),
                _resident((1, LANES)),
                _resident((1, LANES)),
                _resident(dmat.shape), _resident(qdec.shape), _resident(kdec.shape),
                _resident(cdec.shape), _resident(bdmask.shape), _resident((1, width)),
            ] + cast_specs,
            out_specs=[
                pl.BlockSpec((1, ROW_TILE, width), row),
                pl.BlockSpec((1, n_maps * DIFF_QK, ROW_TILE), col),
                pl.BlockSpec((1, ROW_TILE, n_maps * DIFF_QK), row),
                pl.BlockSpec((1, ROW_TILE // ATT_BLOCK, width, ATT_BLOCK), lambda b, i: (b, i, 0, 0)),
            ] + cast_specs,
            out_shape=[act, jax.ShapeDtypeStruct((B, n_maps * DIFF_QK, S), F8),
                       jax.ShapeDtypeStruct((B, S, n_maps * DIFF_QK), F8),
                       jax.ShapeDtypeStruct((B, nk, width, ATT_BLOCK), BF16)]
                      + [jax.ShapeDtypeStruct(w.shape, BF16) for w in later],
            scratch_shapes=[pltpu.VMEM((D, in_width), BF16),
                            pltpu.VMEM((n_tiles, LANES, LANES), F32),
                            tile_bf16, tile_bf16, tile_bf16, tile_bf16,
                            pltpu.VMEM((ROW_TILE, width), F32)],
            compiler_params=params(dimension_semantics=("arbitrary", "arbitrary")),
            name="inproj_retention",
        )(x, norm1_g[l][None, :], w_in[l], cos_t, sin_t,
          jnp.tile(gq8, LANES // DIFF_HD)[None, :], jnp.tile(gk8, LANES // DIFF_HD)[None, :],
          dmat, qdec, kdec, cdec, bdmask, ret_norm_g[l].reshape(1, width), *later)
        wo_b, wg_b, wu_b, wd_b = (b.reshape(w.shape) for b, w in
                                  zip((wo_b, wg_b, wu_b, wd_b), (w_out[l], w_gate[l], w_up[l], w_down[l])))

        lam_spec = _resident((1, DIFF_HD))
        dif = pl.pallas_call(
            functools.partial(_diffattn_kernel, qblock=ATT_QBLOCK, block=ATT_BLOCK, strip=ATT_STRIP,
                              lead=ATT_LEAD),
            grid=(B, n_diff_heads, S // ATT_QBLOCK),
            in_specs=[
                pl.BlockSpec((1, 2 * DIFF_QK, ATT_QBLOCK), lambda b, h, i: (b, h, i)),
                pl.BlockSpec((1, S, 2 * DIFF_QK), lambda b, h, i: (b, 0, h)),
                pl.BlockSpec((1, nk, DIFF_VD, ATT_BLOCK), lambda b, h, i: (b, 0, h, 0)),
                _resident((1, 1)),
                lam_spec, lam_spec, lam_spec, lam_spec,
                _resident((1, DIFF_VD)),
            ],
            out_specs=pl.BlockSpec((1, ATT_QBLOCK, DIFF_VD), lambda b, h, i: (b, i, h)),
            out_shape=act,
            scratch_shapes=[pltpu.VMEM((ATT_RING, ATT_BLOCK, ATT_STRIP), F32),
                            pltpu.VMEM((ATT_RING, 1, ATT_STRIP), F32),
                            pltpu.VMEM((ATT_STRIP, ATT_STRIP), F32),
                            pltpu.VMEM((2, 1, ATT_QBLOCK), F32),
                            pltpu.VMEM((2, 1, ATT_QBLOCK), F32),
                            pltpu.VMEM((2, DIFF_VD, ATT_QBLOCK), F32)],
            compiler_params=params(dimension_semantics=("parallel", "parallel", "arbitrary")),
            name="diffattn",
        )(dqt, dk, dvt, gamma, lambda_q1[l][None, :], lambda_k1[l][None, :], lambda_q2[l][None, :],
          lambda_k2[l][None, :], diff_subln_g[l][None, :])

        tile = pl.BlockSpec((1, ROW_TILE, D), row)
        half = pl.BlockSpec((1, ROW_TILE, width), row)
        x = pl.pallas_call(
            functools.partial(_out_ffn_kernel, ret_width=width, ff_chunks=_ff_chunks(d_ff)),
            grid=(B, S // ROW_TILE),
            in_specs=[tile, half, half,
                      _resident((2 * width, D)), _resident((1, D)),
                      _resident((D, d_ff)), _resident((D, d_ff)), _resident((d_ff, D))],
            out_specs=tile,
            out_shape=jax.ShapeDtypeStruct((B, S, D), F32),
            compiler_params=params(dimension_semantics=("parallel", "parallel")),
            name="out_ffn",
        )(x, ret, dif, wo_b, norm2_g[l][None, :], wg_b, wu_b, wd_b)
    return x
```

```python
import functools
import math

import jax
import jax.numpy as jnp
import numpy as np
from jax import lax
from jax.experimental import pallas as pl
from jax.experimental.pallas import tpu as pltpu

F32 = jnp.float32
BF16 = jnp.bfloat16

LANES = 128
VMEM_LIMIT_BYTES = 56 * 1024 * 1024

RET_HD = 64
DIFF_HD = 64
DIFF_VD = 2 * DIFF_HD
HEADS_PER_TILE = LANES // RET_HD
RET_CHUNK = 256
ROPE_BASE = 10000.0
NORM_EPS = 1e-6
LAMBDA_INIT = 0.8 - 0.6 * math.exp(-0.3 * 0)
MASK_VALUE = -1e30
LOG2E = 1.0 / math.log(2.0)

ROW_TILE = 512
ATT_QBLOCK = 4096
ATT_BLOCK = 512
ATT_STRIP = 256
ATT_RING = 16
ATT_LEAD = 4


def _dot(a, b):
    return jnp.dot(a, b, preferred_element_type=F32)


def _dot_nt(a, b):
    return lax.dot_general(a, b, (((1,), (1,)), ((), ())), preferred_element_type=F32)


def _dot_tn(a, b):
    return lax.dot_general(a, b, (((0,), (0,)), ((), ())), preferred_element_type=F32)


def _half_sums(sq):
    first = lax.broadcasted_iota(jnp.int32, sq.shape, 1) < (LANES // 2)
    a = jnp.sum(jnp.where(first, sq, 0.0), axis=-1, keepdims=True)
    b = jnp.sum(jnp.where(first, 0.0, sq), axis=-1, keepdims=True)
    return jnp.where(first, a, b)


def _inproj_retention_kernel(x_ref, g1_ref, w32_ref, cos_ref, sin_ref, gq_ref, gk_ref,
                             dmat_ref, qdec_ref, kdec_ref, cdec_ref, bdmask_ref, gain_ref,
                             *rest, width, kv_block, n_cast):
    cast_in, rest = rest[:n_cast], rest[n_cast:]
    (ret_ref, dqt_ref, dk_ref, dvt_ref), rest = rest[:4], rest[4:]
    cast_out, rest = rest[:n_cast], rest[n_cast:]
    w_ref, state_ref, rq_ref, rk_ref, rv_ref, rg_ref, acc_ref = rest

    @pl.when((pl.program_id(0) == 0) & (pl.program_id(1) == 0))
    def _():
        w_ref[...] = w32_ref[...].astype(BF16)

    @pl.when(pl.program_id(1) == 0)
    def _():
        state_ref[...] = jnp.zeros_like(state_ref)

    for src_ref, dst_ref in zip(cast_in, cast_out):
        dst_ref[...] = src_ref[...].astype(BF16)

    x = x_ref[0]
    tm = x.shape[0]
    ms = jnp.mean(x * x, axis=-1, keepdims=True)
    h = (x * lax.rsqrt(ms + NORM_EPS) * g1_ref[...]).astype(BF16)

    def proj(j):
        return _dot(h, w_ref[:, j * width:(j + 1) * width])

    n_tiles = width // LANES
    tiles = [slice(t * LANES, (t + 1) * LANES) for t in range(n_tiles)]
    cos = cos_ref[...]
    sin = sin_ref[...]
    lane = lax.broadcasted_iota(jnp.int32, (tm, LANES), 1)
    first_half = (lane % RET_HD) < (RET_HD // 2)

    def rotary(t):
        partner = jnp.where(first_half, pltpu.roll(t, LANES - RET_HD // 2, 1),
                            pltpu.roll(t, RET_HD // 2, 1))
        return t * cos + partner * sin

    rq = proj(0)
    for sl in tiles:
        rq_ref[:, sl] = rotary(rq[:, sl]).astype(BF16)
    rk = proj(1)
    for sl in tiles:
        rk_ref[:, sl] = (rotary(rk[:, sl]) * (RET_HD ** -0.5)).astype(BF16)
    rv_ref[...] = proj(2).astype(BF16)
    rg_ref[...] = proj(3).astype(BF16)

    c = RET_CHUNK
    n_chunks = tm // c
    head_a = lax.broadcasted_iota(jnp.int32, (c, LANES), 1) < RET_HD
    bdmask = bdmask_ref[...]

    def retention(p):
        sl = tiles[p]
        inners, qds, kvs = [], [], []
        for ci in range(n_chunks):
            rows = slice(ci * c, (ci + 1) * c)
            q2 = rq_ref[rows, sl]
            k2 = rk_ref[rows, sl]
            zero = jnp.zeros_like(q2)
            qs = jnp.concatenate([jnp.where(head_a, q2, zero), jnp.where(head_a, zero, q2)], axis=0)
            scores = _dot_nt(qs, k2)
            inners.append((scores * dmat_ref[p]).astype(BF16))
            qdec2 = jnp.concatenate([qdec_ref[:, sl], qdec_ref[:, sl]], axis=0)
            qds.append((qs.astype(F32) * qdec2).astype(BF16))
            kd = (k2.astype(F32) * kdec_ref[:, sl]).astype(BF16)
            kvs.append(_dot_tn(kd, rv_ref[rows, sl]) * bdmask)
        state = state_ref[p]
        for ci in range(n_chunks):
            rows = slice(ci * c, (ci + 1) * c)
            lhs = jnp.concatenate([inners[ci], qds[ci]], axis=1)
            rhs = jnp.concatenate([rv_ref[rows, sl], state.astype(BF16)], axis=0)
            both = _dot(lhs, rhs)
            acc_ref[rows, sl] = jnp.where(head_a, both[:c], both[c:])
            state = state * cdec_ref[:, sl] + kvs[ci]
        state_ref[p] = state

        o = acc_ref[:, sl]
        ssq = _half_sums(o * o)
        y = o * lax.rsqrt(ssq * (1.0 / RET_HD) + NORM_EPS) * gain_ref[:, sl]
        g = rg_ref[:, sl].astype(F32)
        ret_ref[0, :, sl] = (y * (g / (1.0 + jnp.exp(-g)))).astype(BF16)

    def qk_norm(t, g):
        ssq = _half_sums(t * t)
        return t * lax.rsqrt(ssq * (1.0 / DIFF_HD) + NORM_EPS) * g

    def diff_q():
        dq = proj(4)
        q_scale = (DIFF_HD ** -0.5) * LOG2E
        for sl in tiles:
            qn = qk_norm(dq[:, sl], gq_ref[...]) * q_scale
            dqt_ref[0, sl, :] = qn.T.astype(BF16)

    def diff_k():
        dk = proj(5)
        for sl in tiles:
            dk_ref[0, :, sl] = qk_norm(dk[:, sl], gk_ref[...]).astype(BF16)

    def diff_v():
        dv = proj(6)
        for sl in tiles:
            vt = dv[:, sl].T.astype(BF16)
            for j in range(tm // kv_block):
                dvt_ref[0, j, sl, :] = vt[:, j * kv_block:(j + 1) * kv_block]

    others = [diff_q, diff_k, diff_v]
    for p in range(n_tiles):
        retention(p)
        if p < len(others):
            others[p]()
    for f in others[n_tiles:]:
        f()


def _diffattn_kernel(qt_ref, k_ref, vt_ref, lq1_ref, lk1_ref, lq2_ref, lk2_ref, gain_ref,
                     o_ref, s_ref, smax_ref, tri_ref, m_ref, l_ref, acc_ref, *, qblock, block, strip, lead):
    qi = pl.program_id(2)
    ring = s_ref.shape[0]
    n_strips = qblock // strip
    chains = [(mi, t) for mi in range(2) for t in reversed(range(n_strips))]
    n_chains = len(chains)
    per_q = qblock // block
    qt = qt_ref[0]
    row = lax.broadcasted_iota(jnp.int32, qt.shape, 0)
    zero = jnp.zeros_like(qt)
    q_maps = (jnp.where(row < DIFF_HD, qt, zero), jnp.where(row < DIFF_HD, zero, qt))

    kpos = lax.broadcasted_iota(jnp.int32, (strip, strip), 0)
    qpos = lax.broadcasted_iota(jnp.int32, (strip, strip), 1)
    tri_ref[...] = jnp.where(kpos <= qpos, 0.0, MASK_VALUE)

    def scores(kj, c, slot, nkeys=block, tri_rows=None):
        mi, t = chains[c]
        start = pl.multiple_of(kj * block, block)
        k = k_ref[0, pl.ds(start, nkeys), :]
        sc = _dot(k, q_maps[mi][:, t * strip:(t + 1) * strip])
        if tri_rows is not None:
            own = sc[tri_rows:] + tri_ref[...]
            sc = own if tri_rows == 0 else jnp.concatenate([sc[:tri_rows], own], axis=0)
        s_ref[slot, :nkeys, :] = sc
        smax_ref[slot] = jnp.max(sc, axis=0, keepdims=True)

    def softmax_pv(kj, c, slot, nkeys=block):
        mi, t = chains[c]
        cols = slice(t * strip, (t + 1) * strip)
        s = s_ref[slot, :nkeys, :]
        m_old = m_ref[mi, :, cols]
        m_new = jnp.maximum(m_old, smax_ref[slot])
        alpha = jnp.exp2(m_old - m_new)
        p = jnp.exp2(s - m_new)
        l_ref[mi, :, cols] = alpha * l_ref[mi, :, cols] + jnp.sum(p, axis=0, keepdims=True)
        acc_ref[mi, :, cols] = (alpha * acc_ref[mi, :, cols]
                                + _dot(vt_ref[0, kj, :, :nkeys], p.astype(BF16)))
        m_ref[mi, :, cols] = m_new

    for c in range(lead):
        scores(0, c, c)
    m_ref[...] = jnp.full_like(m_ref, MASK_VALUE)
    l_ref[...] = jnp.zeros_like(l_ref)
    acc_ref[...] = jnp.zeros_like(acc_ref)

    trip = 2 * n_chains

    def body(t, carry):
        for j in range(trip):
            nxt = j + lead
            scores(2 * t + nxt // n_chains, nxt % n_chains, nxt % ring)
            softmax_pv(2 * t + j // n_chains, j % n_chains, j % ring)
        return carry

    n_full = qi * per_q
    lax.fori_loop(0, n_full // 2, body, 0)

    tail = []
    for d in range(per_q):
        for c, (mi, t) in enumerate(chains):
            nkeys = min(block, (t + 1) * strip - d * block)
            if nkeys > 0:
                own = t * strip - d * block
                tail.append((d, c, nkeys, own if own < nkeys else None))
    lam = (jnp.exp(jnp.sum(lq1_ref[...] * lk1_ref[...], axis=-1, keepdims=True))
           - jnp.exp(jnp.sum(lq2_ref[...] * lk2_ref[...], axis=-1, keepdims=True))
           + LAMBDA_INIT)

    def finish(t):
        cols = slice(t * strip, (t + 1) * strip)
        o = (acc_ref[0, :, cols] / l_ref[0, :, cols]
             - lam * (acc_ref[1, :, cols] / l_ref[1, :, cols]))
        ms = jnp.mean(o * o, axis=0, keepdims=True)
        o = o * lax.rsqrt(ms + NORM_EPS)
        o_ref[0, cols, :] = (o.T * gain_ref[...] * (1.0 - LAMBDA_INIT)).astype(BF16)

    last_use = {chains[c][1]: i for i, (_, c, _, _) in enumerate(tail)}
    for i, (d, c, nkeys, tri_rows) in enumerate(tail):
        if i < lead:
            assert d == 0 and c == i and nkeys == block and tri_rows is None
        else:
            assert tri_rows is None or (tri_rows >= 0 and tri_rows + strip == nkeys)
        if i + lead < len(tail):
            d2, c2, nkeys2, tri2 = tail[i + lead]
            scores(n_full + d2, c2, (i + lead) % ring, nkeys2, tri2)
        softmax_pv(n_full + d, c, i % ring, nkeys)
        if i == last_use[chains[c][1]]:
            finish(chains[c][1])


def _out_ffn_kernel(x_ref, ret_ref, dif_ref, wo_ref, g2_ref, wg_ref, wu_ref, wd_ref, o_ref, *,
                    ret_width, ff_chunks):
    x1 = (x_ref[0] + _dot(ret_ref[0], wo_ref[:ret_width, :])
          + _dot(dif_ref[0], wo_ref[ret_width:, :]))
    ms = jnp.mean(x1 * x1, axis=-1, keepdims=True)
    h = (x1 * lax.rsqrt(ms + NORM_EPS) * g2_ref[...]).astype(BF16)
    o_ref[0] = x1
    ff = None
    for lo, hi in ff_chunks:
        g = _dot(h, wg_ref[:, lo:hi])
        u = _dot(h, wu_ref[:, lo:hi])
        a = (g / (1.0 + jnp.exp(-g)) * u).astype(BF16)
        d = _dot(a, wd_ref[lo:hi, :])
        ff = d if ff is None else ff + d
    o_ref[0] += ff


def _resident(shape):
    return pl.BlockSpec(shape, lambda *_: (0,) * len(shape), pipeline_mode=pl.Buffered(1))


def _ff_chunks(d_ff, mxu_cols=256, target=1536):
    chunks, lo = [], 0
    while lo < d_ff:
        hi = min(d_ff, lo + target)
        if d_ff - hi and (d_ff - hi) < mxu_cols:
            hi = d_ff
        chunks.append((lo, hi))
        lo = hi
    return tuple(chunks)


def kernel(x, norm1_g, w_in, ret_norm_g, diff_q_norm_g, diff_k_norm_g, lambda_q1, lambda_k1,
           lambda_q2, lambda_k2, diff_subln_g, w_out, norm2_g, w_gate, w_up, w_down):
    B, S, D = x.shape
    depth = w_in.shape[0]
    n_ret_heads = ret_norm_g.shape[1]
    ret_width = n_ret_heads * RET_HD
    in_width = w_in.shape[2]
    width = ret_width
    assert in_width == 7 * width and w_out.shape[1] == 2 * width
    assert width % LANES == 0 and S % ROW_TILE == 0 and S % ATT_BLOCK == 0
    assert ROW_TILE % ATT_BLOCK == 0 and ROW_TILE % RET_CHUNK == 0
    assert all(w.shape[1] % (B * (S // ROW_TILE)) == 0 for w in (w_out, w_gate, w_up, w_down))
    assert S % ATT_QBLOCK == 0 and (ATT_QBLOCK // ATT_BLOCK) % 2 == 0 and ATT_QBLOCK % ATT_STRIP == 0
    assert (4 * ATT_QBLOCK // ATT_STRIP) % ATT_RING == 0 and ATT_LEAD < ATT_RING
    assert ATT_LEAD <= 2 * ATT_QBLOCK // ATT_STRIP
    n_diff_heads = width // DIFF_VD
    n_tiles = width // LANES
    d_ff = w_gate.shape[2]
    nk = S // ATT_BLOCK
    params = functools.partial(pltpu.CompilerParams, vmem_limit_bytes=VMEM_LIMIT_BYTES)

    pos = np.arange(S, dtype=np.float64)
    freqs = 1.0 / (ROPE_BASE ** (np.arange(0, RET_HD, 2, dtype=np.float64) / RET_HD))
    ang = pos[:, None] * freqs[None, :]
    cos_t = jnp.asarray(np.tile(np.cos(ang), (1, LANES // (RET_HD // 2))), F32)
    sin_half = np.sin(ang)
    sin_t = jnp.asarray(np.tile(np.concatenate([-sin_half, sin_half], axis=-1), (1, HEADS_PER_TILE)), F32)
    grp = np.arange(LANES) // RET_HD
    same_group = (grp[:, None] == grp[None, :]).astype(np.float32)
    bdmask = jnp.asarray(same_group, F32)

    log_g = np.log(1.0 - 2.0 ** (-5.0 - np.arange(n_ret_heads, dtype=np.float64)))
    idx = np.arange(RET_CHUNK, dtype=np.float64)
    rel = idx[:, None] - idx[None, :]
    decay_in = np.where(rel[None] >= 0, np.exp(log_g[:, None, None] * np.maximum(rel, 0.0)[None]), 0.0)
    dmat = jnp.asarray(decay_in.reshape(n_tiles, HEADS_PER_TILE * RET_CHUNK, RET_CHUNK), F32)
    by_lane = lambda t: jnp.asarray(np.repeat(t, RET_HD, axis=0).T, F32)
    qdec = by_lane(np.exp(log_g[:, None] * (idx[None, :] + 1.0)))
    kdec = by_lane(np.exp(log_g[:, None] * (RET_CHUNK - 1.0 - idx[None, :])))
    cdec = by_lane(np.exp(log_g * RET_CHUNK)[:, None])

    for l in range(depth):
        row = lambda b, i: (b, i, 0)
        col = lambda b, i: (b, 0, i)
        tab = lambda b, i: (i, 0)
        act = jax.ShapeDtypeStruct((B, S, width), BF16)
        act_t = jax.ShapeDtypeStruct((B, width, S), BF16)
        tile_bf16 = pltpu.VMEM((ROW_TILE, width), BF16)
        n_steps = B * (S // ROW_TILE)
        later = [w_out[l], w_gate[l], w_up[l], w_down[l]]
        later = [w.reshape(n_steps, w.shape[0] // n_steps, w.shape[1]) for w in later]
        step = lambda b, i: (b * (S // ROW_TILE) + i, 0, 0)
        cast_specs = [pl.BlockSpec((1,) + w.shape[1:], step) for w in later]
        ret, dqt, dk, dvt, wo_b, wg_b, wu_b, wd_b = pl.pallas_call(
            functools.partial(_inproj_retention_kernel, width=width, kv_block=ATT_BLOCK,
                              n_cast=len(later)),
            grid=(B, S // ROW_TILE),
            in_specs=[
                pl.BlockSpec((1, ROW_TILE, D), row),
                _resident((1, D)),
                _resident((D, in_width)),
                pl.BlockSpec((ROW_TILE, LANES), tab),
                pl.BlockSpec((ROW_TILE, LANES), tab),
                _resident((1, LANES)),
                _resident((1, LANES)),
                _resident(dmat.shape), _resident(qdec.shape), _resident(kdec.shape),
                _resident(cdec.shape), _resident(bdmask.shape), _resident((1, width)),
            ] + cast_specs,
            out_specs=[
                pl.BlockSpec((1, ROW_TILE, width), row),
                pl.BlockSpec((1, width, ROW_TILE), col),
                pl.BlockSpec((1, ROW_TILE, width), row),
                pl.BlockSpec((1, ROW_TILE // ATT_BLOCK, width, ATT_BLOCK), lambda b, i: (b, i, 0, 0)),
            ] + cast_specs,
            out_shape=[act, act_t, act, jax.ShapeDtypeStruct((B, nk, width, ATT_BLOCK), BF16)]
                      + [jax.ShapeDtypeStruct(w.shape, BF16) for w in later],
            scratch_shapes=[pltpu.VMEM((D, in_width), BF16),
                            pltpu.VMEM((n_tiles, LANES, LANES), F32),
                            tile_bf16, tile_bf16, tile_bf16, tile_bf16,
                            pltpu.VMEM((ROW_TILE, width), F32)],
            compiler_params=params(dimension_semantics=("arbitrary", "arbitrary")),
            name="inproj_retention",
        )(x, norm1_g[l][None, :], w_in[l], cos_t, sin_t,
          jnp.tile(diff_q_norm_g[l], LANES // DIFF_HD)[None, :],
          jnp.tile(diff_k_norm_g[l], LANES // DIFF_HD)[None, :],
          dmat, qdec, kdec, cdec, bdmask, ret_norm_g[l].reshape(1, width), *later)
        wo_b, wg_b, wu_b, wd_b = (b.reshape(w.shape) for b, w in
                                  zip((wo_b, wg_b, wu_b, wd_b), (w_out[l], w_gate[l], w_up[l], w_down[l])))

        lam_spec = _resident((1, DIFF_HD))
        dif = pl.pallas_call(
            functools.partial(_diffattn_kernel, qblock=ATT_QBLOCK, block=ATT_BLOCK, strip=ATT_STRIP,
                              lead=ATT_LEAD),
            grid=(B, n_diff_heads, S // ATT_QBLOCK),
            in_specs=[
                pl.BlockSpec((1, DIFF_VD, ATT_QBLOCK), lambda b, h, i: (b, h, i)),
                pl.BlockSpec((1, S, DIFF_VD), lambda b, h, i: (b, 0, h)),
                pl.BlockSpec((1, nk, DIFF_VD, ATT_BLOCK), lambda b, h, i: (b, 0, h, 0)),
                lam_spec, lam_spec, lam_spec, lam_spec,
                _resident((1, DIFF_VD)),
            ],
            out_specs=pl.BlockSpec((1, ATT_QBLOCK, DIFF_VD), lambda b, h, i: (b, i, h)),
            out_shape=act,
            scratch_shapes=[pltpu.VMEM((ATT_RING, ATT_BLOCK, ATT_STRIP), F32),
                            pltpu.VMEM((ATT_RING, 1, ATT_STRIP), F32),
                            pltpu.VMEM((ATT_STRIP, ATT_STRIP), F32),
                            pltpu.VMEM((2, 1, ATT_QBLOCK), F32),
                            pltpu.VMEM((2, 1, ATT_QBLOCK), F32),
                            pltpu.VMEM((2, DIFF_VD, ATT_QBLOCK), F32)],
            compiler_params=params(dimension_semantics=("parallel", "parallel", "arbitrary")),
            name="diffattn",
        )(dqt, dk, dvt, lambda_q1[l][None, :], lambda_k1[l][None, :], lambda_q2[l][None, :],
          lambda_k2[l][None, :], diff_subln_g[l][None, :])

        tile = pl.BlockSpec((1, ROW_TILE, D), row)
        half = pl.BlockSpec((1, ROW_TILE, width), row)
        x = pl.pallas_call(
            functools.partial(_out_ffn_kernel, ret_width=width, ff_chunks=_ff_chunks(d_ff)),
            grid=(B, S // ROW_TILE),
            in_specs=[tile, half, half,
                      _resident((2 * width, D)), _resident((1, D)),
                      _resident((D, d_ff)), _resident((D, d_ff)), _resident((d_ff, D))],
            out_specs=tile,
            out_shape=jax.ShapeDtypeStruct((B, S, D), F32),
            compiler_params=params(dimension_semantics=("parallel", "parallel")),
            name="out_ffn",
        )(x, ret, dif, wo_b, norm2_g[l][None, :], wg_b, wu_b, wd_b)
    return x
```

```python
import functools
import math

import jax
import jax.numpy as jnp
import numpy as np
from jax import lax
from jax.experimental import pallas as pl
from jax.experimental.pallas import tpu as pltpu

F32 = jnp.float32
BF16 = jnp.bfloat16

LANES = 128
VMEM_LIMIT_BYTES = 56 * 1024 * 1024

RET_HD = 64
DIFF_HD = 64
DIFF_VD = 2 * DIFF_HD
HEADS_PER_TILE = LANES // RET_HD
RET_CHUNK = 256
ROPE_BASE = 10000.0
NORM_EPS = 1e-6
LAMBDA_INIT = 0.8 - 0.6 * math.exp(-0.3 * 0)
MASK_VALUE = -1e30
LOG2E = 1.0 / math.log(2.0)

ROW_TILE = 512
ATT_QBLOCK = 4096
ATT_BLOCK = 512
ATT_STRIP = 256
ATT_RING = 16
ATT_LEAD = 4


def _dot(a, b):
    return jnp.dot(a, b, preferred_element_type=F32)


def _dot_nt(a, b):
    return lax.dot_general(a, b, (((1,), (1,)), ((), ())), preferred_element_type=F32)


def _dot_tn(a, b):
    return lax.dot_general(a, b, (((0,), (0,)), ((), ())), preferred_element_type=F32)


def _half_sums(sq):
    first = lax.broadcasted_iota(jnp.int32, sq.shape, 1) < (LANES // 2)
    a = jnp.sum(jnp.where(first, sq, 0.0), axis=-1, keepdims=True)
    b = jnp.sum(jnp.where(first, 0.0, sq), axis=-1, keepdims=True)
    return jnp.where(first, a, b)


def _inproj_retention_kernel(x_ref, g1_ref, w32_ref, cos_ref, sin_ref, gq_ref, gk_ref,
                             dmat_ref, qdec_ref, kdec_ref, cdec_ref, bdmask_ref, gain_ref,
                             *rest, width, kv_block, n_cast):
    cast_in, rest = rest[:n_cast], rest[n_cast:]
    (ret_ref, dqt_ref, dk_ref, dvt_ref), rest = rest[:4], rest[4:]
    cast_out, rest = rest[:n_cast], rest[n_cast:]
    w_ref, state_ref, rq_ref, rk_ref, rv_ref, rg_ref, acc_ref = rest

    @pl.when((pl.program_id(0) == 0) & (pl.program_id(1) == 0))
    def _():
        w_ref[...] = w32_ref[...].astype(BF16)

    @pl.when(pl.program_id(1) == 0)
    def _():
        state_ref[...] = jnp.zeros_like(state_ref)

    for src_ref, dst_ref in zip(cast_in, cast_out):
        dst_ref[...] = src_ref[...].astype(BF16)

    x = x_ref[0]
    tm = x.shape[0]
    ms = jnp.mean(x * x, axis=-1, keepdims=True)
    h = (x * lax.rsqrt(ms + NORM_EPS) * g1_ref[...]).astype(BF16)

    def proj(j):
        return _dot(h, w_ref[:, j * width:(j + 1) * width])

    n_tiles = width // LANES
    tiles = [slice(t * LANES, (t + 1) * LANES) for t in range(n_tiles)]
    cos = cos_ref[...]
    sin = sin_ref[...]
    lane = lax.broadcasted_iota(jnp.int32, (tm, LANES), 1)
    first_half = (lane % RET_HD) < (RET_HD // 2)

    def rotary(t):
        partner = jnp.where(first_half, pltpu.roll(t, LANES - RET_HD // 2, 1),
                            pltpu.roll(t, RET_HD // 2, 1))
        return t * cos + partner * sin

    rq = proj(0)
    for sl in tiles:
        rq_ref[:, sl] = rotary(rq[:, sl]).astype(BF16)
    rk = proj(1)
    for sl in tiles:
        rk_ref[:, sl] = (rotary(rk[:, sl]) * (RET_HD ** -0.5)).astype(BF16)
    rv_ref[...] = proj(2).astype(BF16)
    rg_ref[...] = proj(3).astype(BF16)

    c = RET_CHUNK
    n_chunks = tm // c
    head_a = lax.broadcasted_iota(jnp.int32, (c, LANES), 1) < RET_HD
    bdmask = bdmask_ref[...]

    def retention(p):
        sl = tiles[p]
        inners, kvs = [], []
        for ci in range(n_chunks):
            rows = slice(ci * c, (ci + 1) * c)
            q2 = rq_ref[rows, sl]
            k2 = rk_ref[rows, sl]
            zero = jnp.zeros_like(q2)
            qs = jnp.concatenate([jnp.where(head_a, q2, zero), jnp.where(head_a, zero, q2)], axis=0)
            scores = _dot_nt(qs, k2)
            inners.append((scores * dmat_ref[p]).astype(BF16))
            kd = (k2.astype(F32) * kdec_ref[:, sl]).astype(BF16)
            kvs.append(_dot_tn(kd, rv_ref[rows, sl]) * bdmask)
        state = state_ref[p]
        for ci in range(n_chunks):
            rows = slice(ci * c, (ci + 1) * c)
            v2 = rv_ref[rows, sl]
            zero = jnp.zeros_like(v2)
            qd = (rq_ref[rows, sl].astype(F32) * qdec_ref[:, sl]).astype(BF16)
            lhs = jnp.concatenate([inners[ci][:c], inners[ci][c:], qd], axis=1)
            rhs = jnp.concatenate([jnp.where(head_a, v2, zero), jnp.where(head_a, zero, v2),
                                   state.astype(BF16)], axis=0)
            acc_ref[rows, sl] = _dot(lhs, rhs)
            state = state * cdec_ref[:, sl] + kvs[ci]
        state_ref[p] = state

        o = acc_ref[:, sl]
        ssq = _half_sums(o * o)
        y = o * lax.rsqrt(ssq * (1.0 / RET_HD) + NORM_EPS) * gain_ref[:, sl]
        g = rg_ref[:, sl].astype(F32)
        ret_ref[0, :, sl] = (y * (g / (1.0 + jnp.exp(-g)))).astype(BF16)

    def qk_norm(t, g):
        ssq = _half_sums(t * t)
        return t * lax.rsqrt(ssq * (1.0 / DIFF_HD) + NORM_EPS) * g

    def diff_q():
        dq = proj(4)
        q_scale = (DIFF_HD ** -0.5) * LOG2E
        for sl in tiles:
            qn = qk_norm(dq[:, sl], gq_ref[...]) * q_scale
            dqt_ref[0, sl, :] = qn.T.astype(BF16)

    def diff_k():
        dk = proj(5)
        for sl in tiles:
            dk_ref[0, :, sl] = qk_norm(dk[:, sl], gk_ref[...]).astype(BF16)

    def diff_v():
        dv = proj(6)
        for sl in tiles:
            vt = dv[:, sl].T.astype(BF16)
            for j in range(tm // kv_block):
                dvt_ref[0, j, sl, :] = vt[:, j * kv_block:(j + 1) * kv_block]

    others = [diff_q, diff_k, diff_v]
    for p in range(n_tiles):
        retention(p)
        if p < len(others):
            others[p]()
    for f in others[n_tiles:]:
        f()


def _diffattn_kernel(qt_ref, k_ref, vt_ref, lq1_ref, lk1_ref, lq2_ref, lk2_ref, gain_ref,
                     o_ref, s_ref, smax_ref, tri_ref, m_ref, l_ref, acc_ref, *, qblock, block, strip, lead):
    qi = pl.program_id(2)
    ring = s_ref.shape[0]
    n_strips = qblock // strip
    chains = [(mi, t) for mi in range(2) for t in reversed(range(n_strips))]
    n_chains = len(chains)
    per_q = qblock // block
    qt = qt_ref[0]
    row = lax.broadcasted_iota(jnp.int32, qt.shape, 0)
    zero = jnp.zeros_like(qt)
    q_maps = (jnp.where(row < DIFF_HD, qt, zero), jnp.where(row < DIFF_HD, zero, qt))

    kpos = lax.broadcasted_iota(jnp.int32, (strip, strip), 0)
    qpos = lax.broadcasted_iota(jnp.int32, (strip, strip), 1)
    tri_ref[...] = jnp.where(kpos <= qpos, 0.0, MASK_VALUE)

    def scores(kj, c, slot, nkeys=block, tri_rows=None):
        mi, t = chains[c]
        start = pl.multiple_of(kj * block, block)
        k = k_ref[0, pl.ds(start, nkeys), :]
        sc = _dot(k, q_maps[mi][:, t * strip:(t + 1) * strip])
        if tri_rows is not None:
            own = sc[tri_rows:] + tri_ref[...]
            sc = own if tri_rows == 0 else jnp.concatenate([sc[:tri_rows], own], axis=0)
        s_ref[slot, :nkeys, :] = sc
        smax_ref[slot] = jnp.max(sc, axis=0, keepdims=True)

    def softmax_pv(kj, c, slot, nkeys=block):
        mi, t = chains[c]
        cols = slice(t * strip, (t + 1) * strip)
        s = s_ref[slot, :nkeys, :]
        m_old = m_ref[mi, :, cols]
        m_new = jnp.maximum(m_old, smax_ref[slot])
        alpha = jnp.exp2(m_old - m_new)
        p = jnp.exp2(s - m_new)
        l_ref[mi, :, cols] = alpha * l_ref[mi, :, cols] + jnp.sum(p, axis=0, keepdims=True)
        acc_ref[mi, :, cols] = (alpha * acc_ref[mi, :, cols]
                                + _dot(vt_ref[0, kj, :, :nkeys], p.astype(BF16)))
        m_ref[mi, :, cols] = m_new

    for c in range(lead):
        scores(0, c, c)
    m_ref[...] = jnp.full_like(m_ref, MASK_VALUE)
    l_ref[...] = jnp.zeros_like(l_ref)
    acc_ref[...] = jnp.zeros_like(acc_ref)

    trip = 2 * n_chains

    def body(t, carry):
        for j in range(trip):
            nxt = j + lead
            scores(2 * t + nxt // n_chains, nxt % n_chains, nxt % ring)
            softmax_pv(2 * t + j // n_chains, j % n_chains, j % ring)
        return carry

    n_full = qi * per_q
    lax.fori_loop(0, n_full // 2, body, 0)

    tail = []
    for d in range(per_q):
        for c, (mi, t) in enumerate(chains):
            nkeys = min(block, (t + 1) * strip - d * block)
            if nkeys > 0:
                own = t * strip - d * block
                tail.append((d, c, nkeys, own if own < nkeys else None))
    lam = (jnp.exp(jnp.sum(lq1_ref[...] * lk1_ref[...], axis=-1, keepdims=True))
           - jnp.exp(jnp.sum(lq2_ref[...] * lk2_ref[...], axis=-1, keepdims=True))
           + LAMBDA_INIT)

    def finish(t):
        cols = slice(t * strip, (t + 1) * strip)
        o = (acc_ref[0, :, cols] / l_ref[0, :, cols]
             - lam * (acc_ref[1, :, cols] / l_ref[1, :, cols]))
        ms = jnp.mean(o * o, axis=0, keepdims=True)
        o = o * lax.rsqrt(ms + NORM_EPS)
        o_ref[0, cols, :] = (o.T * gain_ref[...] * (1.0 - LAMBDA_INIT)).astype(BF16)

    last_use = {chains[c][1]: i for i, (_, c, _, _) in enumerate(tail)}
    for i, (d, c, nkeys, tri_rows) in enumerate(tail):
        if i < lead:
            assert d == 0 and c == i and nkeys == block and tri_rows is None
        else:
            assert tri_rows is None or (tri_rows >= 0 and tri_rows + strip == nkeys)
        if i + lead < len(tail):
            d2, c2, nkeys2, tri2 = tail[i + lead]
            scores(n_full + d2, c2, (i + lead) % ring, nkeys2, tri2)
        softmax_pv(n_full + d, c, i % ring, nkeys)
        if i == last_use[chains[c][1]]:
            finish(chains[c][1])


def _out_ffn_kernel(x_ref, ret_ref, dif_ref, wo_ref, g2_ref, wg_ref, wu_ref, wd_ref, o_ref, *,
                    ret_width, ff_chunks):
    x1 = (x_ref[0] + _dot(ret_ref[0], wo_ref[:ret_width, :])
          + _dot(dif_ref[0], wo_ref[ret_width:, :]))
    ms = jnp.mean(x1 * x1, axis=-1, keepdims=True)
    h = (x1 * lax.rsqrt(ms + NORM_EPS) * g2_ref[...]).astype(BF16)
    o_ref[0] = x1
    ff = None
    for lo, hi in ff_chunks:
        g = _dot(h, wg_ref[:, lo:hi])
        u = _dot(h, wu_ref[:, lo:hi])
        a = (g / (1.0 + jnp.exp(-g)) * u).astype(BF16)
        d = _dot(a, wd_ref[lo:hi, :])
        ff = d if ff is None else ff + d
    o_ref[0] += ff


def _resident(shape):
    return pl.BlockSpec(shape, lambda *_: (0,) * len(shape), pipeline_mode=pl.Buffered(1))


def _ff_chunks(d_ff, mxu_cols=256, target=1536):
    chunks, lo = [], 0
    while lo < d_ff:
        hi = min(d_ff, lo + target)
        if d_ff - hi and (d_ff - hi) < mxu_cols:
            hi = d_ff
        chunks.append((lo, hi))
        lo = hi
    return tuple(chunks)


def kernel(x, norm1_g, w_in, ret_norm_g, diff_q_norm_g, diff_k_norm_g, lambda_q1, lambda_k1,
           lambda_q2, lambda_k2, diff_subln_g, w_out, norm2_g, w_gate, w_up, w_down):
    B, S, D = x.shape
    depth = w_in.shape[0]
    n_ret_heads = ret_norm_g.shape[1]
    ret_width = n_ret_heads * RET_HD
    in_width = w_in.shape[2]
    width = ret_width
    assert in_width == 7 * width and w_out.shape[1] == 2 * width
    assert width % LANES == 0 and S % ROW_TILE == 0 and S % ATT_BLOCK == 0
    assert ROW_TILE % ATT_BLOCK == 0 and ROW_TILE % RET_CHUNK == 0
    assert all(w.shape[1] % (B * (S // ROW_TILE)) == 0 for w in (w_out, w_gate, w_up, w_down))
    assert S % ATT_QBLOCK == 0 and (ATT_QBLOCK // ATT_BLOCK) % 2 == 0 and ATT_QBLOCK % ATT_STRIP == 0
    assert (4 * ATT_QBLOCK // ATT_STRIP) % ATT_RING == 0 and ATT_LEAD < ATT_RING
    assert ATT_LEAD <= 2 * ATT_QBLOCK // ATT_STRIP
    n_diff_heads = width // DIFF_VD
    n_tiles = width // LANES
    d_ff = w_gate.shape[2]
    nk = S // ATT_BLOCK
    params = functools.partial(pltpu.CompilerParams, vmem_limit_bytes=VMEM_LIMIT_BYTES)

    pos = np.arange(S, dtype=np.float64)
    freqs = 1.0 / (ROPE_BASE ** (np.arange(0, RET_HD, 2, dtype=np.float64) / RET_HD))
    ang = pos[:, None] * freqs[None, :]
    cos_t = jnp.asarray(np.tile(np.cos(ang), (1, LANES // (RET_HD // 2))), F32)
    sin_half = np.sin(ang)
    sin_t = jnp.asarray(np.tile(np.concatenate([-sin_half, sin_half], axis=-1), (1, HEADS_PER_TILE)), F32)
    grp = np.arange(LANES) // RET_HD
    same_group = (grp[:, None] == grp[None, :]).astype(np.float32)
    bdmask = jnp.asarray(same_group, F32)

    log_g = np.log(1.0 - 2.0 ** (-5.0 - np.arange(n_ret_heads, dtype=np.float64)))
    idx = np.arange(RET_CHUNK, dtype=np.float64)
    rel = idx[:, None] - idx[None, :]
    decay_in = np.where(rel[None] >= 0, np.exp(log_g[:, None, None] * np.maximum(rel, 0.0)[None]), 0.0)
    dmat = jnp.asarray(decay_in.reshape(n_tiles, HEADS_PER_TILE * RET_CHUNK, RET_CHUNK), F32)
    by_lane = lambda t: jnp.asarray(np.repeat(t, RET_HD, axis=0).T, F32)
    qdec = by_lane(np.exp(log_g[:, None] * (idx[None, :] + 1.0)))
    kdec = by_lane(np.exp(log_g[:, None] * (RET_CHUNK - 1.0 - idx[None, :])))
    cdec = by_lane(np.exp(log_g * RET_CHUNK)[:, None])

    for l in range(depth):
        row = lambda b, i: (b, i, 0)
        col = lambda b, i: (b, 0, i)
        tab = lambda b, i: (i, 0)
        act = jax.ShapeDtypeStruct((B, S, width), BF16)
        act_t = jax.ShapeDtypeStruct((B, width, S), BF16)
        tile_bf16 = pltpu.VMEM((ROW_TILE, width), BF16)
        n_steps = B * (S // ROW_TILE)
        later = [w_out[l], w_gate[l], w_up[l], w_down[l]]
        later = [w.reshape(n_steps, w.shape[0] // n_steps, w.shape[1]) for w in later]
        step = lambda b, i: (b * (S // ROW_TILE) + i, 0, 0)
        cast_specs = [pl.BlockSpec((1,) + w.shape[1:], step) for w in later]
        ret, dqt, dk, dvt, wo_b, wg_b, wu_b, wd_b = pl.pallas_call(
            functools.partial(_inproj_retention_kernel, width=width, kv_block=ATT_BLOCK,
                              n_cast=len(later)),
            grid=(B, S // ROW_TILE),
            in_specs=[
                pl.BlockSpec((1, ROW_TILE, D), row),
                _resident((1, D)),
                _resident((D, in_width)),
                pl.BlockSpec((ROW_TILE, LANES), tab),
                pl.BlockSpec((ROW_TILE, LANES), tab),
                _resident((1, LANES)),
                _resident((1, LANES)),
                _resident(dmat.shape), _resident(qdec.shape), _resident(kdec.shape),
                _resident(cdec.shape), _resident(bdmask.shape), _resident((1, width)),
            ] + cast_specs,
            out_specs=[
                pl.BlockSpec((1, ROW_TILE, width), row),
                pl.BlockSpec((1, width, ROW_TILE), col),
                pl.BlockSpec((1, ROW_TILE, width), row),
                pl.BlockSpec((1, ROW_TILE // ATT_BLOCK, width, ATT_BLOCK), lambda b, i: (b, i, 0, 0)),
            ] + cast_specs,
            out_shape=[act, act_t, act, jax.ShapeDtypeStruct((B, nk, width, ATT_BLOCK), BF16)]
                      + [jax.ShapeDtypeStruct(w.shape, BF16) for w in later],
            scratch_shapes=[pltpu.VMEM((D, in_width), BF16),
                            pltpu.VMEM((n_tiles, LANES, LANES), F32),
                            tile_bf16, tile_bf16, tile_bf16, tile_bf16,
                            pltpu.VMEM((ROW_TILE, width), F32)],
            compiler_params=params(dimension_semantics=("arbitrary", "arbitrary")),
            name="inproj_retention",
        )(x, norm1_g[l][None, :], w_in[l], cos_t, sin_t,
          jnp.tile(diff_q_norm_g[l], LANES // DIFF_HD)[None, :],
          jnp.tile(diff_k_norm_g[l], LANES // DIFF_HD)[None, :],
          dmat, qdec, kdec, cdec, bdmask, ret_norm_g[l].reshape(1, width), *later)
        wo_b, wg_b, wu_b, wd_b = (b.reshape(w.shape) for b, w in
                                  zip((wo_b, wg_b, wu_b, wd_b), (w_out[l], w_gate[l], w_up[l], w_down[l])))

        lam_spec = _resident((1, DIFF_HD))
        dif = pl.pallas_call(
            functools.partial(_diffattn_kernel, qblock=ATT_QBLOCK, block=ATT_BLOCK, strip=ATT_STRIP,
                              lead=ATT_LEAD),
            grid=(B, n_diff_heads, S // ATT_QBLOCK),
            in_specs=[
                pl.BlockSpec((1, DIFF_VD, ATT_QBLOCK), lambda b, h, i: (b, h, i)),
                pl.BlockSpec((1, S, DIFF_VD), lambda b, h, i: (b, 0, h)),
                pl.BlockSpec((1, nk, DIFF_VD, ATT_BLOCK), lambda b, h, i: (b, 0, h, 0)),
                lam_spec, lam_spec, lam_spec, lam_spec,
                _resident((1, DIFF_VD)),
            ],
            out_specs=pl.BlockSpec((1, ATT_QBLOCK, DIFF_VD), lambda b, h, i: (b, i, h)),
            out_shape=act,
            scratch_shapes=[pltpu.VMEM((ATT_RING, ATT_BLOCK, ATT_STRIP), F32),
                            pltpu.VMEM((ATT_RING, 1, ATT_STRIP), F32),
                            pltpu.VMEM((ATT_STRIP, ATT_STRIP), F32),
                            pltpu.VMEM((2, 1, ATT_QBLOCK), F32),
                            pltpu.VMEM((2, 1, ATT_QBLOCK), F32),
                            pltpu.VMEM((2, DIFF_VD, ATT_QBLOCK), F32)],
            compiler_params=params(dimension_semantics=("parallel", "parallel", "arbitrary")),
            name="diffattn",
        )(dqt, dk, dvt, lambda_q1[l][None, :], lambda_k1[l][None, :], lambda_q2[l][None, :],
          lambda_k2[l][None, :], diff_subln_g[l][None, :])

        tile = pl.BlockSpec((1, ROW_TILE, D), row)
        half = pl.BlockSpec((1, ROW_TILE, width), row)
        x = pl.pallas_call(
            functools.partial(_out_ffn_kernel, ret_width=width, ff_chunks=_ff_chunks(d_ff)),
            grid=(B, S // ROW_TILE),
            in_specs=[tile, half, half,
                      _resident((2 * width, D)), _resident((1, D)),
                      _resident((D, d_ff)), _resident((D, d_ff)), _resident((d_ff, D))],
            out_specs=tile,
            out_shape=jax.ShapeDtypeStruct((B, S, D), F32),
            compiler_params=params(dimension_semantics=("parallel", "parallel")),
            name="out_ffn",
        )(x, ret, dif, wo_b, norm2_g[l][None, :], wg_b, wu_b, wd_b)
    return x
```

```python
import functools
import math

import jax
import jax.numpy as jnp
import numpy as np
from jax import lax
from jax.experimental import pallas as pl
from jax.experimental.pallas import tpu as pltpu

F32 = jnp.float32
BF16 = jnp.bfloat16

LANES = 128
VMEM_LIMIT_BYTES = 56 * 1024 * 1024

RET_HD = 64
DIFF_HD = 64
DIFF_VD = 2 * DIFF_HD
HEADS_PER_TILE = LANES // RET_HD
RET_CHUNK = 128
ROPE_BASE = 10000.0
NORM_EPS = 1e-6
LAMBDA_INIT = 0.8 - 0.6 * math.exp(-0.3 * 0)
MASK_VALUE = -1e30
LOG2E = 1.0 / math.log(2.0)

ROW_TILE = 512
ATT_QBLOCK = 4096
ATT_BLOCK = 512
ATT_STRIP = 256
ATT_RING = 16
ATT_LEAD = 4


def _dot(a, b):
    return jnp.dot(a, b, preferred_element_type=F32)


def _dot_nt(a, b):
    return lax.dot_general(a, b, (((1,), (1,)), ((), ())), preferred_element_type=F32)


def _dot_tn(a, b):
    return lax.dot_general(a, b, (((0,), (0,)), ((), ())), preferred_element_type=F32)


def _half_sums(sq):
    first = lax.broadcasted_iota(jnp.int32, sq.shape, 1) < (LANES // 2)
    a = jnp.sum(jnp.where(first, sq, 0.0), axis=-1, keepdims=True)
    b = jnp.sum(jnp.where(first, 0.0, sq), axis=-1, keepdims=True)
    return jnp.where(first, a, b)


def _inproj_retention_kernel(x_ref, g1_ref, w32_ref, cos_ref, sin_ref, gq_ref, gk_ref,
                             dmat_ref, qdec_ref, kdec_ref, cdec_ref, bdmask_ref, gain_ref,
                             *rest, width, kv_block, n_cast):
    cast_in, rest = rest[:n_cast], rest[n_cast:]
    (ret_ref, dqt_ref, dk_ref, dvt_ref), rest = rest[:4], rest[4:]
    cast_out, rest = rest[:n_cast], rest[n_cast:]
    w_ref, state_ref, rq_ref, rk_ref, rv_ref, rg_ref, acc_ref = rest

    @pl.when((pl.program_id(0) == 0) & (pl.program_id(1) == 0))
    def _():
        w_ref[...] = w32_ref[...].astype(BF16)

    @pl.when(pl.program_id(1) == 0)
    def _():
        state_ref[...] = jnp.zeros_like(state_ref)

    for src_ref, dst_ref in zip(cast_in, cast_out):
        dst_ref[...] = src_ref[...].astype(BF16)

    x = x_ref[0]
    tm = x.shape[0]
    ms = jnp.mean(x * x, axis=-1, keepdims=True)
    h = (x * lax.rsqrt(ms + NORM_EPS) * g1_ref[...]).astype(BF16)

    def proj(j):
        return _dot(h, w_ref[:, j * width:(j + 1) * width])

    n_tiles = width // LANES
    tiles = [slice(t * LANES, (t + 1) * LANES) for t in range(n_tiles)]
    cos = cos_ref[...]
    sin = sin_ref[...]
    lane = lax.broadcasted_iota(jnp.int32, (tm, LANES), 1)
    first_half = (lane % RET_HD) < (RET_HD // 2)

    def rotary(t):
        partner = jnp.where(first_half, pltpu.roll(t, LANES - RET_HD // 2, 1),
                            pltpu.roll(t, RET_HD // 2, 1))
        return t * cos + partner * sin

    rq = proj(0)
    for sl in tiles:
        rq_ref[:, sl] = rotary(rq[:, sl]).astype(BF16)
    rk = proj(1)
    for sl in tiles:
        rk_ref[:, sl] = (rotary(rk[:, sl]) * (RET_HD ** -0.5)).astype(BF16)
    rv_ref[...] = proj(2).astype(BF16)
    rg_ref[...] = proj(3).astype(BF16)

    c = RET_CHUNK
    n_chunks = tm // c
    head_a = lax.broadcasted_iota(jnp.int32, (c, LANES), 1) < RET_HD
    bdmask = bdmask_ref[...]

    def retention(p):
        sl = tiles[p]
        inners, kvs = [], []
        for ci in range(n_chunks):
            rows = slice(ci * c, (ci + 1) * c)
            q2 = rq_ref[rows, sl]
            k2 = rk_ref[rows, sl]
            zero = jnp.zeros_like(q2)
            qs = jnp.concatenate([jnp.where(head_a, q2, zero), jnp.where(head_a, zero, q2)], axis=0)
            scores = _dot_nt(qs, k2)
            inners.append((scores * dmat_ref[p]).astype(BF16))
            kd = (k2.astype(F32) * kdec_ref[:, sl]).astype(BF16)
            kvs.append(_dot_tn(kd, rv_ref[rows, sl]) * bdmask)
        state = state_ref[p]
        for ci in range(n_chunks):
            rows = slice(ci * c, (ci + 1) * c)
            v2 = rv_ref[rows, sl]
            zero = jnp.zeros_like(v2)
            qd = (rq_ref[rows, sl].astype(F32) * qdec_ref[:, sl]).astype(BF16)
            lhs = jnp.concatenate([inners[ci][:c], inners[ci][c:], qd], axis=1)
            rhs = jnp.concatenate([jnp.where(head_a, v2, zero), jnp.where(head_a, zero, v2),
                                   state.astype(BF16)], axis=0)
            acc_ref[rows, sl] = _dot(lhs, rhs)
            state = state * cdec_ref[:, sl] + kvs[ci]
        state_ref[p] = state

        o = acc_ref[:, sl]
        ssq = _half_sums(o * o)
        y = o * lax.rsqrt(ssq * (1.0 / RET_HD) + NORM_EPS) * gain_ref[:, sl]
        g = rg_ref[:, sl].astype(F32)
        ret_ref[0, :, sl] = (y * (g / (1.0 + jnp.exp(-g)))).astype(BF16)

    def qk_norm(t, g):
        ssq = _half_sums(t * t)
        return t * lax.rsqrt(ssq * (1.0 / DIFF_HD) + NORM_EPS) * g

    def diff_q():
        dq = proj(4)
        q_scale = (DIFF_HD ** -0.5) * LOG2E
        for sl in tiles:
            qn = qk_norm(dq[:, sl], gq_ref[...]) * q_scale
            dqt_ref[0, sl, :] = qn.T.astype(BF16)

    def diff_k():
        dk = proj(5)
        for sl in tiles:
            dk_ref[0, :, sl] = qk_norm(dk[:, sl], gk_ref[...]).astype(BF16)

    def diff_v():
        dv = proj(6)
        for sl in tiles:
            vt = dv[:, sl].T.astype(BF16)
            for j in range(tm // kv_block):
                dvt_ref[0, j, sl, :] = vt[:, j * kv_block:(j + 1) * kv_block]

    others = [diff_q, diff_k, diff_v]
    for p in range(n_tiles):
        retention(p)
        if p < len(others):
            others[p]()
    for f in others[n_tiles:]:
        f()


def _diffattn_kernel(qt_ref, k_ref, vt_ref, lq1_ref, lk1_ref, lq2_ref, lk2_ref, gain_ref,
                     o_ref, s_ref, smax_ref, tri_ref, m_ref, l_ref, acc_ref, *, qblock, block, strip, lead):
    qi = pl.program_id(2)
    ring = s_ref.shape[0]
    n_strips = qblock // strip
    chains = [(mi, t) for mi in range(2) for t in reversed(range(n_strips))]
    n_chains = len(chains)
    per_q = qblock // block
    qt = qt_ref[0]
    row = lax.broadcasted_iota(jnp.int32, qt.shape, 0)
    zero = jnp.zeros_like(qt)
    q_maps = (jnp.where(row < DIFF_HD, qt, zero), jnp.where(row < DIFF_HD, zero, qt))

    kpos = lax.broadcasted_iota(jnp.int32, (strip, strip), 0)
    qpos = lax.broadcasted_iota(jnp.int32, (strip, strip), 1)
    tri_ref[...] = jnp.where(kpos <= qpos, 0.0, MASK_VALUE)

    def scores(kj, c, slot, nkeys=block, tri_rows=None):
        mi, t = chains[c]
        start = pl.multiple_of(kj * block, block)
        k = k_ref[0, pl.ds(start, nkeys), :]
        sc = _dot(k, q_maps[mi][:, t * strip:(t + 1) * strip])
        if tri_rows is not None:
            own = sc[tri_rows:] + tri_ref[...]
            sc = own if tri_rows == 0 else jnp.concatenate([sc[:tri_rows], own], axis=0)
        s_ref[slot, :nkeys, :] = sc
        smax_ref[slot] = jnp.max(sc, axis=0, keepdims=True)

    def softmax_pv(kj, c, slot, nkeys=block):
        mi, t = chains[c]
        cols = slice(t * strip, (t + 1) * strip)
        s = s_ref[slot, :nkeys, :]
        m_old = m_ref[mi, :, cols]
        m_new = jnp.maximum(m_old, smax_ref[slot])
        alpha = jnp.exp2(m_old - m_new)
        p = jnp.exp2(s - m_new)
        l_ref[mi, :, cols] = alpha * l_ref[mi, :, cols] + jnp.sum(p, axis=0, keepdims=True)
        acc_ref[mi, :, cols] = (alpha * acc_ref[mi, :, cols]
                                + _dot(vt_ref[0, kj, :, :nkeys], p.astype(BF16)))
        m_ref[mi, :, cols] = m_new

    for c in range(lead):
        scores(0, c, c)
    m_ref[...] = jnp.full_like(m_ref, MASK_VALUE)
    l_ref[...] = jnp.zeros_like(l_ref)
    acc_ref[...] = jnp.zeros_like(acc_ref)

    trip = 2 * n_chains

    def body(t, carry):
        for j in range(trip):
            nxt = j + lead
            scores(2 * t + nxt // n_chains, nxt % n_chains, nxt % ring)
            softmax_pv(2 * t + j // n_chains, j % n_chains, j % ring)
        return carry

    n_full = qi * per_q
    lax.fori_loop(0, n_full // 2, body, 0)

    tail = []
    for d in range(per_q):
        for c, (mi, t) in enumerate(chains):
            nkeys = min(block, (t + 1) * strip - d * block)
            if nkeys > 0:
                own = t * strip - d * block
                tail.append((d, c, nkeys, own if own < nkeys else None))
    lam = (jnp.exp(jnp.sum(lq1_ref[...] * lk1_ref[...], axis=-1, keepdims=True))
           - jnp.exp(jnp.sum(lq2_ref[...] * lk2_ref[...], axis=-1, keepdims=True))
           + LAMBDA_INIT)

    def finish(t):
        cols = slice(t * strip, (t + 1) * strip)
        o = (acc_ref[0, :, cols] / l_ref[0, :, cols]
             - lam * (acc_ref[1, :, cols] / l_ref[1, :, cols]))
        ms = jnp.mean(o * o, axis=0, keepdims=True)
        o = o * lax.rsqrt(ms + NORM_EPS)
        o_ref[0, cols, :] = (o.T * gain_ref[...] * (1.0 - LAMBDA_INIT)).astype(BF16)

    last_use = {chains[c][1]: i for i, (_, c, _, _) in enumerate(tail)}
    for i, (d, c, nkeys, tri_rows) in enumerate(tail):
        if i < lead:
            assert d == 0 and c == i and nkeys == block and tri_rows is None
        else:
            assert tri_rows is None or (tri_rows >= 0 and tri_rows + strip == nkeys)
        if i + lead < len(tail):
            d2, c2, nkeys2, tri2 = tail[i + lead]
            scores(n_full + d2, c2, (i + lead) % ring, nkeys2, tri2)
        softmax_pv(n_full + d, c, i % ring, nkeys)
        if i == last_use[chains[c][1]]:
            finish(chains[c][1])


def _out_ffn_kernel(x_ref, ret_ref, dif_ref, wo_ref, g2_ref, wg_ref, wu_ref, wd_ref, o_ref, *,
                    ret_width, ff_chunks):
    x1 = (x_ref[0] + _dot(ret_ref[0], wo_ref[:ret_width, :])
          + _dot(dif_ref[0], wo_ref[ret_width:, :]))
    ms = jnp.mean(x1 * x1, axis=-1, keepdims=True)
    h = (x1 * lax.rsqrt(ms + NORM_EPS) * g2_ref[...]).astype(BF16)
    o_ref[0] = x1
    ff = None
    for lo, hi in ff_chunks:
        g = _dot(h, wg_ref[:, lo:hi])
        u = _dot(h, wu_ref[:, lo:hi])
        a = (g / (1.0 + jnp.exp(-g)) * u).astype(BF16)
        d = _dot(a, wd_ref[lo:hi, :])
        ff = d if ff is None else ff + d
    o_ref[0] += ff


def _resident(shape):
    return pl.BlockSpec(shape, lambda *_: (0,) * len(shape), pipeline_mode=pl.Buffered(1))


def _ff_chunks(d_ff, mxu_cols=256, target=1536):
    chunks, lo = [], 0
    while lo < d_ff:
        hi = min(d_ff, lo + target)
        if d_ff - hi and (d_ff - hi) < mxu_cols:
            hi = d_ff
        chunks.append((lo, hi))
        lo = hi
    return tuple(chunks)


def kernel(x, norm1_g, w_in, ret_norm_g, diff_q_norm_g, diff_k_norm_g, lambda_q1, lambda_k1,
           lambda_q2, lambda_k2, diff_subln_g, w_out, norm2_g, w_gate, w_up, w_down):
    B, S, D = x.shape
    depth = w_in.shape[0]
    n_ret_heads = ret_norm_g.shape[1]
    ret_width = n_ret_heads * RET_HD
    in_width = w_in.shape[2]
    width = ret_width
    assert in_width == 7 * width and w_out.shape[1] == 2 * width
    assert width % LANES == 0 and S % ROW_TILE == 0 and S % ATT_BLOCK == 0
    assert ROW_TILE % ATT_BLOCK == 0 and ROW_TILE % RET_CHUNK == 0
    assert all(w.shape[1] % (B * (S // ROW_TILE)) == 0 for w in (w_out, w_gate, w_up, w_down))
    assert S % ATT_QBLOCK == 0 and (ATT_QBLOCK // ATT_BLOCK) % 2 == 0 and ATT_QBLOCK % ATT_STRIP == 0
    assert (4 * ATT_QBLOCK // ATT_STRIP) % ATT_RING == 0 and ATT_LEAD < ATT_RING
    assert ATT_LEAD <= 2 * ATT_QBLOCK // ATT_STRIP
    n_diff_heads = width // DIFF_VD
    n_tiles = width // LANES
    d_ff = w_gate.shape[2]
    nk = S // ATT_BLOCK
    params = functools.partial(pltpu.CompilerParams, vmem_limit_bytes=VMEM_LIMIT_BYTES)

    pos = np.arange(S, dtype=np.float64)
    freqs = 1.0 / (ROPE_BASE ** (np.arange(0, RET_HD, 2, dtype=np.float64) / RET_HD))
    ang = pos[:, None] * freqs[None, :]
    cos_t = jnp.asarray(np.tile(np.cos(ang), (1, LANES // (RET_HD // 2))), F32)
    sin_half = np.sin(ang)
    sin_t = jnp.asarray(np.tile(np.concatenate([-sin_half, sin_half], axis=-1), (1, HEADS_PER_TILE)), F32)
    grp = np.arange(LANES) // RET_HD
    same_group = (grp[:, None] == grp[None, :]).astype(np.float32)
    bdmask = jnp.asarray(same_group, F32)

    log_g = np.log(1.0 - 2.0 ** (-5.0 - np.arange(n_ret_heads, dtype=np.float64)))
    idx = np.arange(RET_CHUNK, dtype=np.float64)
    rel = idx[:, None] - idx[None, :]
    decay_in = np.where(rel[None] >= 0, np.exp(log_g[:, None, None] * np.maximum(rel, 0.0)[None]), 0.0)
    dmat = jnp.asarray(decay_in.reshape(n_tiles, HEADS_PER_TILE * RET_CHUNK, RET_CHUNK), F32)
    by_lane = lambda t: jnp.asarray(np.repeat(t, RET_HD, axis=0).T, F32)
    qdec = by_lane(np.exp(log_g[:, None] * (idx[None, :] + 1.0)))
    kdec = by_lane(np.exp(log_g[:, None] * (RET_CHUNK - 1.0 - idx[None, :])))
    cdec = by_lane(np.exp(log_g * RET_CHUNK)[:, None])

    for l in range(depth):
        row = lambda b, i: (b, i, 0)
        col = lambda b, i: (b, 0, i)
        tab = lambda b, i: (i, 0)
        act = jax.ShapeDtypeStruct((B, S, width), BF16)
        act_t = jax.ShapeDtypeStruct((B, width, S), BF16)
        tile_bf16 = pltpu.VMEM((ROW_TILE, width), BF16)
        n_steps = B * (S // ROW_TILE)
        later = [w_out[l], w_gate[l], w_up[l], w_down[l]]
        later = [w.reshape(n_steps, w.shape[0] // n_steps, w.shape[1]) for w in later]
        step = lambda b, i: (b * (S // ROW_TILE) + i, 0, 0)
        cast_specs = [pl.BlockSpec((1,) + w.shape[1:], step) for w in later]
        ret, dqt, dk, dvt, wo_b, wg_b, wu_b, wd_b = pl.pallas_call(
            functools.partial(_inproj_retention_kernel, width=width, kv_block=ATT_BLOCK,
                              n_cast=len(later)),
            grid=(B, S // ROW_TILE),
            in_specs=[
                pl.BlockSpec((1, ROW_TILE, D), row),
                _resident((1, D)),
                _resident((D, in_width)),
                pl.BlockSpec((ROW_TILE, LANES), tab),
                pl.BlockSpec((ROW_TILE, LANES), tab),
                _resident((1, LANES)),
                _resident((1, LANES)),
                _resident(dmat.shape), _resident(qdec.shape), _resident(kdec.shape),
                _resident(cdec.shape), _resident(bdmask.shape), _resident((1, width)),
            ] + cast_specs,
            out_specs=[
                pl.BlockSpec((1, ROW_TILE, width), row),
                pl.BlockSpec((1, width, ROW_TILE), col),
                pl.BlockSpec((1, ROW_TILE, width), row),
                pl.BlockSpec((1, ROW_TILE // ATT_BLOCK, width, ATT_BLOCK), lambda b, i: (b, i, 0, 0)),
            ] + cast_specs,
            out_shape=[act, act_t, act, jax.ShapeDtypeStruct((B, nk, width, ATT_BLOCK), BF16)]
                      + [jax.ShapeDtypeStruct(w.shape, BF16) for w in later],
            scratch_shapes=[pltpu.VMEM((D, in_width), BF16),
                            pltpu.VMEM((n_tiles, LANES, LANES), F32),
                            tile_bf16, tile_bf16, tile_bf16, tile_bf16,
                            pltpu.VMEM((ROW_TILE, width), F32)],
            compiler_params=params(dimension_semantics=("arbitrary", "arbitrary")),
            name="inproj_retention",
        )(x, norm1_g[l][None, :], w_in[l], cos_t, sin_t,
          jnp.tile(diff_q_norm_g[l], LANES // DIFF_HD)[None, :],
          jnp.tile(diff_k_norm_g[l], LANES // DIFF_HD)[None, :],
          dmat, qdec, kdec, cdec, bdmask, ret_norm_g[l].reshape(1, width), *later)
        wo_b, wg_b, wu_b, wd_b = (b.reshape(w.shape) for b, w in
                                  zip((wo_b, wg_b, wu_b, wd_b), (w_out[l], w_gate[l], w_up[l], w_down[l])))

        lam_spec = _resident((1, DIFF_HD))
        dif = pl.pallas_call(
            functools.partial(_diffattn_kernel, qblock=ATT_QBLOCK, block=ATT_BLOCK, strip=ATT_STRIP,
                              lead=ATT_LEAD),
            grid=(B, n_diff_heads, S // ATT_QBLOCK),
            in_specs=[
                pl.BlockSpec((1, DIFF_VD, ATT_QBLOCK), lambda b, h, i: (b, h, i)),
                pl.BlockSpec((1, S, DIFF_VD), lambda b, h, i: (b, 0, h)),
                pl.BlockSpec((1, nk, DIFF_VD, ATT_BLOCK), lambda b, h, i: (b, 0, h, 0)),
                lam_spec, lam_spec, lam_spec, lam_spec,
                _resident((1, DIFF_VD)),
            ],
            out_specs=pl.BlockSpec((1, ATT_QBLOCK, DIFF_VD), lambda b, h, i: (b, i, h)),
            out_shape=act,
            scratch_shapes=[pltpu.VMEM((ATT_RING, ATT_BLOCK, ATT_STRIP), F32),
                            pltpu.VMEM((ATT_RING, 1, ATT_STRIP), F32),
                            pltpu.VMEM((ATT_STRIP, ATT_STRIP), F32),
                            pltpu.VMEM((2, 1, ATT_QBLOCK), F32),
                            pltpu.VMEM((2, 1, ATT_QBLOCK), F32),
                            pltpu.VMEM((2, DIFF_VD, ATT_QBLOCK), F32)],
            compiler_params=params(dimension_semantics=("parallel", "parallel", "arbitrary")),
            name="diffattn",
        )(dqt, dk, dvt, lambda_q1[l][None, :], lambda_k1[l][None, :], lambda_q2[l][None, :],
          lambda_k2[l][None, :], diff_subln_g[l][None, :])

        tile = pl.BlockSpec((1, ROW_TILE, D), row)
        half = pl.BlockSpec((1, ROW_TILE, width), row)
        x = pl.pallas_call(
            functools.partial(_out_ffn_kernel, ret_width=width, ff_chunks=_ff_chunks(d_ff)),
            grid=(B, S // ROW_TILE),
            in_specs=[tile, half, half,
                      _resident((2 * width, D)), _resident((1, D)),
                      _resident((D, d_ff)), _resident((D, d_ff)), _resident((d_ff, D))],
            out_specs=tile,
            out_shape=jax.ShapeDtypeStruct((B, S, D), F32),
            compiler_params=params(dimension_semantics=("parallel", "parallel")),
            name="out_ffn",
        )(x, ret, dif, wo_b, norm2_g[l][None, :], wg_b, wu_b, wd_b)
    return x
```

```python
import functools
import math

import jax
import jax.numpy as jnp
import numpy as np
from jax import lax
from jax.experimental import pallas as pl
from jax.experimental.pallas import tpu as pltpu

F32 = jnp.float32
BF16 = jnp.bfloat16

LANES = 128
VMEM_LIMIT_BYTES = 56 * 1024 * 1024

RET_HD = 64
DIFF_HD = 64
DIFF_VD = 2 * DIFF_HD
HEADS_PER_TILE = LANES // RET_HD
RET_CHUNK = 128
ROPE_BASE = 10000.0
NORM_EPS = 1e-6
LAMBDA_INIT = 0.8 - 0.6 * math.exp(-0.3 * 0)
MASK_VALUE = -1e30
LOG2E = 1.0 / math.log(2.0)

ROW_TILE = 512
ATT_QBLOCK = 4096
ATT_BLOCK = 512
ATT_STRIP = 256
ATT_RING = 16
ATT_LEAD = 4


def _dot(a, b):
    return jnp.dot(a, b, preferred_element_type=F32)


def _dot_nt(a, b):
    return lax.dot_general(a, b, (((1,), (1,)), ((), ())), preferred_element_type=F32)


def _dot_tn(a, b):
    return lax.dot_general(a, b, (((0,), (0,)), ((), ())), preferred_element_type=F32)


def _half_sums(sq):
    first = lax.broadcasted_iota(jnp.int32, sq.shape, 1) < (LANES // 2)
    a = jnp.sum(jnp.where(first, sq, 0.0), axis=-1, keepdims=True)
    b = jnp.sum(jnp.where(first, 0.0, sq), axis=-1, keepdims=True)
    return jnp.where(first, a, b)


def _inproj_retention_kernel(x_ref, g1_ref, w32_ref, cos_ref, sin_ref, gq_ref, gk_ref,
                             dmat_ref, qdec_ref, kdec_ref, cdec_ref, bdmask_ref, gain_ref,
                             *rest, width, kv_block, n_cast):
    cast_in, rest = rest[:n_cast], rest[n_cast:]
    (ret_ref, dqt_ref, dk_ref, dvt_ref), rest = rest[:4], rest[4:]
    cast_out, rest = rest[:n_cast], rest[n_cast:]
    w_ref, state_ref, rq_ref, rk_ref, rv_ref, rg_ref = rest

    @pl.when((pl.program_id(0) == 0) & (pl.program_id(1) == 0))
    def _():
        w_ref[...] = w32_ref[...].astype(BF16)

    @pl.when(pl.program_id(1) == 0)
    def _():
        state_ref[...] = jnp.zeros_like(state_ref)

    for src_ref, dst_ref in zip(cast_in, cast_out):
        dst_ref[...] = src_ref[...].astype(BF16)

    x = x_ref[0]
    tm = x.shape[0]
    ms = jnp.mean(x * x, axis=-1, keepdims=True)
    h = (x * lax.rsqrt(ms + NORM_EPS) * g1_ref[...]).astype(BF16)

    def proj(j):
        return _dot(h, w_ref[:, j * width:(j + 1) * width])

    n_tiles = width // LANES
    tiles = [slice(t * LANES, (t + 1) * LANES) for t in range(n_tiles)]
    cos = cos_ref[...]
    sin = sin_ref[...]
    lane = lax.broadcasted_iota(jnp.int32, (tm, LANES), 1)
    first_half = (lane % RET_HD) < (RET_HD // 2)

    def rotary(t):
        partner = jnp.where(first_half, pltpu.roll(t, LANES - RET_HD // 2, 1),
                            pltpu.roll(t, RET_HD // 2, 1))
        return t * cos + partner * sin

    rq = proj(0)
    for sl in tiles:
        rq_ref[:, sl] = rotary(rq[:, sl]).astype(BF16)
    rk = proj(1)
    for sl in tiles:
        rk_ref[:, sl] = (rotary(rk[:, sl]) * (RET_HD ** -0.5)).astype(BF16)
    rv_ref[...] = proj(2).astype(BF16)
    rg_ref[...] = proj(3).astype(BF16)

    c = RET_CHUNK
    n_chunks = tm // c
    head_a = lax.broadcasted_iota(jnp.int32, (c, LANES), 1) < RET_HD
    bdmask = bdmask_ref[...]

    def retention(p):
        sl = tiles[p]
        inners, kvs = [], []
        for ci in range(n_chunks):
            rows = slice(ci * c, (ci + 1) * c)
            q2 = rq_ref[rows, sl]
            k2 = rk_ref[rows, sl]
            zero = jnp.zeros_like(q2)
            qs = jnp.concatenate([jnp.where(head_a, q2, zero), jnp.where(head_a, zero, q2)], axis=0)
            scores = _dot_nt(qs, k2)
            inners.append((scores * dmat_ref[p]).astype(BF16))
            kd = (k2.astype(F32) * kdec_ref[:, sl]).astype(BF16)
            kvs.append(_dot_tn(kd, rv_ref[rows, sl]) * bdmask)
        state = state_ref[p]
        for ci in range(n_chunks):
            rows = slice(ci * c, (ci + 1) * c)
            v2 = rv_ref[rows, sl]
            zero = jnp.zeros_like(v2)
            qd = (rq_ref[rows, sl].astype(F32) * qdec_ref[:, sl]).astype(BF16)
            lhs = jnp.concatenate([inners[ci][:c], inners[ci][c:], qd], axis=1)
            rhs = jnp.concatenate([jnp.where(head_a, v2, zero), jnp.where(head_a, zero, v2),
                                   state.astype(BF16)], axis=0)
            o = _dot(lhs, rhs)
            state = state * cdec_ref[:, sl] + kvs[ci]
            ssq = _half_sums(o * o)
            y = o * lax.rsqrt(ssq * (1.0 / RET_HD) + NORM_EPS) * gain_ref[:, sl]
            g = rg_ref[rows, sl].astype(F32)
            ret_ref[0, rows, sl] = (y * (g / (1.0 + jnp.exp(-g)))).astype(BF16)
        state_ref[p] = state

    def qk_norm(t, g):
        ssq = _half_sums(t * t)
        return t * lax.rsqrt(ssq * (1.0 / DIFF_HD) + NORM_EPS) * g

    def diff_q():
        dq = proj(4)
        q_scale = (DIFF_HD ** -0.5) * LOG2E
        for sl in tiles:
            qn = qk_norm(dq[:, sl], gq_ref[...]) * q_scale
            dqt_ref[0, sl, :] = qn.T.astype(BF16)

    def diff_k():
        dk = proj(5)
        for sl in tiles:
            dk_ref[0, :, sl] = qk_norm(dk[:, sl], gk_ref[...]).astype(BF16)

    def diff_v():
        dv = proj(6)
        for sl in tiles:
            vt = dv[:, sl].T.astype(BF16)
            for j in range(tm // kv_block):
                dvt_ref[0, j, sl, :] = vt[:, j * kv_block:(j + 1) * kv_block]

    others = [diff_q, diff_k, diff_v]
    for p in range(n_tiles):
        retention(p)
        if p < len(others):
            others[p]()
    for f in others[n_tiles:]:
        f()


def _diffattn_kernel(qt_ref, k_ref, vt_ref, lq1_ref, lk1_ref, lq2_ref, lk2_ref, gain_ref,
                     o_ref, s_ref, smax_ref, tri_ref, m_ref, l_ref, acc_ref, *, qblock, block, strip, lead):
    qi = pl.program_id(2)
    ring = s_ref.shape[0]
    n_strips = qblock // strip
    chains = [(mi, t) for mi in range(2) for t in reversed(range(n_strips))]
    n_chains = len(chains)
    per_q = qblock // block
    qt = qt_ref[0]
    row = lax.broadcasted_iota(jnp.int32, qt.shape, 0)
    zero = jnp.zeros_like(qt)
    q_maps = (jnp.where(row < DIFF_HD, qt, zero), jnp.where(row < DIFF_HD, zero, qt))

    kpos = lax.broadcasted_iota(jnp.int32, (strip, strip), 0)
    qpos = lax.broadcasted_iota(jnp.int32, (strip, strip), 1)
    tri_ref[...] = jnp.where(kpos <= qpos, 0.0, MASK_VALUE)

    def scores(kj, c, slot, nkeys=block, tri_rows=None):
        mi, t = chains[c]
        start = pl.multiple_of(kj * block, block)
        k = k_ref[0, pl.ds(start, nkeys), :]
        sc = _dot(k, q_maps[mi][:, t * strip:(t + 1) * strip])
        if tri_rows is not None:
            own = sc[tri_rows:] + tri_ref[...]
            sc = own if tri_rows == 0 else jnp.concatenate([sc[:tri_rows], own], axis=0)
        s_ref[slot, :nkeys, :] = sc
        smax_ref[slot] = jnp.max(sc, axis=0, keepdims=True)

    def softmax_pv(kj, c, slot, nkeys=block):
        mi, t = chains[c]
        cols = slice(t * strip, (t + 1) * strip)
        s = s_ref[slot, :nkeys, :]
        m_old = m_ref[mi, :, cols]
        m_new = jnp.maximum(m_old, smax_ref[slot])
        alpha = jnp.exp2(m_old - m_new)
        p = jnp.exp2(s - m_new)
        l_ref[mi, :, cols] = alpha * l_ref[mi, :, cols] + jnp.sum(p, axis=0, keepdims=True)
        acc_ref[mi, :, cols] = (alpha * acc_ref[mi, :, cols]
                                + _dot(vt_ref[0, kj, :, :nkeys], p.astype(BF16)))
        m_ref[mi, :, cols] = m_new

    for c in range(lead):
        scores(0, c, c)
    m_ref[...] = jnp.full_like(m_ref, MASK_VALUE)
    l_ref[...] = jnp.zeros_like(l_ref)
    acc_ref[...] = jnp.zeros_like(acc_ref)

    trip = 2 * n_chains

    def body(t, carry):
        for j in range(trip):
            nxt = j + lead
            scores(2 * t + nxt // n_chains, nxt % n_chains, nxt % ring)
            softmax_pv(2 * t + j // n_chains, j % n_chains, j % ring)
        return carry

    n_full = qi * per_q
    lax.fori_loop(0, n_full // 2, body, 0)

    tail = []
    for d in range(per_q):
        for c, (mi, t) in enumerate(chains):
            nkeys = min(block, (t + 1) * strip - d * block)
            if nkeys > 0:
                own = t * strip - d * block
                tail.append((d, c, nkeys, own if own < nkeys else None))
    lam = (jnp.exp(jnp.sum(lq1_ref[...] * lk1_ref[...], axis=-1, keepdims=True))
           - jnp.exp(jnp.sum(lq2_ref[...] * lk2_ref[...], axis=-1, keepdims=True))
           + LAMBDA_INIT)

    def finish(t):
        cols = slice(t * strip, (t + 1) * strip)
        o = (acc_ref[0, :, cols] / l_ref[0, :, cols]
             - lam * (acc_ref[1, :, cols] / l_ref[1, :, cols]))
        ms = jnp.mean(o * o, axis=0, keepdims=True)
        o = o * lax.rsqrt(ms + NORM_EPS)
        o_ref[0, cols, :] = (o.T * gain_ref[...] * (1.0 - LAMBDA_INIT)).astype(BF16)

    last_use = {chains[c][1]: i for i, (_, c, _, _) in enumerate(tail)}
    for i, (d, c, nkeys, tri_rows) in enumerate(tail):
        if i < lead:
            assert d == 0 and c == i and nkeys == block and tri_rows is None
        else:
            assert tri_rows is None or (tri_rows >= 0 and tri_rows + strip == nkeys)
        if i + lead < len(tail):
            d2, c2, nkeys2, tri2 = tail[i + lead]
            scores(n_full + d2, c2, (i + lead) % ring, nkeys2, tri2)
        softmax_pv(n_full + d, c, i % ring, nkeys)
        if i == last_use[chains[c][1]]:
            finish(chains[c][1])


def _out_ffn_kernel(x_ref, ret_ref, dif_ref, wo_ref, g2_ref, wg_ref, wu_ref, wd_ref, o_ref, *,
                    ret_width, ff_chunks):
    x1 = (x_ref[0] + _dot(ret_ref[0], wo_ref[:ret_width, :])
          + _dot(dif_ref[0], wo_ref[ret_width:, :]))
    ms = jnp.mean(x1 * x1, axis=-1, keepdims=True)
    h = (x1 * lax.rsqrt(ms + NORM_EPS) * g2_ref[...]).astype(BF16)
    o_ref[0] = x1
    ff = None
    for lo, hi in ff_chunks:
        g = _dot(h, wg_ref[:, lo:hi])
        u = _dot(h, wu_ref[:, lo:hi])
        a = (g / (1.0 + jnp.exp(-g)) * u).astype(BF16)
        d = _dot(a, wd_ref[lo:hi, :])
        ff = d if ff is None else ff + d
    o_ref[0] += ff


def _resident(shape):
    return pl.BlockSpec(shape, lambda *_: (0,) * len(shape), pipeline_mode=pl.Buffered(1))


def _ff_chunks(d_ff, mxu_cols=256, target=1536):
    chunks, lo = [], 0
    while lo < d_ff:
        hi = min(d_ff, lo + target)
        if d_ff - hi and (d_ff - hi) < mxu_cols:
            hi = d_ff
        chunks.append((lo, hi))
        lo = hi
    return tuple(chunks)


def kernel(x, norm1_g, w_in, ret_norm_g, diff_q_norm_g, diff_k_norm_g, lambda_q1, lambda_k1,
           lambda_q2, lambda_k2, diff_subln_g, w_out, norm2_g, w_gate, w_up, w_down):
    B, S, D = x.shape
    depth = w_in.shape[0]
    n_ret_heads = ret_norm_g.shape[1]
    ret_width = n_ret_heads * RET_HD
    in_width = w_in.shape[2]
    width = ret_width
    assert in_width == 7 * width and w_out.shape[1] == 2 * width
    assert width % LANES == 0 and S % ROW_TILE == 0 and S % ATT_BLOCK == 0
    assert ROW_TILE % ATT_BLOCK == 0 and ROW_TILE % RET_CHUNK == 0
    assert all(w.shape[1] % (B * (S // ROW_TILE)) == 0 for w in (w_out, w_gate, w_up, w_down))
    assert S % ATT_QBLOCK == 0 and (ATT_QBLOCK // ATT_BLOCK) % 2 == 0 and ATT_QBLOCK % ATT_STRIP == 0
    assert (4 * ATT_QBLOCK // ATT_STRIP) % ATT_RING == 0 and ATT_LEAD < ATT_RING
    assert ATT_LEAD <= 2 * ATT_QBLOCK // ATT_STRIP
    n_diff_heads = width // DIFF_VD
    n_tiles = width // LANES
    d_ff = w_gate.shape[2]
    nk = S // ATT_BLOCK
    params = functools.partial(pltpu.CompilerParams, vmem_limit_bytes=VMEM_LIMIT_BYTES)

    pos = np.arange(S, dtype=np.float64)
    freqs = 1.0 / (ROPE_BASE ** (np.arange(0, RET_HD, 2, dtype=np.float64) / RET_HD))
    ang = pos[:, None] * freqs[None, :]
    cos_t = jnp.asarray(np.tile(np.cos(ang), (1, LANES // (RET_HD // 2))), F32)
    sin_half = np.sin(ang)
    sin_t = jnp.asarray(np.tile(np.concatenate([-sin_half, sin_half], axis=-1), (1, HEADS_PER_TILE)), F32)
    grp = np.arange(LANES) // RET_HD
    same_group = (grp[:, None] == grp[None, :]).astype(np.float32)
    bdmask = jnp.asarray(same_group, F32)

    log_g = np.log(1.0 - 2.0 ** (-5.0 - np.arange(n_ret_heads, dtype=np.float64)))
    idx = np.arange(RET_CHUNK, dtype=np.float64)
    rel = idx[:, None] - idx[None, :]
    decay_in = np.where(rel[None] >= 0, np.exp(log_g[:, None, None] * np.maximum(rel, 0.0)[None]), 0.0)
    dmat = jnp.asarray(decay_in.reshape(n_tiles, HEADS_PER_TILE * RET_CHUNK, RET_CHUNK), F32)
    by_lane = lambda t: jnp.asarray(np.repeat(t, RET_HD, axis=0).T, F32)
    qdec = by_lane(np.exp(log_g[:, None] * (idx[None, :] + 1.0)))
    kdec = by_lane(np.exp(log_g[:, None] * (RET_CHUNK - 1.0 - idx[None, :])))
    cdec = by_lane(np.exp(log_g * RET_CHUNK)[:, None])

    for l in range(depth):
        row = lambda b, i: (b, i, 0)
        col = lambda b, i: (b, 0, i)
        tab = lambda b, i: (i, 0)
        act = jax.ShapeDtypeStruct((B, S, width), BF16)
        act_t = jax.ShapeDtypeStruct((B, width, S), BF16)
        tile_bf16 = pltpu.VMEM((ROW_TILE, width), BF16)
        n_steps = B * (S // ROW_TILE)
        later = [w_out[l], w_gate[l], w_up[l], w_down[l]]
        later = [w.reshape(n_steps, w.shape[0] // n_steps, w.shape[1]) for w in later]
        step = lambda b, i: (b * (S // ROW_TILE) + i, 0, 0)
        cast_specs = [pl.BlockSpec((1,) + w.shape[1:], step) for w in later]
        ret, dqt, dk, dvt, wo_b, wg_b, wu_b, wd_b = pl.pallas_call(
            functools.partial(_inproj_retention_kernel, width=width, kv_block=ATT_BLOCK,
                              n_cast=len(later)),
            grid=(B, S // ROW_TILE),
            in_specs=[
                pl.BlockSpec((1, ROW_TILE, D), row),
                _resident((1, D)),
                _resident((D, in_width)),
                pl.BlockSpec((ROW_TILE, LANES), tab),
                pl.BlockSpec((ROW_TILE, LANES), tab),
                _resident((1, LANES)),
                _resident((1, LANES)),
                _resident(dmat.shape), _resident(qdec.shape), _resident(kdec.shape),
                _resident(cdec.shape), _resident(bdmask.shape), _resident((1, width)),
            ] + cast_specs,
            out_specs=[
                pl.BlockSpec((1, ROW_TILE, width), row),
                pl.BlockSpec((1, width, ROW_TILE), col),
                pl.BlockSpec((1, ROW_TILE, width), row),
                pl.BlockSpec((1, ROW_TILE // ATT_BLOCK, width, ATT_BLOCK), lambda b, i: (b, i, 0, 0)),
            ] + cast_specs,
            out_shape=[act, act_t, act, jax.ShapeDtypeStruct((B, nk, width, ATT_BLOCK), BF16)]
                      + [jax.ShapeDtypeStruct(w.shape, BF16) for w in later],
            scratch_shapes=[pltpu.VMEM((D, in_width), BF16),
                            pltpu.VMEM((n_tiles, LANES, LANES), F32),
                            tile_bf16, tile_bf16, tile_bf16, tile_bf16],
            compiler_params=params(dimension_semantics=("arbitrary", "arbitrary")),
            name="inproj_retention",
        )(x, norm1_g[l][None, :], w_in[l], cos_t, sin_t,
          jnp.tile(diff_q_norm_g[l], LANES // DIFF_HD)[None, :],
          jnp.tile(diff_k_norm_g[l], LANES // DIFF_HD)[None, :],
          dmat, qdec, kdec, cdec, bdmask, ret_norm_g[l].reshape(1, width), *later)
        wo_b, wg_b, wu_b, wd_b = (b.reshape(w.shape) for b, w in
                                  zip((wo_b, wg_b, wu_b, wd_b), (w_out[l], w_gate[l], w_up[l], w_down[l])))

        lam_spec = _resident((1, DIFF_HD))
        dif = pl.pallas_call(
            functools.partial(_diffattn_kernel, qblock=ATT_QBLOCK, block=ATT_BLOCK, strip=ATT_STRIP,
                              lead=ATT_LEAD),
            grid=(B, n_diff_heads, S // ATT_QBLOCK),
            in_specs=[
                pl.BlockSpec((1, DIFF_VD, ATT_QBLOCK), lambda b, h, i: (b, h, i)),
                pl.BlockSpec((1, S, DIFF_VD), lambda b, h, i: (b, 0, h)),
                pl.BlockSpec((1, nk, DIFF_VD, ATT_BLOCK), lambda b, h, i: (b, 0, h, 0)),
                lam_spec, lam_spec, lam_spec, lam_spec,
                _resident((1, DIFF_VD)),
            ],
            out_specs=pl.BlockSpec((1, ATT_QBLOCK, DIFF_VD), lambda b, h, i: (b, i, h)),
            out_shape=act,
            scratch_shapes=[pltpu.VMEM((ATT_RING, ATT_BLOCK, ATT_STRIP), F32),
                            pltpu.VMEM((ATT_RING, 1, ATT_STRIP), F32),
                            pltpu.VMEM((ATT_STRIP, ATT_STRIP), F32),
                            pltpu.VMEM((2, 1, ATT_QBLOCK), F32),
                            pltpu.VMEM((2, 1, ATT_QBLOCK), F32),
                            pltpu.VMEM((2, DIFF_VD, ATT_QBLOCK), F32)],
            compiler_params=params(dimension_semantics=("parallel", "parallel", "arbitrary")),
            name="diffattn",
        )(dqt, dk, dvt, lambda_q1[l][None, :], lambda_k1[l][None, :], lambda_q2[l][None, :],
          lambda_k2[l][None, :], diff_subln_g[l][None, :])

        tile = pl.BlockSpec((1, ROW_TILE, D), row)
        half = pl.BlockSpec((1, ROW_TILE, width), row)
        x = pl.pallas_call(
            functools.partial(_out_ffn_kernel, ret_width=width, ff_chunks=_ff_chunks(d_ff)),
            grid=(B, S // ROW_TILE),
            in_specs=[tile, half, half,
                      _resident((2 * width, D)), _resident((1, D)),
                      _resident((D, d_ff)), _resident((D, d_ff)), _resident((d_ff, D))],
            out_specs=tile,
            out_shape=jax.ShapeDtypeStruct((B, S, D), F32),
            compiler_params=params(dimension_semantics=("parallel", "parallel")),
            name="out_ffn",
        )(x, ret, dif, wo_b, norm2_g[l][None, :], wg_b, wu_b, wd_b)
    return x
```

```python
import functools
import math

import jax
import jax.numpy as jnp
import numpy as np
from jax import lax
from jax.experimental import pallas as pl
from jax.experimental.pallas import tpu as pltpu

F32 = jnp.float32
BF16 = jnp.bfloat16

LANES = 128
VMEM_LIMIT_BYTES = 56 * 1024 * 1024

RET_HD = 64
DIFF_HD = 64
DIFF_VD = 2 * DIFF_HD
HEADS_PER_TILE = LANES // RET_HD
RET_CHUNK = 128
ROPE_BASE = 10000.0
NORM_EPS = 1e-6
LAMBDA_INIT = 0.8 - 0.6 * math.exp(-0.3 * 0)
MASK_VALUE = -1e30
LOG2E = 1.0 / math.log(2.0)

ROW_TILE = 512
ATT_QBLOCK = 4096
ATT_BLOCK = 512
ATT_STRIP = 256
ATT_RING = 16
ATT_LEAD = 4


def _dot(a, b):
    return jnp.dot(a, b, preferred_element_type=F32)


def _dot_nt(a, b):
    return lax.dot_general(a, b, (((1,), (1,)), ((), ())), preferred_element_type=F32)


def _dot_tn(a, b):
    return lax.dot_general(a, b, (((0,), (0,)), ((), ())), preferred_element_type=F32)


def _half_sums(sq):
    first = lax.broadcasted_iota(jnp.int32, sq.shape, 1) < (LANES // 2)
    a = jnp.sum(jnp.where(first, sq, 0.0), axis=-1, keepdims=True)
    b = jnp.sum(jnp.where(first, 0.0, sq), axis=-1, keepdims=True)
    return jnp.where(first, a, b)


def _inproj_retention_kernel(x_ref, g1_ref, w32_ref, cos_ref, sin_ref, gq_ref, gk_ref,
                             dmat_ref, qdec_ref, kdec_ref, cdec_ref, bdmask_ref, gain_ref,
                             *rest, width, kv_block, n_cast):
    cast_in, rest = rest[:n_cast], rest[n_cast:]
    (ret_ref, dqt_ref, dk_ref, dvt_ref), rest = rest[:4], rest[4:]
    cast_out, rest = rest[:n_cast], rest[n_cast:]
    w_ref, state_ref, rq_ref, rk_ref, rv_ref, rg_ref, acc_ref = rest

    @pl.when((pl.program_id(0) == 0) & (pl.program_id(1) == 0))
    def _():
        w_ref[...] = w32_ref[...].astype(BF16)

    @pl.when(pl.program_id(1) == 0)
    def _():
        state_ref[...] = jnp.zeros_like(state_ref)

    for src_ref, dst_ref in zip(cast_in, cast_out):
        dst_ref[...] = src_ref[...].astype(BF16)

    x = x_ref[0]
    tm = x.shape[0]
    ms = jnp.mean(x * x, axis=-1, keepdims=True)
    h = (x * lax.rsqrt(ms + NORM_EPS) * g1_ref[...]).astype(BF16)

    def proj(j):
        return _dot(h, w_ref[:, j * width:(j + 1) * width])

    n_tiles = width // LANES
    tiles = [slice(t * LANES, (t + 1) * LANES) for t in range(n_tiles)]
    cos = cos_ref[...]
    sin = sin_ref[...]
    lane = lax.broadcasted_iota(jnp.int32, (tm, LANES), 1)
    first_half = (lane % RET_HD) < (RET_HD // 2)

    def rotary(t):
        partner = jnp.where(first_half, pltpu.roll(t, LANES - RET_HD // 2, 1),
                            pltpu.roll(t, RET_HD // 2, 1))
        return t * cos + partner * sin

    rq = proj(0)
    for sl in tiles:
        rq_ref[:, sl] = rotary(rq[:, sl]).astype(BF16)
    rk = proj(1)
    for sl in tiles:
        rk_ref[:, sl] = (rotary(rk[:, sl]) * (RET_HD ** -0.5)).astype(BF16)
    rv_ref[...] = proj(2).astype(BF16)
    rg_ref[...] = proj(3).astype(BF16)

    c = RET_CHUNK
    n_chunks = tm // c
    head_a = lax.broadcasted_iota(jnp.int32, (c, LANES), 1) < RET_HD
    bdmask = bdmask_ref[...]

    def retention(p):
        sl = tiles[p]
        inners, kvs = [], []
        for ci in range(n_chunks):
            rows = slice(ci * c, (ci + 1) * c)
            q2 = rq_ref[rows, sl]
            k2 = rk_ref[rows, sl]
            zero = jnp.zeros_like(q2)
            qs = jnp.concatenate([jnp.where(head_a, q2, zero), jnp.where(head_a, zero, q2)], axis=0)
            scores = _dot_nt(qs, k2)
            inners.append((scores * dmat_ref[p]).astype(BF16))
            kd = (k2.astype(F32) * kdec_ref[:, sl]).astype(BF16)
            kvs.append(_dot_tn(kd, rv_ref[rows, sl]) * bdmask)
        state = state_ref[p]
        for ci in range(n_chunks):
            rows = slice(ci * c, (ci + 1) * c)
            v2 = rv_ref[rows, sl]
            zero = jnp.zeros_like(v2)
            qd = (rq_ref[rows, sl].astype(F32) * qdec_ref[:, sl]).astype(BF16)
            lhs = jnp.concatenate([inners[ci][:c], inners[ci][c:], qd], axis=1)
            rhs = jnp.concatenate([jnp.where(head_a, v2, zero), jnp.where(head_a, zero, v2),
                                   state.astype(BF16)], axis=0)
            acc_ref[rows, sl] = _dot(lhs, rhs)
            state = state * cdec_ref[:, sl] + kvs[ci]
        state_ref[p] = state

        o = acc_ref[:, sl]
        ssq = _half_sums(o * o)
        y = o * lax.rsqrt(ssq * (1.0 / RET_HD) + NORM_EPS) * gain_ref[:, sl]
        g = rg_ref[:, sl].astype(F32)
        ret_ref[0, :, sl] = (y * (g / (1.0 + jnp.exp(-g)))).astype(BF16)

    def qk_norm(t, g):
        ssq = _half_sums(t * t)
        return t * lax.rsqrt(ssq * (1.0 / DIFF_HD) + NORM_EPS) * g

    def diff_q():
        dq = proj(4)
        q_scale = (DIFF_HD ** -0.5) * LOG2E
        for hd, sl in enumerate(tiles):
            qn = qk_norm(dq[:, sl], gq_ref[...]) * q_scale
            qt = qn.T.astype(BF16)
            zero = jnp.zeros((DIFF_HD, tm), BF16)
            dqt_ref[0, 2 * hd * LANES:(2 * hd + 2) * LANES, :] = jnp.concatenate(
                [qt[:DIFF_HD], zero, zero, qt[DIFF_HD:]], axis=0)

    def diff_k():
        dk = proj(5)
        for sl in tiles:
            dk_ref[0, :, sl] = qk_norm(dk[:, sl], gk_ref[...]).astype(BF16)

    def diff_v():
        dv = proj(6)
        for sl in tiles:
            vt = dv[:, sl].T.astype(BF16)
            for j in range(tm // kv_block):
                dvt_ref[0, j, sl, :] = vt[:, j * kv_block:(j + 1) * kv_block]

    others = [diff_q, diff_k, diff_v]
    for p in range(n_tiles):
        retention(p)
        if p < len(others):
            others[p]()
    for f in others[n_tiles:]:
        f()


def _diffattn_kernel(qt_ref, k_ref, vt_ref, lq1_ref, lk1_ref, lq2_ref, lk2_ref, gain_ref,
                     o_ref, s_ref, smax_ref, tri_ref, m_ref, l_ref, acc_ref, *, qblock, block, strip, lead):
    qi = pl.program_id(2)
    ring = s_ref.shape[0]
    n_strips = qblock // strip
    chains = [(mi, t) for mi in range(2) for t in reversed(range(n_strips))]
    n_chains = len(chains)
    per_q = qblock // block
    kpos = lax.broadcasted_iota(jnp.int32, (strip, strip), 0)
    qpos = lax.broadcasted_iota(jnp.int32, (strip, strip), 1)
    tri_ref[...] = jnp.where(kpos <= qpos, 0.0, MASK_VALUE)

    def scores(kj, c, slot, nkeys=block, tri_rows=None):
        mi, t = chains[c]
        start = pl.multiple_of(kj * block, block)
        k = k_ref[0, pl.ds(start, nkeys), :]
        sc = _dot(k, qt_ref[0, mi * LANES:(mi + 1) * LANES, t * strip:(t + 1) * strip])
        if tri_rows is not None:
            own = sc[tri_rows:] + tri_ref[...]
            sc = own if tri_rows == 0 else jnp.concatenate([sc[:tri_rows], own], axis=0)
        s_ref[slot, :nkeys, :] = sc
        smax_ref[slot] = jnp.max(sc, axis=0, keepdims=True)

    def softmax_pv(kj, c, slot, nkeys=block):
        mi, t = chains[c]
        cols = slice(t * strip, (t + 1) * strip)
        s = s_ref[slot, :nkeys, :]
        m_old = m_ref[mi, :, cols]
        m_new = jnp.maximum(m_old, smax_ref[slot])
        alpha = jnp.exp2(m_old - m_new)
        p = jnp.exp2(s - m_new)
        l_ref[mi, :, cols] = alpha * l_ref[mi, :, cols] + jnp.sum(p, axis=0, keepdims=True)
        acc_ref[mi, :, cols] = (alpha * acc_ref[mi, :, cols]
                                + _dot(vt_ref[0, kj, :, :nkeys], p.astype(BF16)))
        m_ref[mi, :, cols] = m_new

    for c in range(lead):
        scores(0, c, c)
    m_ref[...] = jnp.full_like(m_ref, MASK_VALUE)
    l_ref[...] = jnp.zeros_like(l_ref)
    acc_ref[...] = jnp.zeros_like(acc_ref)

    trip = 2 * n_chains

    def body(t, carry):
        for j in range(trip):
            nxt = j + lead
            scores(2 * t + nxt // n_chains, nxt % n_chains, nxt % ring)
            softmax_pv(2 * t + j // n_chains, j % n_chains, j % ring)
        return carry

    n_full = qi * per_q
    lax.fori_loop(0, n_full // 2, body, 0)

    tail = []
    for d in range(per_q):
        for c, (mi, t) in enumerate(chains):
            nkeys = min(block, (t + 1) * strip - d * block)
            if nkeys > 0:
                own = t * strip - d * block
                tail.append((d, c, nkeys, own if own < nkeys else None))
    lam = (jnp.exp(jnp.sum(lq1_ref[...] * lk1_ref[...], axis=-1, keepdims=True))
           - jnp.exp(jnp.sum(lq2_ref[...] * lk2_ref[...], axis=-1, keepdims=True))
           + LAMBDA_INIT)

    def finish(t):
        cols = slice(t * strip, (t + 1) * strip)
        o = (acc_ref[0, :, cols] / l_ref[0, :, cols]
             - lam * (acc_ref[1, :, cols] / l_ref[1, :, cols]))
        ms = jnp.mean(o * o, axis=0, keepdims=True)
        o = o * lax.rsqrt(ms + NORM_EPS)
        o_ref[0, cols, :] = (o.T * gain_ref[...] * (1.0 - LAMBDA_INIT)).astype(BF16)

    last_use = {chains[c][1]: i for i, (_, c, _, _) in enumerate(tail)}
    for i, (d, c, nkeys, tri_rows) in enumerate(tail):
        if i < lead:
            assert d == 0 and c == i and nkeys == block and tri_rows is None
        else:
            assert tri_rows is None or (tri_rows >= 0 and tri_rows + strip == nkeys)
        if i + lead < len(tail):
            d2, c2, nkeys2, tri2 = tail[i + lead]
            scores(n_full + d2, c2, (i + lead) % ring, nkeys2, tri2)
        softmax_pv(n_full + d, c, i % ring, nkeys)
        if i == last_use[chains[c][1]]:
            finish(chains[c][1])


def _out_ffn_kernel(x_ref, ret_ref, dif_ref, wo_ref, g2_ref, wg_ref, wu_ref, wd_ref, o_ref, *,
                    ret_width, ff_chunks):
    x1 = (x_ref[0] + _dot(ret_ref[0], wo_ref[:ret_width, :])
          + _dot(dif_ref[0], wo_ref[ret_width:, :]))
    ms = jnp.mean(x1 * x1, axis=-1, keepdims=True)
    h = (x1 * lax.rsqrt(ms + NORM_EPS) * g2_ref[...]).astype(BF16)
    o_ref[0] = x1
    ff = None
    for lo, hi in ff_chunks:
        g = _dot(h, wg_ref[:, lo:hi])
        u = _dot(h, wu_ref[:, lo:hi])
        a = (g / (1.0 + jnp.exp(-g)) * u).astype(BF16)
        d = _dot(a, wd_ref[lo:hi, :])
        ff = d if ff is None else ff + d
    o_ref[0] += ff


def _resident(shape):
    return pl.BlockSpec(shape, lambda *_: (0,) * len(shape), pipeline_mode=pl.Buffered(1))


def _ff_chunks(d_ff, mxu_cols=256, target=1536):
    chunks, lo = [], 0
    while lo < d_ff:
        hi = min(d_ff, lo + target)
        if d_ff - hi and (d_ff - hi) < mxu_cols:
            hi = d_ff
        chunks.append((lo, hi))
        lo = hi
    return tuple(chunks)


def kernel(x, norm1_g, w_in, ret_norm_g, diff_q_norm_g, diff_k_norm_g, lambda_q1, lambda_k1,
           lambda_q2, lambda_k2, diff_subln_g, w_out, norm2_g, w_gate, w_up, w_down):
    B, S, D = x.shape
    depth = w_in.shape[0]
    n_ret_heads = ret_norm_g.shape[1]
    ret_width = n_ret_heads * RET_HD
    in_width = w_in.shape[2]
    width = ret_width
    assert in_width == 7 * width and w_out.shape[1] == 2 * width
    assert width % LANES == 0 and S % ROW_TILE == 0 and S % ATT_BLOCK == 0
    assert ROW_TILE % ATT_BLOCK == 0 and ROW_TILE % RET_CHUNK == 0
    assert all(w.shape[1] % (B * (S // ROW_TILE)) == 0 for w in (w_out, w_gate, w_up, w_down))
    assert S % ATT_QBLOCK == 0 and (ATT_QBLOCK // ATT_BLOCK) % 2 == 0 and ATT_QBLOCK % ATT_STRIP == 0
    assert (4 * ATT_QBLOCK // ATT_STRIP) % ATT_RING == 0 and ATT_LEAD < ATT_RING
    assert ATT_LEAD <= 2 * ATT_QBLOCK // ATT_STRIP
    n_diff_heads = width // DIFF_VD
    n_tiles = width // LANES
    d_ff = w_gate.shape[2]
    nk = S // ATT_BLOCK
    params = functools.partial(pltpu.CompilerParams, vmem_limit_bytes=VMEM_LIMIT_BYTES)

    pos = np.arange(S, dtype=np.float64)
    freqs = 1.0 / (ROPE_BASE ** (np.arange(0, RET_HD, 2, dtype=np.float64) / RET_HD))
    ang = pos[:, None] * freqs[None, :]
    cos_t = jnp.asarray(np.tile(np.cos(ang), (1, LANES // (RET_HD // 2))), F32)
    sin_half = np.sin(ang)
    sin_t = jnp.asarray(np.tile(np.concatenate([-sin_half, sin_half], axis=-1), (1, HEADS_PER_TILE)), F32)
    grp = np.arange(LANES) // RET_HD
    same_group = (grp[:, None] == grp[None, :]).astype(np.float32)
    bdmask = jnp.asarray(same_group, F32)

    log_g = np.log(1.0 - 2.0 ** (-5.0 - np.arange(n_ret_heads, dtype=np.float64)))
    idx = np.arange(RET_CHUNK, dtype=np.float64)
    rel = idx[:, None] - idx[None, :]
    decay_in = np.where(rel[None] >= 0, np.exp(log_g[:, None, None] * np.maximum(rel, 0.0)[None]), 0.0)
    dmat = jnp.asarray(decay_in.reshape(n_tiles, HEADS_PER_TILE * RET_CHUNK, RET_CHUNK), F32)
    by_lane = lambda t: jnp.asarray(np.repeat(t, RET_HD, axis=0).T, F32)
    qdec = by_lane(np.exp(log_g[:, None] * (idx[None, :] + 1.0)))
    kdec = by_lane(np.exp(log_g[:, None] * (RET_CHUNK - 1.0 - idx[None, :])))
    cdec = by_lane(np.exp(log_g * RET_CHUNK)[:, None])

    for l in range(depth):
        row = lambda b, i: (b, i, 0)
        col = lambda b, i: (b, 0, i)
        tab = lambda b, i: (i, 0)
        act = jax.ShapeDtypeStruct((B, S, width), BF16)
        act_t = jax.ShapeDtypeStruct((B, 2 * width, S), BF16)
        tile_bf16 = pltpu.VMEM((ROW_TILE, width), BF16)
        n_steps = B * (S // ROW_TILE)
        later = [w_out[l], w_gate[l], w_up[l], w_down[l]]
        later = [w.reshape(n_steps, w.shape[0] // n_steps, w.shape[1]) for w in later]
        step = lambda b, i: (b * (S // ROW_TILE) + i, 0, 0)
        cast_specs = [pl.BlockSpec((1,) + w.shape[1:], step) for w in later]
        ret, dqt, dk, dvt, wo_b, wg_b, wu_b, wd_b = pl.pallas_call(
            functools.partial(_inproj_retention_kernel, width=width, kv_block=ATT_BLOCK,
                              n_cast=len(later)),
            grid=(B, S // ROW_TILE),
            in_specs=[
                pl.BlockSpec((1, ROW_TILE, D), row),
                _resident((1, D)),
                _resident((D, in_width)),
                pl.BlockSpec((ROW_TILE, LANES), tab),
                pl.BlockSpec((ROW_TILE, LANES), tab),
                _resident((1, LANES)),
                _resident((1, LANES)),
                _resident(dmat.shape), _resident(qdec.shape), _resident(kdec.shape),
                _resident(cdec.shape), _resident(bdmask.shape), _resident((1, width)),
            ] + cast_specs,
            out_specs=[
                pl.BlockSpec((1, ROW_TILE, width), row),
                pl.BlockSpec((1, 2 * width, ROW_TILE), col),
                pl.BlockSpec((1, ROW_TILE, width), row),
                pl.BlockSpec((1, ROW_TILE // ATT_BLOCK, width, ATT_BLOCK), lambda b, i: (b, i, 0, 0)),
            ] + cast_specs,
            out_shape=[act, act_t, act, jax.ShapeDtypeStruct((B, nk, width, ATT_BLOCK), BF16)]
                      + [jax.ShapeDtypeStruct(w.shape, BF16) for w in later],
            scratch_shapes=[pltpu.VMEM((D, in_width), BF16),
                            pltpu.VMEM((n_tiles, LANES, LANES), F32),
                            tile_bf16, tile_bf16, tile_bf16, tile_bf16,
                            pltpu.VMEM((ROW_TILE, width), F32)],
            compiler_params=params(dimension_semantics=("arbitrary", "arbitrary")),
            name="inproj_retention",
        )(x, norm1_g[l][None, :], w_in[l], cos_t, sin_t,
          jnp.tile(diff_q_norm_g[l], LANES // DIFF_HD)[None, :],
          jnp.tile(diff_k_norm_g[l], LANES // DIFF_HD)[None, :],
          dmat, qdec, kdec, cdec, bdmask, ret_norm_g[l].reshape(1, width), *later)
        wo_b, wg_b, wu_b, wd_b = (b.reshape(w.shape) for b, w in
                                  zip((wo_b, wg_b, wu_b, wd_b), (w_out[l], w_gate[l], w_up[l], w_down[l])))

        lam_spec = _resident((1, DIFF_HD))
        dif = pl.pallas_call(
            functools.partial(_diffattn_kernel, qblock=ATT_QBLOCK, block=ATT_BLOCK, strip=ATT_STRIP,
                              lead=ATT_LEAD),
            grid=(B, n_diff_heads, S // ATT_QBLOCK),
            in_specs=[
                pl.BlockSpec((1, 2 * DIFF_VD, ATT_QBLOCK), lambda b, h, i: (b, h, i)),
                pl.BlockSpec((1, S, DIFF_VD), lambda b, h, i: (b, 0, h)),
                pl.BlockSpec((1, nk, DIFF_VD, ATT_BLOCK), lambda b, h, i: (b, 0, h, 0)),
                lam_spec, lam_spec, lam_spec, lam_spec,
                _resident((1, DIFF_VD)),
            ],
            out_specs=pl.BlockSpec((1, ATT_QBLOCK, DIFF_VD), lambda b, h, i: (b, i, h)),
            out_shape=act,
            scratch_shapes=[pltpu.VMEM((ATT_RING, ATT_BLOCK, ATT_STRIP), F32),
                            pltpu.VMEM((ATT_RING, 1, ATT_STRIP), F32),
                            pltpu.VMEM((ATT_STRIP, ATT_STRIP), F32),
                            pltpu.VMEM((2, 1, ATT_QBLOCK), F32),
                            pltpu.VMEM((2, 1, ATT_QBLOCK), F32),
                            pltpu.VMEM((2, DIFF_VD, ATT_QBLOCK), F32)],
            compiler_params=params(dimension_semantics=("parallel", "parallel", "arbitrary")),
            name="diffattn",
        )(dqt, dk, dvt, lambda_q1[l][None, :], lambda_k1[l][None, :], lambda_q2[l][None, :],
          lambda_k2[l][None, :], diff_subln_g[l][None, :])

        tile = pl.BlockSpec((1, ROW_TILE, D), row)
        half = pl.BlockSpec((1, ROW_TILE, width), row)
        x = pl.pallas_call(
            functools.partial(_out_ffn_kernel, ret_width=width, ff_chunks=_ff_chunks(d_ff)),
            grid=(B, S // ROW_TILE),
            in_specs=[tile, half, half,
                      _resident((2 * width, D)), _resident((1, D)),
                      _resident((D, d_ff)), _resident((D, d_ff)), _resident((d_ff, D))],
            out_specs=tile,
            out_shape=jax.ShapeDtypeStruct((B, S, D), F32),
            compiler_params=params(dimension_semantics=("parallel", "parallel")),
            name="out_ffn",
        )(x, ret, dif, wo_b, norm2_g[l][None, :], wg_b, wu_b, wd_b)
    return x
```

```python
import functools
import math

import jax
import jax.numpy as jnp
import numpy as np
from jax import lax
from jax.experimental import pallas as pl
from jax.experimental.pallas import tpu as pltpu

F32 = jnp.float32
BF16 = jnp.bfloat16

LANES = 128
VMEM_LIMIT_BYTES = 56 * 1024 * 1024

RET_HD = 64
DIFF_HD = 64
DIFF_VD = 2 * DIFF_HD
HEADS_PER_TILE = LANES // RET_HD
RET_CHUNK = 128
ROPE_BASE = 10000.0
NORM_EPS = 1e-6
LAMBDA_INIT = 0.8 - 0.6 * math.exp(-0.3 * 0)
MASK_VALUE = -1e30
LOG2E = 1.0 / math.log(2.0)

ROW_TILE = 512
ATT_QBLOCK = 4096
ATT_BLOCK = 512
ATT_STRIP = 256
ATT_RING = 8
ATT_LEAD = 4


def _dot(a, b):
    return jnp.dot(a, b, preferred_element_type=F32)


def _dot_nt(a, b):
    return lax.dot_general(a, b, (((1,), (1,)), ((), ())), preferred_element_type=F32)


def _dot_tn(a, b):
    return lax.dot_general(a, b, (((0,), (0,)), ((), ())), preferred_element_type=F32)


def _half_sums(sq):
    first = lax.broadcasted_iota(jnp.int32, sq.shape, 1) < (LANES // 2)
    a = jnp.sum(jnp.where(first, sq, 0.0), axis=-1, keepdims=True)
    b = jnp.sum(jnp.where(first, 0.0, sq), axis=-1, keepdims=True)
    return jnp.where(first, a, b)


def _inproj_retention_kernel(x_ref, g1_ref, w32_ref, cos_ref, sin_ref, gq_ref, gk_ref,
                             dmat_ref, qdec_ref, kdec_ref, cdec_ref, bdmask_ref, gain_ref,
                             *rest, width, kv_block, n_cast):
    cast_in, rest = rest[:n_cast], rest[n_cast:]
    (ret_ref, dqt_ref, dk_ref, dvt_ref), rest = rest[:4], rest[4:]
    cast_out, rest = rest[:n_cast], rest[n_cast:]
    w_ref, state_ref, rq_ref, rk_ref, rv_ref, rg_ref, acc_ref = rest

    @pl.when((pl.program_id(0) == 0) & (pl.program_id(1) == 0))
    def _():
        w_ref[...] = w32_ref[...].astype(BF16)

    @pl.when(pl.program_id(1) == 0)
    def _():
        state_ref[...] = jnp.zeros_like(state_ref)

    for src_ref, dst_ref in zip(cast_in, cast_out):
        dst_ref[...] = src_ref[...].astype(BF16)

    x = x_ref[0]
    tm = x.shape[0]
    ms = jnp.mean(x * x, axis=-1, keepdims=True)
    h = (x * lax.rsqrt(ms + NORM_EPS) * g1_ref[...]).astype(BF16)

    def proj(j):
        return _dot(h, w_ref[:, j * width:(j + 1) * width])

    n_tiles = width // LANES
    tiles = [slice(t * LANES, (t + 1) * LANES) for t in range(n_tiles)]
    cos = cos_ref[...]
    sin = sin_ref[...]
    lane = lax.broadcasted_iota(jnp.int32, (tm, LANES), 1)
    first_half = (lane % RET_HD) < (RET_HD // 2)

    def rotary(t):
        partner = jnp.where(first_half, pltpu.roll(t, LANES - RET_HD // 2, 1),
                            pltpu.roll(t, RET_HD // 2, 1))
        return t * cos + partner * sin

    rq = proj(0)
    for sl in tiles:
        rq_ref[:, sl] = rotary(rq[:, sl]).astype(BF16)
    rk = proj(1)
    for sl in tiles:
        rk_ref[:, sl] = (rotary(rk[:, sl]) * (RET_HD ** -0.5)).astype(BF16)
    rv_ref[...] = proj(2).astype(BF16)
    rg_ref[...] = proj(3).astype(BF16)

    c = RET_CHUNK
    n_chunks = tm // c
    head_a = lax.broadcasted_iota(jnp.int32, (c, LANES), 1) < RET_HD
    bdmask = bdmask_ref[...]

    def retention(p):
        sl = tiles[p]
        inners, kvs = [], []
        for ci in range(n_chunks):
            rows = slice(ci * c, (ci + 1) * c)
            q2 = rq_ref[rows, sl]
            k2 = rk_ref[rows, sl]
            zero = jnp.zeros_like(q2)
            qs = jnp.concatenate([jnp.where(head_a, q2, zero), jnp.where(head_a, zero, q2)], axis=0)
            scores = _dot_nt(qs, k2)
            inners.append((scores * dmat_ref[p]).astype(BF16))
            kd = (k2.astype(F32) * kdec_ref[:, sl]).astype(BF16)
            kvs.append(_dot_tn(kd, rv_ref[rows, sl]) * bdmask)
        state = state_ref[p]
        for ci in range(n_chunks):
            rows = slice(ci * c, (ci + 1) * c)
            v2 = rv_ref[rows, sl]
            zero = jnp.zeros_like(v2)
            qd = (rq_ref[rows, sl].astype(F32) * qdec_ref[:, sl]).astype(BF16)
            lhs = jnp.concatenate([inners[ci][:c], inners[ci][c:], qd], axis=1)
            rhs = jnp.concatenate([jnp.where(head_a, v2, zero), jnp.where(head_a, zero, v2),
                                   state.astype(BF16)], axis=0)
            acc_ref[rows, sl] = _dot(lhs, rhs)
            state = state * cdec_ref[:, sl] + kvs[ci]
        state_ref[p] = state

        o = acc_ref[:, sl]
        ssq = _half_sums(o * o)
        y = o * lax.rsqrt(ssq * (1.0 / RET_HD) + NORM_EPS) * gain_ref[:, sl]
        g = rg_ref[:, sl].astype(F32)
        ret_ref[0, :, sl] = (y * (g / (1.0 + jnp.exp(-g)))).astype(BF16)

    def qk_norm(t, g):
        ssq = _half_sums(t * t)
        return t * lax.rsqrt(ssq * (1.0 / DIFF_HD) + NORM_EPS) * g

    def diff_q():
        dq = proj(4)
        q_scale = (DIFF_HD ** -0.5) * LOG2E
        for sl in tiles:
            qn = qk_norm(dq[:, sl], gq_ref[...]) * q_scale
            dqt_ref[0, sl, :] = qn.T.astype(BF16)

    def diff_k():
        dk = proj(5)
        for sl in tiles:
            dk_ref[0, :, sl] = qk_norm(dk[:, sl], gk_ref[...]).astype(BF16)

    def diff_v():
        dv = proj(6)
        for sl in tiles:
            vt = dv[:, sl].T.astype(BF16)
            for j in range(tm // kv_block):
                dvt_ref[0, j, sl, :] = vt[:, j * kv_block:(j + 1) * kv_block]

    others = [diff_q, diff_k, diff_v]
    for p in range(n_tiles):
        retention(p)
        if p < len(others):
            others[p]()
    for f in others[n_tiles:]:
        f()


def _diffattn_kernel(qt_ref, k_ref, vt_ref, lq1_ref, lk1_ref, lq2_ref, lk2_ref, gain_ref,
                     o_ref, s_ref, smax_ref, tri_ref, m_ref, l_ref, acc_ref, *, qblock, block, strip, lead):
    qi = pl.program_id(2)
    ring = s_ref.shape[0]
    n_strips = qblock // strip
    chains = [(mi, t) for mi in range(2) for t in reversed(range(n_strips))]
    n_chains = len(chains)
    per_q = qblock // block
    qt = qt_ref[0]
    row = lax.broadcasted_iota(jnp.int32, qt.shape, 0)
    zero = jnp.zeros_like(qt)
    q_maps = (jnp.where(row < DIFF_HD, qt, zero), jnp.where(row < DIFF_HD, zero, qt))

    kpos = lax.broadcasted_iota(jnp.int32, (strip, strip), 0)
    qpos = lax.broadcasted_iota(jnp.int32, (strip, strip), 1)
    tri_ref[...] = jnp.where(kpos <= qpos, 0.0, MASK_VALUE)

    def scores(kj, c, slot, nkeys=block, tri_rows=None):
        mi, t = chains[c]
        start = pl.multiple_of(kj * block, block)
        k = k_ref[0, pl.ds(start, nkeys), :]
        sc = _dot(k, q_maps[mi][:, t * strip:(t + 1) * strip])
        if tri_rows is not None:
            own = sc[tri_rows:] + tri_ref[...]
            sc = own if tri_rows == 0 else jnp.concatenate([sc[:tri_rows], own], axis=0)
        s_ref[slot, :nkeys, :] = sc
        smax_ref[slot] = jnp.max(sc, axis=0, keepdims=True)

    def softmax_pv(kj, c, slot, nkeys=block):
        mi, t = chains[c]
        cols = slice(t * strip, (t + 1) * strip)
        s = s_ref[slot, :nkeys, :]
        m_old = m_ref[mi, :, cols]
        m_new = jnp.maximum(m_old, smax_ref[slot])
        alpha = jnp.exp2(m_old - m_new)
        p = jnp.exp2(s - m_new)
        l_ref[mi, :, cols] = alpha * l_ref[mi, :, cols] + jnp.sum(p, axis=0, keepdims=True)
        acc_ref[mi, :, cols] = (alpha * acc_ref[mi, :, cols]
                                + _dot(vt_ref[0, kj, :, :nkeys], p.astype(BF16)))
        m_ref[mi, :, cols] = m_new

    for c in range(lead):
        scores(0, c, c)
    m_ref[...] = jnp.full_like(m_ref, MASK_VALUE)
    l_ref[...] = jnp.zeros_like(l_ref)
    acc_ref[...] = jnp.zeros_like(acc_ref)

    trip = 2 * n_chains

    def body(t, carry):
        for j in range(trip):
            nxt = j + lead
            scores(2 * t + nxt // n_chains, nxt % n_chains, nxt % ring)
            softmax_pv(2 * t + j // n_chains, j % n_chains, j % ring)
        return carry

    n_full = qi * per_q
    lax.fori_loop(0, n_full // 2, body, 0)

    tail = []
    for d in range(per_q):
        for c, (mi, t) in enumerate(chains):
            nkeys = min(block, (t + 1) * strip - d * block)
            if nkeys > 0:
                own = t * strip - d * block
                tail.append((d, c, nkeys, own if own < nkeys else None))
    lam = (jnp.exp(jnp.sum(lq1_ref[...] * lk1_ref[...], axis=-1, keepdims=True))
           - jnp.exp(jnp.sum(lq2_ref[...] * lk2_ref[...], axis=-1, keepdims=True))
           + LAMBDA_INIT)

    def finish(t):
        cols = slice(t * strip, (t + 1) * strip)
        o = (acc_ref[0, :, cols] / l_ref[0, :, cols]
             - lam * (acc_ref[1, :, cols] / l_ref[1, :, cols]))
        ms = jnp.mean(o * o, axis=0, keepdims=True)
        o = o * lax.rsqrt(ms + NORM_EPS)
        o_ref[0, cols, :] = (o.T * gain_ref[...] * (1.0 - LAMBDA_INIT)).astype(BF16)

    last_use = {chains[c][1]: i for i, (_, c, _, _) in enumerate(tail)}
    for i, (d, c, nkeys, tri_rows) in enumerate(tail):
        if i < lead:
            assert d == 0 and c == i and nkeys == block and tri_rows is None
        else:
            assert tri_rows is None or (tri_rows >= 0 and tri_rows + strip == nkeys)
        if i + lead < len(tail):
            d2, c2, nkeys2, tri2 = tail[i + lead]
            scores(n_full + d2, c2, (i + lead) % ring, nkeys2, tri2)
        softmax_pv(n_full + d, c, i % ring, nkeys)
        if i == last_use[chains[c][1]]:
            finish(chains[c][1])


def _out_ffn_kernel(x_ref, ret_ref, dif_ref, wo_ref, g2_ref, wg_ref, wu_ref, wd_ref, o_ref, *,
                    ret_width, ff_chunks):
    x1 = (x_ref[0] + _dot(ret_ref[0], wo_ref[:ret_width, :])
          + _dot(dif_ref[0], wo_ref[ret_width:, :]))
    ms = jnp.mean(x1 * x1, axis=-1, keepdims=True)
    h = (x1 * lax.rsqrt(ms + NORM_EPS) * g2_ref[...]).astype(BF16)
    o_ref[0] = x1
    ff = None
    for lo, hi in ff_chunks:
        g = _dot(h, wg_ref[:, lo:hi])
        u = _dot(h, wu_ref[:, lo:hi])
        a = (g / (1.0 + jnp.exp(-g)) * u).astype(BF16)
        d = _dot(a, wd_ref[lo:hi, :])
        ff = d if ff is None else ff + d
    o_ref[0] += ff


def _resident(shape):
    return pl.BlockSpec(shape, lambda *_: (0,) * len(shape), pipeline_mode=pl.Buffered(1))


def _ff_chunks(d_ff, mxu_cols=256, target=1536):
    chunks, lo = [], 0
    while lo < d_ff:
        hi = min(d_ff, lo + target)
        if d_ff - hi and (d_ff - hi) < mxu_cols:
            hi = d_ff
        chunks.append((lo, hi))
        lo = hi
    return tuple(chunks)


def kernel(x, norm1_g, w_in, ret_norm_g, diff_q_norm_g, diff_k_norm_g, lambda_q1, lambda_k1,
           lambda_q2, lambda_k2, diff_subln_g, w_out, norm2_g, w_gate, w_up, w_down):
    B, S, D = x.shape
    depth = w_in.shape[0]
    n_ret_heads = ret_norm_g.shape[1]
    ret_width = n_ret_heads * RET_HD
    in_width = w_in.shape[2]
    width = ret_width
    assert in_width == 7 * width and w_out.shape[1] == 2 * width
    assert width % LANES == 0 and S % ROW_TILE == 0 and S % ATT_BLOCK == 0
    assert ROW_TILE % ATT_BLOCK == 0 and ROW_TILE % RET_CHUNK == 0
    assert all(w.shape[1] % (B * (S // ROW_TILE)) == 0 for w in (w_out, w_gate, w_up, w_down))
    assert S % ATT_QBLOCK == 0 and (ATT_QBLOCK // ATT_BLOCK) % 2 == 0 and ATT_QBLOCK % ATT_STRIP == 0
    assert (4 * ATT_QBLOCK // ATT_STRIP) % ATT_RING == 0 and ATT_LEAD < ATT_RING
    assert ATT_LEAD <= 2 * ATT_QBLOCK // ATT_STRIP
    n_diff_heads = width // DIFF_VD
    n_tiles = width // LANES
    d_ff = w_gate.shape[2]
    nk = S // ATT_BLOCK
    params = functools.partial(pltpu.CompilerParams, vmem_limit_bytes=VMEM_LIMIT_BYTES)

    pos = np.arange(S, dtype=np.float64)
    freqs = 1.0 / (ROPE_BASE ** (np.arange(0, RET_HD, 2, dtype=np.float64) / RET_HD))
    ang = pos[:, None] * freqs[None, :]
    cos_t = jnp.asarray(np.tile(np.cos(ang), (1, LANES // (RET_HD // 2))), F32)
    sin_half = np.sin(ang)
    sin_t = jnp.asarray(np.tile(np.concatenate([-sin_half, sin_half], axis=-1), (1, HEADS_PER_TILE)), F32)
    grp = np.arange(LANES) // RET_HD
    same_group = (grp[:, None] == grp[None, :]).astype(np.float32)
    bdmask = jnp.asarray(same_group, F32)

    log_g = np.log(1.0 - 2.0 ** (-5.0 - np.arange(n_ret_heads, dtype=np.float64)))
    idx = np.arange(RET_CHUNK, dtype=np.float64)
    rel = idx[:, None] - idx[None, :]
    decay_in = np.where(rel[None] >= 0, np.exp(log_g[:, None, None] * np.maximum(rel, 0.0)[None]), 0.0)
    dmat = jnp.asarray(decay_in.reshape(n_tiles, HEADS_PER_TILE * RET_CHUNK, RET_CHUNK), F32)
    by_lane = lambda t: jnp.asarray(np.repeat(t, RET_HD, axis=0).T, F32)
    qdec = by_lane(np.exp(log_g[:, None] * (idx[None, :] + 1.0)))
    kdec = by_lane(np.exp(log_g[:, None] * (RET_CHUNK - 1.0 - idx[None, :])))
    cdec = by_lane(np.exp(log_g * RET_CHUNK)[:, None])

    for l in range(depth):
        row = lambda b, i: (b, i, 0)
        col = lambda b, i: (b, 0, i)
        tab = lambda b, i: (i, 0)
        act = jax.ShapeDtypeStruct((B, S, width), BF16)
        act_t = jax.ShapeDtypeStruct((B, width, S), BF16)
        tile_bf16 = pltpu.VMEM((ROW_TILE, width), BF16)
        n_steps = B * (S // ROW_TILE)
        later = [w_out[l], w_gate[l], w_up[l], w_down[l]]
        later = [w.reshape(n_steps, w.shape[0] // n_steps, w.shape[1]) for w in later]
        step = lambda b, i: (b * (S // ROW_TILE) + i, 0, 0)
        cast_specs = [pl.BlockSpec((1,) + w.shape[1:], step) for w in later]
        ret, dqt, dk, dvt, wo_b, wg_b, wu_b, wd_b = pl.pallas_call(
            functools.partial(_inproj_retention_kernel, width=width, kv_block=ATT_BLOCK,
                              n_cast=len(later)),
            grid=(B, S // ROW_TILE),
            in_specs=[
                pl.BlockSpec((1, ROW_TILE, D), row),
                _resident((1, D)),
                _resident((D, in_width)),
                pl.BlockSpec((ROW_TILE, LANES), tab),
                pl.BlockSpec((ROW_TILE, LANES), tab),
                _resident((1, LANES)),
                _resident((1, LANES)),
                _resident(dmat.shape), _resident(qdec.shape), _resident(kdec.shape),
                _resident(cdec.shape), _resident(bdmask.shape), _resident((1, width)),
            ] + cast_specs,
            out_specs=[
                pl.BlockSpec((1, ROW_TILE, width), row),
                pl.BlockSpec((1, width, ROW_TILE), col),
                pl.BlockSpec((1, ROW_TILE, width), row),
                pl.BlockSpec((1, ROW_TILE // ATT_BLOCK, width, ATT_BLOCK), lambda b, i: (b, i, 0, 0)),
            ] + cast_specs,
            out_shape=[act, act_t, act, jax.ShapeDtypeStruct((B, nk, width, ATT_BLOCK), BF16)]
                      + [jax.ShapeDtypeStruct(w.shape, BF16) for w in later],
            scratch_shapes=[pltpu.VMEM((D, in_width), BF16),
                            pltpu.VMEM((n_tiles, LANES, LANES), F32),
                            tile_bf16, tile_bf16, tile_bf16, tile_bf16,
                            pltpu.VMEM((ROW_TILE, width), F32)],
            compiler_params=params(dimension_semantics=("arbitrary", "arbitrary")),
            name="inproj_retention",
        )(x, norm1_g[l][None, :], w_in[l], cos_t, sin_t,
          jnp.tile(diff_q_norm_g[l], LANES // DIFF_HD)[None, :],
          jnp.tile(diff_k_norm_g[l], LANES // DIFF_HD)[None, :],
          dmat, qdec, kdec, cdec, bdmask, ret_norm_g[l].reshape(1, width), *later)
        wo_b, wg_b, wu_b, wd_b = (b.reshape(w.shape) for b, w in
                                  zip((wo_b, wg_b, wu_b, wd_b), (w_out[l], w_gate[l], w_up[l], w_down[l])))

        lam_spec = _resident((1, DIFF_HD))
        dif = pl.pallas_call(
            functools.partial(_diffattn_kernel, qblock=ATT_QBLOCK, block=ATT_BLOCK, strip=ATT_STRIP,
                              lead=ATT_LEAD),
            grid=(B, n_diff_heads, S // ATT_QBLOCK),
            in_specs=[
                pl.BlockSpec((1, DIFF_VD, ATT_QBLOCK), lambda b, h, i: (b, h, i)),
                pl.BlockSpec((1, S, DIFF_VD), lambda b, h, i: (b, 0, h)),
                pl.BlockSpec((1, nk, DIFF_VD, ATT_BLOCK), lambda b, h, i: (b, 0, h, 0)),
                lam_spec, lam_spec, lam_spec, lam_spec,
                _resident((1, DIFF_VD)),
            ],
            out_specs=pl.BlockSpec((1, ATT_QBLOCK, DIFF_VD), lambda b, h, i: (b, i, h)),
            out_shape=act,
            scratch_shapes=[pltpu.VMEM((ATT_RING, ATT_BLOCK, ATT_STRIP), F32),
                            pltpu.VMEM((ATT_RING, 1, ATT_STRIP), F32),
                            pltpu.VMEM((ATT_STRIP, ATT_STRIP), F32),
                            pltpu.VMEM((2, 1, ATT_QBLOCK), F32),
                            pltpu.VMEM((2, 1, ATT_QBLOCK), F32),
                            pltpu.VMEM((2, DIFF_VD, ATT_QBLOCK), F32)],
            compiler_params=params(dimension_semantics=("parallel", "parallel", "arbitrary")),
            name="diffattn",
        )(dqt, dk, dvt, lambda_q1[l][None, :], lambda_k1[l][None, :], lambda_q2[l][None, :],
          lambda_k2[l][None, :], diff_subln_g[l][None, :])

        tile = pl.BlockSpec((1, ROW_TILE, D), row)
        half = pl.BlockSpec((1, ROW_TILE, width), row)
        x = pl.pallas_call(
            functools.partial(_out_ffn_kernel, ret_width=width, ff_chunks=_ff_chunks(d_ff)),
            grid=(B, S // ROW_TILE),
            in_specs=[tile, half, half,
                      _resident((2 * width, D)), _resident((1, D)),
                      _resident((D, d_ff)), _resident((D, d_ff)), _resident((d_ff, D))],
            out_specs=tile,
            out_shape=jax.ShapeDtypeStruct((B, S, D), F32),
            compiler_params=params(dimension_semantics=("parallel", "parallel")),
            name="out_ffn",
        )(x, ret, dif, wo_b, norm2_g[l][None, :], wg_b, wu_b, wd_b)
    return x
```

```python
import functools
import math

import jax
import jax.numpy as jnp
import numpy as np
from jax import lax
from jax.experimental import pallas as pl
from jax.experimental.pallas import tpu as pltpu

F32 = jnp.float32
BF16 = jnp.bfloat16

LANES = 128
VMEM_LIMIT_BYTES = 56 * 1024 * 1024

RET_HD = 64
DIFF_HD = 64
DIFF_VD = 2 * DIFF_HD
HEADS_PER_TILE = LANES // RET_HD
RET_CHUNK = 128
ROPE_BASE = 10000.0
NORM_EPS = 1e-6
LAMBDA_INIT = 0.8 - 0.6 * math.exp(-0.3 * 0)
MASK_VALUE = -1e30
LOG2E = 1.0 / math.log(2.0)

ROW_TILE = 512
ATT_QBLOCK = 4096
ATT_BLOCK = 512
ATT_STRIP = 256
ATT_RING = 32
ATT_LEAD = 4


def _dot(a, b):
    return jnp.dot(a, b, preferred_element_type=F32)


def _dot_nt(a, b):
    return lax.dot_general(a, b, (((1,), (1,)), ((), ())), preferred_element_type=F32)


def _dot_tn(a, b):
    return lax.dot_general(a, b, (((0,), (0,)), ((), ())), preferred_element_type=F32)


def _half_sums(sq):
    first = lax.broadcasted_iota(jnp.int32, sq.shape, 1) < (LANES // 2)
    a = jnp.sum(jnp.where(first, sq, 0.0), axis=-1, keepdims=True)
    b = jnp.sum(jnp.where(first, 0.0, sq), axis=-1, keepdims=True)
    return jnp.where(first, a, b)


def _inproj_retention_kernel(x_ref, g1_ref, w32_ref, cos_ref, sin_ref, gq_ref, gk_ref,
                             dmat_ref, qdec_ref, kdec_ref, cdec_ref, bdmask_ref, gain_ref,
                             *rest, width, kv_block, n_cast):
    cast_in, rest = rest[:n_cast], rest[n_cast:]
    (ret_ref, dqt_ref, dk_ref, dvt_ref), rest = rest[:4], rest[4:]
    cast_out, rest = rest[:n_cast], rest[n_cast:]
    w_ref, state_ref, rq_ref, rk_ref, rv_ref, rg_ref, acc_ref = rest

    @pl.when((pl.program_id(0) == 0) & (pl.program_id(1) == 0))
    def _():
        w_ref[...] = w32_ref[...].astype(BF16)

    @pl.when(pl.program_id(1) == 0)
    def _():
        state_ref[...] = jnp.zeros_like(state_ref)

    for src_ref, dst_ref in zip(cast_in, cast_out):
        dst_ref[...] = src_ref[...].astype(BF16)

    x = x_ref[0]
    tm = x.shape[0]
    ms = jnp.mean(x * x, axis=-1, keepdims=True)
    h = (x * lax.rsqrt(ms + NORM_EPS) * g1_ref[...]).astype(BF16)

    def proj(j):
        return _dot(h, w_ref[:, j * width:(j + 1) * width])

    n_tiles = width // LANES
    tiles = [slice(t * LANES, (t + 1) * LANES) for t in range(n_tiles)]
    cos = cos_ref[...]
    sin = sin_ref[...]
    lane = lax.broadcasted_iota(jnp.int32, (tm, LANES), 1)
    first_half = (lane % RET_HD) < (RET_HD // 2)

    def rotary(t):
        partner = jnp.where(first_half, pltpu.roll(t, LANES - RET_HD // 2, 1),
                            pltpu.roll(t, RET_HD // 2, 1))
        return t * cos + partner * sin

    rq = proj(0)
    for sl in tiles:
        rq_ref[:, sl] = rotary(rq[:, sl]).astype(BF16)
    rk = proj(1)
    for sl in tiles:
        rk_ref[:, sl] = (rotary(rk[:, sl]) * (RET_HD ** -0.5)).astype(BF16)
    rv_ref[...] = proj(2).astype(BF16)
    rg_ref[...] = proj(3).astype(BF16)

    c = RET_CHUNK
    n_chunks = tm // c
    head_a = lax.broadcasted_iota(jnp.int32, (c, LANES), 1) < RET_HD
    bdmask = bdmask_ref[...]

    def retention(p):
        sl = tiles[p]
        inners, kvs = [], []
        for ci in range(n_chunks):
            rows = slice(ci * c, (ci + 1) * c)
            q2 = rq_ref[rows, sl]
            k2 = rk_ref[rows, sl]
            zero = jnp.zeros_like(q2)
            qs = jnp.concatenate([jnp.where(head_a, q2, zero), jnp.where(head_a, zero, q2)], axis=0)
            scores = _dot_nt(qs, k2)
            inners.append((scores * dmat_ref[p]).astype(BF16))
            kd = (k2.astype(F32) * kdec_ref[:, sl]).astype(BF16)
            kvs.append(_dot_tn(kd, rv_ref[rows, sl]) * bdmask)
        state = state_ref[p]
        for ci in range(n_chunks):
            rows = slice(ci * c, (ci + 1) * c)
            v2 = rv_ref[rows, sl]
            zero = jnp.zeros_like(v2)
            qd = (rq_ref[rows, sl].astype(F32) * qdec_ref[:, sl]).astype(BF16)
            lhs = jnp.concatenate([inners[ci][:c], inners[ci][c:], qd], axis=1)
            rhs = jnp.concatenate([jnp.where(head_a, v2, zero), jnp.where(head_a, zero, v2),
                                   state.astype(BF16)], axis=0)
            acc_ref[rows, sl] = _dot(lhs, rhs)
            state = state * cdec_ref[:, sl] + kvs[ci]
        state_ref[p] = state

        o = acc_ref[:, sl]
        ssq = _half_sums(o * o)
        y = o * lax.rsqrt(ssq * (1.0 / RET_HD) + NORM_EPS) * gain_ref[:, sl]
        g = rg_ref[:, sl].astype(F32)
        ret_ref[0, :, sl] = (y * (g / (1.0 + jnp.exp(-g)))).astype(BF16)

    def qk_norm(t, g):
        ssq = _half_sums(t * t)
        return t * lax.rsqrt(ssq * (1.0 / DIFF_HD) + NORM_EPS) * g

    def diff_q():
        dq = proj(4)
        q_scale = (DIFF_HD ** -0.5) * LOG2E
        for sl in tiles:
            qn = qk_norm(dq[:, sl], gq_ref[...]) * q_scale
            dqt_ref[0, sl, :] = qn.T.astype(BF16)

    def diff_k():
        dk = proj(5)
        for sl in tiles:
            dk_ref[0, :, sl] = qk_norm(dk[:, sl], gk_ref[...]).astype(BF16)

    def diff_v():
        dv = proj(6)
        for sl in tiles:
            vt = dv[:, sl].T.astype(BF16)
            for j in range(tm // kv_block):
                dvt_ref[0, j, sl, :] = vt[:, j * kv_block:(j + 1) * kv_block]

    others = [diff_q, diff_k, diff_v]
    for p in range(n_tiles):
        retention(p)
        if p < len(others):
            others[p]()
    for f in others[n_tiles:]:
        f()


def _diffattn_kernel(qt_ref, k_ref, vt_ref, lq1_ref, lk1_ref, lq2_ref, lk2_ref, gain_ref,
                     o_ref, s_ref, smax_ref, tri_ref, m_ref, l_ref, acc_ref, *, qblock, block, strip, lead):
    qi = pl.program_id(2)
    ring = s_ref.shape[0]
    n_strips = qblock // strip
    chains = [(mi, t) for mi in range(2) for t in reversed(range(n_strips))]
    n_chains = len(chains)
    per_q = qblock // block
    qt = qt_ref[0]
    row = lax.broadcasted_iota(jnp.int32, qt.shape, 0)
    zero = jnp.zeros_like(qt)
    q_maps = (jnp.where(row < DIFF_HD, qt, zero), jnp.where(row < DIFF_HD, zero, qt))

    kpos = lax.broadcasted_iota(jnp.int32, (strip, strip), 0)
    qpos = lax.broadcasted_iota(jnp.int32, (strip, strip), 1)
    tri_ref[...] = jnp.where(kpos <= qpos, 0.0, MASK_VALUE)

    def scores(kj, c, slot, nkeys=block, tri_rows=None):
        mi, t = chains[c]
        start = pl.multiple_of(kj * block, block)
        k = k_ref[0, pl.ds(start, nkeys), :]
        sc = _dot(k, q_maps[mi][:, t * strip:(t + 1) * strip])
        if tri_rows is not None:
            own = sc[tri_rows:] + tri_ref[...]
            sc = own if tri_rows == 0 else jnp.concatenate([sc[:tri_rows], own], axis=0)
        s_ref[slot, :nkeys, :] = sc
        smax_ref[slot] = jnp.max(sc, axis=0, keepdims=True)

    def softmax_pv(kj, c, slot, nkeys=block):
        mi, t = chains[c]
        cols = slice(t * strip, (t + 1) * strip)
        s = s_ref[slot, :nkeys, :]
        m_old = m_ref[mi, :, cols]
        m_new = jnp.maximum(m_old, smax_ref[slot])
        alpha = jnp.exp2(m_old - m_new)
        p = jnp.exp2(s - m_new)
        l_ref[mi, :, cols] = alpha * l_ref[mi, :, cols] + jnp.sum(p, axis=0, keepdims=True)
        acc_ref[mi, :, cols] = (alpha * acc_ref[mi, :, cols]
                                + _dot(vt_ref[0, kj, :, :nkeys], p.astype(BF16)))
        m_ref[mi, :, cols] = m_new

    for c in range(lead):
        scores(0, c, c)
    m_ref[...] = jnp.full_like(m_ref, MASK_VALUE)
    l_ref[...] = jnp.zeros_like(l_ref)
    acc_ref[...] = jnp.zeros_like(acc_ref)

    trip = 2 * n_chains

    def body(t, carry):
        for j in range(trip):
            nxt = j + lead
            scores(2 * t + nxt // n_chains, nxt % n_chains, nxt % ring)
            softmax_pv(2 * t + j // n_chains, j % n_chains, j % ring)
        return carry

    n_full = qi * per_q
    lax.fori_loop(0, n_full // 2, body, 0)

    tail = []
    for d in range(per_q):
        for c, (mi, t) in enumerate(chains):
            nkeys = min(block, (t + 1) * strip - d * block)
            if nkeys > 0:
                own = t * strip - d * block
                tail.append((d, c, nkeys, own if own < nkeys else None))
    lam = (jnp.exp(jnp.sum(lq1_ref[...] * lk1_ref[...], axis=-1, keepdims=True))
           - jnp.exp(jnp.sum(lq2_ref[...] * lk2_ref[...], axis=-1, keepdims=True))
           + LAMBDA_INIT)

    def finish(t):
        cols = slice(t * strip, (t + 1) * strip)
        o = (acc_ref[0, :, cols] / l_ref[0, :, cols]
             - lam * (acc_ref[1, :, cols] / l_ref[1, :, cols]))
        ms = jnp.mean(o * o, axis=0, keepdims=True)
        o = o * lax.rsqrt(ms + NORM_EPS)
        o_ref[0, cols, :] = (o.T * gain_ref[...] * (1.0 - LAMBDA_INIT)).astype(BF16)

    last_use = {chains[c][1]: i for i, (_, c, _, _) in enumerate(tail)}
    for i, (d, c, nkeys, tri_rows) in enumerate(tail):
        if i < lead:
            assert d == 0 and c == i and nkeys == block and tri_rows is None
        else:
            assert tri_rows is None or (tri_rows >= 0 and tri_rows + strip == nkeys)
        if i + lead < len(tail):
            d2, c2, nkeys2, tri2 = tail[i + lead]
            scores(n_full + d2, c2, (i + lead) % ring, nkeys2, tri2)
        softmax_pv(n_full + d, c, i % ring, nkeys)
        if i == last_use[chains[c][1]]:
            finish(chains[c][1])


def _out_ffn_kernel(x_ref, ret_ref, dif_ref, wo_ref, g2_ref, wg_ref, wu_ref, wd_ref, o_ref, *,
                    ret_width, ff_chunks):
    x1 = (x_ref[0] + _dot(ret_ref[0], wo_ref[:ret_width, :])
          + _dot(dif_ref[0], wo_ref[ret_width:, :]))
    ms = jnp.mean(x1 * x1, axis=-1, keepdims=True)
    h = (x1 * lax.rsqrt(ms + NORM_EPS) * g2_ref[...]).astype(BF16)
    o_ref[0] = x1
    ff = None
    for lo, hi in ff_chunks:
        g = _dot(h, wg_ref[:, lo:hi])
        u = _dot(h, wu_ref[:, lo:hi])
        a = (g / (1.0 + jnp.exp(-g)) * u).astype(BF16)
        d = _dot(a, wd_ref[lo:hi, :])
        ff = d if ff is None else ff + d
    o_ref[0] += ff


def _resident(shape):
    return pl.BlockSpec(shape, lambda *_: (0,) * len(shape), pipeline_mode=pl.Buffered(1))


def _ff_chunks(d_ff, mxu_cols=256, target=1536):
    chunks, lo = [], 0
    while lo < d_ff:
        hi = min(d_ff, lo + target)
        if d_ff - hi and (d_ff - hi) < mxu_cols:
            hi = d_ff
        chunks.append((lo, hi))
        lo = hi
    return tuple(chunks)


def kernel(x, norm1_g, w_in, ret_norm_g, diff_q_norm_g, diff_k_norm_g, lambda_q1, lambda_k1,
           lambda_q2, lambda_k2, diff_subln_g, w_out, norm2_g, w_gate, w_up, w_down):
    B, S, D = x.shape
    depth = w_in.shape[0]
    n_ret_heads = ret_norm_g.shape[1]
    ret_width = n_ret_heads * RET_HD
    in_width = w_in.shape[2]
    width = ret_width
    assert in_width == 7 * width and w_out.shape[1] == 2 * width
    assert width % LANES == 0 and S % ROW_TILE == 0 and S % ATT_BLOCK == 0
    assert ROW_TILE % ATT_BLOCK == 0 and ROW_TILE % RET_CHUNK == 0
    assert all(w.shape[1] % (B * (S // ROW_TILE)) == 0 for w in (w_out, w_gate, w_up, w_down))
    assert S % ATT_QBLOCK == 0 and (ATT_QBLOCK // ATT_BLOCK) % 2 == 0 and ATT_QBLOCK % ATT_STRIP == 0
    assert (4 * ATT_QBLOCK // ATT_STRIP) % ATT_RING == 0 and ATT_LEAD < ATT_RING
    assert ATT_LEAD <= 2 * ATT_QBLOCK // ATT_STRIP
    n_diff_heads = width // DIFF_VD
    n_tiles = width // LANES
    d_ff = w_gate.shape[2]
    nk = S // ATT_BLOCK
    params = functools.partial(pltpu.CompilerParams, vmem_limit_bytes=VMEM_LIMIT_BYTES)

    pos = np.arange(S, dtype=np.float64)
    freqs = 1.0 / (ROPE_BASE ** (np.arange(0, RET_HD, 2, dtype=np.float64) / RET_HD))
    ang = pos[:, None] * freqs[None, :]
    cos_t = jnp.asarray(np.tile(np.cos(ang), (1, LANES // (RET_HD // 2))), F32)
    sin_half = np.sin(ang)
    sin_t = jnp.asarray(np.tile(np.concatenate([-sin_half, sin_half], axis=-1), (1, HEADS_PER_TILE)), F32)
    grp = np.arange(LANES) // RET_HD
    same_group = (grp[:, None] == grp[None, :]).astype(np.float32)
    bdmask = jnp.asarray(same_group, F32)

    log_g = np.log(1.0 - 2.0 ** (-5.0 - np.arange(n_ret_heads, dtype=np.float64)))
    idx = np.arange(RET_CHUNK, dtype=np.float64)
    rel = idx[:, None] - idx[None, :]
    decay_in = np.where(rel[None] >= 0, np.exp(log_g[:, None, None] * np.maximum(rel, 0.0)[None]), 0.0)
    dmat = jnp.asarray(decay_in.reshape(n_tiles, HEADS_PER_TILE * RET_CHUNK, RET_CHUNK), F32)
    by_lane = lambda t: jnp.asarray(np.repeat(t, RET_HD, axis=0).T, F32)
    qdec = by_lane(np.exp(log_g[:, None] * (idx[None, :] + 1.0)))
    kdec = by_lane(np.exp(log_g[:, None] * (RET_CHUNK - 1.0 - idx[None, :])))
    cdec = by_lane(np.exp(log_g * RET_CHUNK)[:, None])

    for l in range(depth):
        row = lambda b, i: (b, i, 0)
        col = lambda b, i: (b, 0, i)
        tab = lambda b, i: (i, 0)
        act = jax.ShapeDtypeStruct((B, S, width), BF16)
        act_t = jax.ShapeDtypeStruct((B, width, S), BF16)
        tile_bf16 = pltpu.VMEM((ROW_TILE, width), BF16)
        n_steps = B * (S // ROW_TILE)
        later = [w_out[l], w_gate[l], w_up[l], w_down[l]]
        later = [w.reshape(n_steps, w.shape[0] // n_steps, w.shape[1]) for w in later]
        step = lambda b, i: (b * (S // ROW_TILE) + i, 0, 0)
        cast_specs = [pl.BlockSpec((1,) + w.shape[1:], step) for w in later]
        ret, dqt, dk, dvt, wo_b, wg_b, wu_b, wd_b = pl.pallas_call(
            functools.partial(_inproj_retention_kernel, width=width, kv_block=ATT_BLOCK,
                              n_cast=len(later)),
            grid=(B, S // ROW_TILE),
            in_specs=[
                pl.BlockSpec((1, ROW_TILE, D), row),
                _resident((1, D)),
                _resident((D, in_width)),
                pl.BlockSpec((ROW_TILE, LANES), tab),
                pl.BlockSpec((ROW_TILE, LANES), tab),
                _resident((1, LANES)),
                _resident((1, LANES)),
                _resident(dmat.shape), _resident(qdec.shape), _resident(kdec.shape),
                _resident(cdec.shape), _resident(bdmask.shape), _resident((1, width)),
            ] + cast_specs,
            out_specs=[
                pl.BlockSpec((1, ROW_TILE, width), row),
                pl.BlockSpec((1, width, ROW_TILE), col),
                pl.BlockSpec((1, ROW_TILE, width), row),
                pl.BlockSpec((1, ROW_TILE // ATT_BLOCK, width, ATT_BLOCK), lambda b, i: (b, i, 0, 0)),
            ] + cast_specs,
            out_shape=[act, act_t, act, jax.ShapeDtypeStruct((B, nk, width, ATT_BLOCK), BF16)]
                      + [jax.ShapeDtypeStruct(w.shape, BF16) for w in later],
            scratch_shapes=[pltpu.VMEM((D, in_width), BF16),
                            pltpu.VMEM((n_tiles, LANES, LANES), F32),
                            tile_bf16, tile_bf16, tile_bf16, tile_bf16,
                            pltpu.VMEM((ROW_TILE, width), F32)],
            compiler_params=params(dimension_semantics=("arbitrary", "arbitrary")),
            name="inproj_retention",
        )(x, norm1_g[l][None, :], w_in[l], cos_t, sin_t,
          jnp.tile(diff_q_norm_g[l], LANES // DIFF_HD)[None, :],
          jnp.tile(diff_k_norm_g[l], LANES // DIFF_HD)[None, :],
          dmat, qdec, kdec, cdec, bdmask, ret_norm_g[l].reshape(1, width), *later)
        wo_b, wg_b, wu_b, wd_b = (b.reshape(w.shape) for b, w in
                                  zip((wo_b, wg_b, wu_b, wd_b), (w_out[l], w_gate[l], w_up[l], w_down[l])))

        lam_spec = _resident((1, DIFF_HD))
        dif = pl.pallas_call(
            functools.partial(_diffattn_kernel, qblock=ATT_QBLOCK, block=ATT_BLOCK, strip=ATT_STRIP,
                              lead=ATT_LEAD),
            grid=(B, n_diff_heads, S // ATT_QBLOCK),
            in_specs=[
                pl.BlockSpec((1, DIFF_VD, ATT_QBLOCK), lambda b, h, i: (b, h, i)),
                pl.BlockSpec((1, S, DIFF_VD), lambda b, h, i: (b, 0, h)),
                pl.BlockSpec((1, nk, DIFF_VD, ATT_BLOCK), lambda b, h, i: (b, 0, h, 0)),
                lam_spec, lam_spec, lam_spec, lam_spec,
                _resident((1, DIFF_VD)),
            ],
            out_specs=pl.BlockSpec((1, ATT_QBLOCK, DIFF_VD), lambda b, h, i: (b, i, h)),
            out_shape=act,
            scratch_shapes=[pltpu.VMEM((ATT_RING, ATT_BLOCK, ATT_STRIP), F32),
                            pltpu.VMEM((ATT_RING, 1, ATT_STRIP), F32),
                            pltpu.VMEM((ATT_STRIP, ATT_STRIP), F32),
                            pltpu.VMEM((2, 1, ATT_QBLOCK), F32),
                            pltpu.VMEM((2, 1, ATT_QBLOCK), F32),
                            pltpu.VMEM((2, DIFF_VD, ATT_QBLOCK), F32)],
            compiler_params=params(dimension_semantics=("parallel", "parallel", "arbitrary")),
            name="diffattn",
        )(dqt, dk, dvt, lambda_q1[l][None, :], lambda_k1[l][None, :], lambda_q2[l][None, :],
          lambda_k2[l][None, :], diff_subln_g[l][None, :])

        tile = pl.BlockSpec((1, ROW_TILE, D), row)
        half = pl.BlockSpec((1, ROW_TILE, width), row)
        x = pl.pallas_call(
            functools.partial(_out_ffn_kernel, ret_width=width, ff_chunks=_ff_chunks(d_ff)),
            grid=(B, S // ROW_TILE),
            in_specs=[tile, half, half,
                      _resident((2 * width, D)), _resident((1, D)),
                      _resident((D, d_ff)), _resident((D, d_ff)), _resident((d_ff, D))],
            out_specs=tile,
            out_shape=jax.ShapeDtypeStruct((B, S, D), F32),
            compiler_params=params(dimension_semantics=("parallel", "parallel")),
            name="out_ffn",
        )(x, ret, dif, wo_b, norm2_g[l][None, :], wg_b, wu_b, wd_b)
    return x
```

```python
import functools
import math

import jax
import jax.numpy as jnp
import numpy as np
from jax import lax
from jax.experimental import pallas as pl
from jax.experimental.pallas import tpu as pltpu

F32 = jnp.float32
BF16 = jnp.bfloat16

LANES = 128
SPARE_ROWS = 16
VMEM_LIMIT_BYTES = 56 * 1024 * 1024

RET_HD = 64
DIFF_HD = 64
DIFF_VD = 2 * DIFF_HD
HEADS_PER_TILE = LANES // RET_HD
RET_CHUNK = 128
ROPE_BASE = 10000.0
NORM_EPS = 1e-6
LAMBDA_INIT = 0.8 - 0.6 * math.exp(-0.3 * 0)
MASK_VALUE = -1e30
LOG2E = 1.0 / math.log(2.0)

ROW_TILE = 512
ATT_QBLOCK = 4096
ATT_BLOCK = 512
ATT_STRIP = 256
ATT_RING = 16
ATT_LEAD = 4


def _dot(a, b):
    return jnp.dot(a, b, preferred_element_type=F32)


def _dot_nt(a, b):
    return lax.dot_general(a, b, (((1,), (1,)), ((), ())), preferred_element_type=F32)


def _dot_tn(a, b):
    return lax.dot_general(a, b, (((0,), (0,)), ((), ())), preferred_element_type=F32)


def _half_sums(sq):
    first = lax.broadcasted_iota(jnp.int32, sq.shape, 1) < (LANES // 2)
    a = jnp.sum(jnp.where(first, sq, 0.0), axis=-1, keepdims=True)
    b = jnp.sum(jnp.where(first, 0.0, sq), axis=-1, keepdims=True)
    return jnp.where(first, a, b)


def _inproj_retention_kernel(x_ref, g1_ref, w32_ref, cos_ref, sin_ref, gq_ref, gk_ref,
                             dmat_ref, qdec_ref, kdec_ref, cdec_ref, bdmask_ref, gain_ref,
                             *rest, width, kv_block, n_cast):
    cast_in, rest = rest[:n_cast], rest[n_cast:]
    (ret_ref, dqt_ref, dk_ref, dvt_ref), rest = rest[:4], rest[4:]
    cast_out, rest = rest[:n_cast], rest[n_cast:]
    w_ref, state_ref, rq_ref, rk_ref, rv_ref, rg_ref, acc_ref, h_ref = rest

    @pl.when((pl.program_id(0) == 0) & (pl.program_id(1) == 0))
    def _():
        w_ref[...] = w32_ref[...].astype(BF16)

    @pl.when(pl.program_id(1) == 0)
    def _():
        state_ref[...] = jnp.zeros_like(state_ref)

    for src_ref, dst_ref in zip(cast_in, cast_out):
        dst_ref[...] = src_ref[...].astype(BF16)

    x = x_ref[0]
    tm = x.shape[0]
    ms = jnp.mean(x * x, axis=-1, keepdims=True)
    h = (x * lax.rsqrt(ms + NORM_EPS) * g1_ref[...]).astype(BF16)

    def proj(j, lhs=None):
        lhs = h if lhs is None else lhs
        return _dot(lhs, w_ref[:, j * width:(j + 1) * width])

    h_ref[:tm] = h

    def h_after(popped):
        h_ref[tm:, :LANES] = pltpu.bitcast(popped[:SPARE_ROWS // 2], BF16)
        start = pl.multiple_of(jnp.minimum(pl.program_id(1), 0), SPARE_ROWS)
        return h_ref[pl.ds(start, tm), :]

    n_tiles = width // LANES
    tiles = [slice(t * LANES, (t + 1) * LANES) for t in range(n_tiles)]
    cos = cos_ref[...]
    sin = sin_ref[...]
    lane = lax.broadcasted_iota(jnp.int32, (tm, LANES), 1)
    first_half = (lane % RET_HD) < (RET_HD // 2)

    def rotary(t):
        partner = jnp.where(first_half, pltpu.roll(t, LANES - RET_HD // 2, 1),
                            pltpu.roll(t, RET_HD // 2, 1))
        return t * cos + partner * sin

    rq = proj(0)
    for sl in tiles:
        rq_ref[:, sl] = rotary(rq[:, sl]).astype(BF16)
    rk = proj(1)
    for sl in tiles:
        rk_ref[:, sl] = (rotary(rk[:, sl]) * (RET_HD ** -0.5)).astype(BF16)
    rv_ref[...] = proj(2).astype(BF16)
    rg_ref[...] = proj(3).astype(BF16)

    c = RET_CHUNK
    n_chunks = tm // c
    head_a = lax.broadcasted_iota(jnp.int32, (c, LANES), 1) < RET_HD
    bdmask = bdmask_ref[...]

    def retention(p):
        sl = tiles[p]
        inners, kvs = [], []
        for ci in range(n_chunks):
            rows = slice(ci * c, (ci + 1) * c)
            q2 = rq_ref[rows, sl]
            k2 = rk_ref[rows, sl]
            zero = jnp.zeros_like(q2)
            qs = jnp.concatenate([jnp.where(head_a, q2, zero), jnp.where(head_a, zero, q2)], axis=0)
            scores = _dot_nt(qs, k2)
            inners.append((scores * dmat_ref[p]).astype(BF16))
            kd = (k2.astype(F32) * kdec_ref[:, sl]).astype(BF16)
            kvs.append(_dot_tn(kd, rv_ref[rows, sl]) * bdmask)
        state = state_ref[p]
        for ci in range(n_chunks):
            rows = slice(ci * c, (ci + 1) * c)
            v2 = rv_ref[rows, sl]
            zero = jnp.zeros_like(v2)
            qd = (rq_ref[rows, sl].astype(F32) * qdec_ref[:, sl]).astype(BF16)
            lhs = jnp.concatenate([inners[ci][:c], inners[ci][c:], qd], axis=1)
            rhs = jnp.concatenate([jnp.where(head_a, v2, zero), jnp.where(head_a, zero, v2),
                                   state.astype(BF16)], axis=0)
            o_chunk = _dot(lhs, rhs)
            popped = o_chunk if ci == 0 else popped
            acc_ref[rows, sl] = o_chunk
            state = state * cdec_ref[:, sl] + kvs[ci]
        state_ref[p] = state

        o = acc_ref[:, sl]
        ssq = _half_sums(o * o)
        y = o * lax.rsqrt(ssq * (1.0 / RET_HD) + NORM_EPS) * gain_ref[:, sl]
        g = rg_ref[:, sl].astype(F32)
        ret_ref[0, :, sl] = (y * (g / (1.0 + jnp.exp(-g)))).astype(BF16)
        return popped

    def qk_norm(t, g):
        ssq = _half_sums(t * t)
        return t * lax.rsqrt(ssq * (1.0 / DIFF_HD) + NORM_EPS) * g

    def diff_q(popped):
        dq = proj(4, h_after(popped))
        q_scale = (DIFF_HD ** -0.5) * LOG2E
        for sl in tiles:
            qn = qk_norm(dq[:, sl], gq_ref[...]) * q_scale
            dqt_ref[0, sl, :] = qn.T.astype(BF16)

    def diff_k(popped):
        dk = proj(5, h_after(popped))
        for sl in tiles:
            dk_ref[0, :, sl] = qk_norm(dk[:, sl], gk_ref[...]).astype(BF16)

    def diff_v(popped):
        dv = proj(6, h_after(popped))
        for sl in tiles:
            vt = dv[:, sl].T.astype(BF16)
            for j in range(tm // kv_block):
                dvt_ref[0, j, sl, :] = vt[:, j * kv_block:(j + 1) * kv_block]

    others = [diff_q, diff_k, diff_v]
    assert len(others) < n_tiles
    for p in range(n_tiles):
        popped = retention(p)
        if p < len(others):
            others[p](popped)


def _diffattn_kernel(qt_ref, k_ref, vt_ref, lq1_ref, lk1_ref, lq2_ref, lk2_ref, gain_ref,
                     o_ref, s_ref, smax_ref, tri_ref, m_ref, l_ref, acc_ref, *, qblock, block, strip, lead):
    qi = pl.program_id(2)
    ring = s_ref.shape[0]
    n_strips = qblock // strip
    chains = [(mi, t) for mi in range(2) for t in reversed(range(n_strips))]
    n_chains = len(chains)
    per_q = qblock // block
    qt = qt_ref[0]
    row = lax.broadcasted_iota(jnp.int32, qt.shape, 0)
    zero = jnp.zeros_like(qt)
    q_maps = (jnp.where(row < DIFF_HD, qt, zero), jnp.where(row < DIFF_HD, zero, qt))

    kpos = lax.broadcasted_iota(jnp.int32, (strip, strip), 0)
    qpos = lax.broadcasted_iota(jnp.int32, (strip, strip), 1)
    tri_ref[...] = jnp.where(kpos <= qpos, 0.0, MASK_VALUE)

    def scores(kj, c, slot, nkeys=block, tri_rows=None):
        mi, t = chains[c]
        start = pl.multiple_of(kj * block, block)
        k = k_ref[0, pl.ds(start, nkeys), :]
        sc = _dot(k, q_maps[mi][:, t * strip:(t + 1) * strip])
        if tri_rows is not None:
            own = sc[tri_rows:] + tri_ref[...]
            sc = own if tri_rows == 0 else jnp.concatenate([sc[:tri_rows], own], axis=0)
        s_ref[slot, :nkeys, :] = sc
        smax_ref[slot] = jnp.max(sc, axis=0, keepdims=True)

    def softmax_pv(kj, c, slot, nkeys=block):
        mi, t = chains[c]
        cols = slice(t * strip, (t + 1) * strip)
        s = s_ref[slot, :nkeys, :]
        m_old = m_ref[mi, :, cols]
        m_new = jnp.maximum(m_old, smax_ref[slot])
        alpha = jnp.exp2(m_old - m_new)
        p = jnp.exp2(s - m_new)
        l_ref[mi, :, cols] = alpha * l_ref[mi, :, cols] + jnp.sum(p, axis=0, keepdims=True)
        acc_ref[mi, :, cols] = (alpha * acc_ref[mi, :, cols]
                                + _dot(vt_ref[0, kj, :, :nkeys], p.astype(BF16)))
        m_ref[mi, :, cols] = m_new

    for c in range(lead):
        scores(0, c, c)
    m_ref[...] = jnp.full_like(m_ref, MASK_VALUE)
    l_ref[...] = jnp.zeros_like(l_ref)
    acc_ref[...] = jnp.zeros_like(acc_ref)

    trip = 2 * n_chains

    def body(t, carry):
        for j in range(trip):
            nxt = j + lead
            scores(2 * t + nxt // n_chains, nxt % n_chains, nxt % ring)
            softmax_pv(2 * t + j // n_chains, j % n_chains, j % ring)
        return carry

    n_full = qi * per_q
    lax.fori_loop(0, n_full // 2, body, 0)

    tail = []
    for d in range(per_q):
        for c, (mi, t) in enumerate(chains):
            nkeys = min(block, (t + 1) * strip - d * block)
            if nkeys > 0:
                own = t * strip - d * block
                tail.append((d, c, nkeys, own if own < nkeys else None))
    lam = (jnp.exp(jnp.sum(lq1_ref[...] * lk1_ref[...], axis=-1, keepdims=True))
           - jnp.exp(jnp.sum(lq2_ref[...] * lk2_ref[...], axis=-1, keepdims=True))
           + LAMBDA_INIT)

    def finish(t):
        cols = slice(t * strip, (t + 1) * strip)
        o = (acc_ref[0, :, cols] / l_ref[0, :, cols]
             - lam * (acc_ref[1, :, cols] / l_ref[1, :, cols]))
        ms = jnp.mean(o * o, axis=0, keepdims=True)
        o = o * lax.rsqrt(ms + NORM_EPS)
        o_ref[0, cols, :] = (o.T * gain_ref[...] * (1.0 - LAMBDA_INIT)).astype(BF16)

    last_use = {chains[c][1]: i for i, (_, c, _, _) in enumerate(tail)}
    for i, (d, c, nkeys, tri_rows) in enumerate(tail):
        if i < lead:
            assert d == 0 and c == i and nkeys == block and tri_rows is None
        else:
            assert tri_rows is None or (tri_rows >= 0 and tri_rows + strip == nkeys)
        if i + lead < len(tail):
            d2, c2, nkeys2, tri2 = tail[i + lead]
            scores(n_full + d2, c2, (i + lead) % ring, nkeys2, tri2)
        softmax_pv(n_full + d, c, i % ring, nkeys)
        if i == last_use[chains[c][1]]:
            finish(chains[c][1])


def _out_ffn_kernel(x_ref, ret_ref, dif_ref, wo_ref, g2_ref, wg_ref, wu_ref, wd_ref, o_ref, *,
                    ret_width, ff_chunks):
    x1 = (x_ref[0] + _dot(ret_ref[0], wo_ref[:ret_width, :])
          + _dot(dif_ref[0], wo_ref[ret_width:, :]))
    ms = jnp.mean(x1 * x1, axis=-1, keepdims=True)
    h = (x1 * lax.rsqrt(ms + NORM_EPS) * g2_ref[...]).astype(BF16)
    o_ref[0] = x1
    ff = None
    for lo, hi in ff_chunks:
        g = _dot(h, wg_ref[:, lo:hi])
        u = _dot(h, wu_ref[:, lo:hi])
        a = (g / (1.0 + jnp.exp(-g)) * u).astype(BF16)
        d = _dot(a, wd_ref[lo:hi, :])
        ff = d if ff is None else ff + d
    o_ref[0] += ff


def _resident(shape):
    return pl.BlockSpec(shape, lambda *_: (0,) * len(shape), pipeline_mode=pl.Buffered(1))


def _ff_chunks(d_ff, mxu_cols=256, target=1536):
    chunks, lo = [], 0
    while lo < d_ff:
        hi = min(d_ff, lo + target)
        if d_ff - hi and (d_ff - hi) < mxu_cols:
            hi = d_ff
        chunks.append((lo, hi))
        lo = hi
    return tuple(chunks)


def kernel(x, norm1_g, w_in, ret_norm_g, diff_q_norm_g, diff_k_norm_g, lambda_q1, lambda_k1,
           lambda_q2, lambda_k2, diff_subln_g, w_out, norm2_g, w_gate, w_up, w_down):
    B, S, D = x.shape
    depth = w_in.shape[0]
    n_ret_heads = ret_norm_g.shape[1]
    ret_width = n_ret_heads * RET_HD
    in_width = w_in.shape[2]
    width = ret_width
    assert in_width == 7 * width and w_out.shape[1] == 2 * width
    assert width % LANES == 0 and S % ROW_TILE == 0 and S % ATT_BLOCK == 0
    assert ROW_TILE % ATT_BLOCK == 0 and ROW_TILE % RET_CHUNK == 0
    assert all(w.shape[1] % (B * (S // ROW_TILE)) == 0 for w in (w_out, w_gate, w_up, w_down))
    assert S % ATT_QBLOCK == 0 and (ATT_QBLOCK // ATT_BLOCK) % 2 == 0 and ATT_QBLOCK % ATT_STRIP == 0
    assert (4 * ATT_QBLOCK // ATT_STRIP) % ATT_RING == 0 and ATT_LEAD < ATT_RING
    assert ATT_LEAD <= 2 * ATT_QBLOCK // ATT_STRIP
    n_diff_heads = width // DIFF_VD
    n_tiles = width // LANES
    d_ff = w_gate.shape[2]
    nk = S // ATT_BLOCK
    params = functools.partial(pltpu.CompilerParams, vmem_limit_bytes=VMEM_LIMIT_BYTES)

    pos = np.arange(S, dtype=np.float64)
    freqs = 1.0 / (ROPE_BASE ** (np.arange(0, RET_HD, 2, dtype=np.float64) / RET_HD))
    ang = pos[:, None] * freqs[None, :]
    cos_t = jnp.asarray(np.tile(np.cos(ang), (1, LANES // (RET_HD // 2))), F32)
    sin_half = np.sin(ang)
    sin_t = jnp.asarray(np.tile(np.concatenate([-sin_half, sin_half], axis=-1), (1, HEADS_PER_TILE)), F32)
    grp = np.arange(LANES) // RET_HD
    same_group = (grp[:, None] == grp[None, :]).astype(np.float32)
    bdmask = jnp.asarray(same_group, F32)

    log_g = np.log(1.0 - 2.0 ** (-5.0 - np.arange(n_ret_heads, dtype=np.float64)))
    idx = np.arange(RET_CHUNK, dtype=np.float64)
    rel = idx[:, None] - idx[None, :]
    decay_in = np.where(rel[None] >= 0, np.exp(log_g[:, None, None] * np.maximum(rel, 0.0)[None]), 0.0)
    dmat = jnp.asarray(decay_in.reshape(n_tiles, HEADS_PER_TILE * RET_CHUNK, RET_CHUNK), F32)
    by_lane = lambda t: jnp.asarray(np.repeat(t, RET_HD, axis=0).T, F32)
    qdec = by_lane(np.exp(log_g[:, None] * (idx[None, :] + 1.0)))
    kdec = by_lane(np.exp(log_g[:, None] * (RET_CHUNK - 1.0 - idx[None, :])))
    cdec = by_lane(np.exp(log_g * RET_CHUNK)[:, None])

    for l in range(depth):
        row = lambda b, i: (b, i, 0)
        col = lambda b, i: (b, 0, i)
        tab = lambda b, i: (i, 0)
        act = jax.ShapeDtypeStruct((B, S, width), BF16)
        act_t = jax.ShapeDtypeStruct((B, width, S), BF16)
        tile_bf16 = pltpu.VMEM((ROW_TILE, width), BF16)
        n_steps = B * (S // ROW_TILE)
        later = [w_out[l], w_gate[l], w_up[l], w_down[l]]
        later = [w.reshape(n_steps, w.shape[0] // n_steps, w.shape[1]) for w in later]
        step = lambda b, i: (b * (S // ROW_TILE) + i, 0, 0)
        cast_specs = [pl.BlockSpec((1,) + w.shape[1:], step) for w in later]
        ret, dqt, dk, dvt, wo_b, wg_b, wu_b, wd_b = pl.pallas_call(
            functools.partial(_inproj_retention_kernel, width=width, kv_block=ATT_BLOCK,
                              n_cast=len(later)),
            grid=(B, S // ROW_TILE),
            in_specs=[
                pl.BlockSpec((1, ROW_TILE, D), row),
                _resident((1, D)),
                _resident((D, in_width)),
                pl.BlockSpec((ROW_TILE, LANES), tab),
                pl.BlockSpec((ROW_TILE, LANES), tab),
                _resident((1, LANES)),
                _resident((1, LANES)),
                _resident(dmat.shape), _resident(qdec.shape), _resident(kdec.shape),
                _resident(cdec.shape), _resident(bdmask.shape), _resident((1, width)),
            ] + cast_specs,
            out_specs=[
                pl.BlockSpec((1, ROW_TILE, width), row),
                pl.BlockSpec((1, width, ROW_TILE), col),
                pl.BlockSpec((1, ROW_TILE, width), row),
                pl.BlockSpec((1, ROW_TILE // ATT_BLOCK, width, ATT_BLOCK), lambda b, i: (b, i, 0, 0)),
            ] + cast_specs,
            out_shape=[act, act_t, act, jax.ShapeDtypeStruct((B, nk, width, ATT_BLOCK), BF16)]
                      + [jax.ShapeDtypeStruct(w.shape, BF16) for w in later],
            scratch_shapes=[pltpu.VMEM((D, in_width), BF16),
                            pltpu.VMEM((n_tiles, LANES, LANES), F32),
                            tile_bf16, tile_bf16, tile_bf16, tile_bf16,
                            pltpu.VMEM((ROW_TILE, width), F32),
                            pltpu.VMEM((ROW_TILE + SPARE_ROWS, D), BF16)],
            compiler_params=params(dimension_semantics=("arbitrary", "arbitrary")),
            name="inproj_retention",
        )(x, norm1_g[l][None, :], w_in[l], cos_t, sin_t,
          jnp.tile(diff_q_norm_g[l], LANES // DIFF_HD)[None, :],
          jnp.tile(diff_k_norm_g[l], LANES // DIFF_HD)[None, :],
          dmat, qdec, kdec, cdec, bdmask, ret_norm_g[l].reshape(1, width), *later)
        wo_b, wg_b, wu_b, wd_b = (b.reshape(w.shape) for b, w in
                                  zip((wo_b, wg_b, wu_b, wd_b), (w_out[l], w_gate[l], w_up[l], w_down[l])))

        lam_spec = _resident((1, DIFF_HD))
        dif = pl.pallas_call(
            functools.partial(_diffattn_kernel, qblock=ATT_QBLOCK, block=ATT_BLOCK, strip=ATT_STRIP,
                              lead=ATT_LEAD),
            grid=(B, n_diff_heads, S // ATT_QBLOCK),
            in_specs=[
                pl.BlockSpec((1, DIFF_VD, ATT_QBLOCK), lambda b, h, i: (b, h, i)),
                pl.BlockSpec((1, S, DIFF_VD), lambda b, h, i: (b, 0, h)),
                pl.BlockSpec((1, nk, DIFF_VD, ATT_BLOCK), lambda b, h, i: (b, 0, h, 0)),
                lam_spec, lam_spec, lam_spec, lam_spec,
                _resident((1, DIFF_VD)),
            ],
            out_specs=pl.BlockSpec((1, ATT_QBLOCK, DIFF_VD), lambda b, h, i: (b, i, h)),
            out_shape=act,
            scratch_shapes=[pltpu.VMEM((ATT_RING, ATT_BLOCK, ATT_STRIP), F32),
                            pltpu.VMEM((ATT_RING, 1, ATT_STRIP), F32),
                            pltpu.VMEM((ATT_STRIP, ATT_STRIP), F32),
                            pltpu.VMEM((2, 1, ATT_QBLOCK), F32),
                            pltpu.VMEM((2, 1, ATT_QBLOCK), F32),
                            pltpu.VMEM((2, DIFF_VD, ATT_QBLOCK), F32)],
            compiler_params=params(dimension_semantics=("parallel", "parallel", "arbitrary")),
            name="diffattn",
        )(dqt, dk, dvt, lambda_q1[l][None, :], lambda_k1[l][None, :], lambda_q2[l][None, :],
          lambda_k2[l][None, :], diff_subln_g[l][None, :])

        tile = pl.BlockSpec((1, ROW_TILE, D), row)
        half = pl.BlockSpec((1, ROW_TILE, width), row)
        x = pl.pallas_call(
            functools.partial(_out_ffn_kernel, ret_width=width, ff_chunks=_ff_chunks(d_ff)),
            grid=(B, S // ROW_TILE),
            in_specs=[tile, half, half,
                      _resident((2 * width, D)), _resident((1, D)),
                      _resident((D, d_ff)), _resident((D, d_ff)), _resident((d_ff, D))],
            out_specs=tile,
            out_shape=jax.ShapeDtypeStruct((B, S, D), F32),
            compiler_params=params(dimension_semantics=("parallel", "parallel")),
            name="out_ffn",
        )(x, ret, dif, wo_b, norm2_g[l][None, :], wg_b, wu_b, wd_b)
    return x
```

```python
import functools
import math

import jax
import jax.numpy as jnp
import numpy as np
from jax import lax
from jax.experimental import pallas as pl
from jax.experimental.pallas import tpu as pltpu

F32 = jnp.float32
BF16 = jnp.bfloat16

LANES = 128
VMEM_LIMIT_BYTES = 56 * 1024 * 1024

RET_HD = 64
DIFF_HD = 64
DIFF_VD = 2 * DIFF_HD
HEADS_PER_TILE = LANES // RET_HD
RET_CHUNK = 128
ROPE_BASE = 10000.0
NORM_EPS = 1e-6
LAMBDA_INIT = 0.8 - 0.6 * math.exp(-0.3 * 0)
MASK_VALUE = -1e30
LOG2E = 1.0 / math.log(2.0)

ROW_TILE = 512
ATT_QBLOCK = 4096
ATT_BLOCK = 512
ATT_STRIP = 256
ATT_RING = 16
ATT_LEAD = 4


def _dot(a, b):
    return jnp.dot(a, b, preferred_element_type=F32)


def _dot_nt(a, b):
    return lax.dot_general(a, b, (((1,), (1,)), ((), ())), preferred_element_type=F32)


def _dot_tn(a, b):
    return lax.dot_general(a, b, (((0,), (0,)), ((), ())), preferred_element_type=F32)


def _half_sums(sq):
    first = lax.broadcasted_iota(jnp.int32, sq.shape, 1) < (LANES // 2)
    a = jnp.sum(jnp.where(first, sq, 0.0), axis=-1, keepdims=True)
    b = jnp.sum(jnp.where(first, 0.0, sq), axis=-1, keepdims=True)
    return jnp.where(first, a, b)


def _inproj_retention_kernel(x_ref, g1_ref, w32_ref, cos_ref, sin_ref, gq_ref, gk_ref,
                             dmat_ref, qdec_ref, kdec_ref, cdec_ref, bdmask_ref, gain_ref,
                             *rest, width, kv_block, n_cast):
    cast_in, rest = rest[:n_cast], rest[n_cast:]
    (ret_ref, dqt_ref, dk_ref, dvt_ref), rest = rest[:4], rest[4:]
    cast_out, rest = rest[:n_cast], rest[n_cast:]
    w_ref, state_ref, rq_ref, rk_ref, rv_ref, rg_ref, acc_ref = rest

    @pl.when((pl.program_id(0) == 0) & (pl.program_id(1) == 0))
    def _():
        w_ref[...] = w32_ref[...].astype(BF16)

    @pl.when(pl.program_id(1) == 0)
    def _():
        state_ref[...] = jnp.zeros_like(state_ref)

    for src_ref, dst_ref in zip(cast_in, cast_out):
        dst_ref[...] = src_ref[...].astype(BF16)

    x = x_ref[0]
    tm = x.shape[0]
    ms = jnp.mean(x * x, axis=-1, keepdims=True)
    h = (x * lax.rsqrt(ms + NORM_EPS) * g1_ref[...]).astype(BF16)

    def proj(j):
        return _dot(h, w_ref[:, j * width:(j + 1) * width])

    n_tiles = width // LANES
    tiles = [slice(t * LANES, (t + 1) * LANES) for t in range(n_tiles)]
    cos = cos_ref[...]
    sin = sin_ref[...]
    lane = lax.broadcasted_iota(jnp.int32, (tm, LANES), 1)
    first_half = (lane % RET_HD) < (RET_HD // 2)

    def rotary(t):
        partner = jnp.where(first_half, pltpu.roll(t, LANES - RET_HD // 2, 1),
                            pltpu.roll(t, RET_HD // 2, 1))
        return t * cos + partner * sin

    rq = proj(0)
    for sl in tiles:
        rq_ref[:, sl] = rotary(rq[:, sl]).astype(BF16)
    rk = proj(1)
    for sl in tiles:
        rk_ref[:, sl] = (rotary(rk[:, sl]) * (RET_HD ** -0.5)).astype(BF16)
    rv_ref[...] = proj(2).astype(BF16)
    rg_ref[...] = proj(3).astype(BF16)

    c = RET_CHUNK
    n_chunks = tm // c
    head_a = lax.broadcasted_iota(jnp.int32, (c, LANES), 1) < RET_HD
    bdmask = bdmask_ref[...]

    def retention(p):
        sl = tiles[p]
        inners, kvs = [], []
        for ci in range(n_chunks):
            rows = slice(ci * c, (ci + 1) * c)
            q2 = rq_ref[rows, sl]
            k2 = rk_ref[rows, sl]
            zero = jnp.zeros_like(q2)
            qs = jnp.concatenate([jnp.where(head_a, q2, zero), jnp.where(head_a, zero, q2)], axis=0)
            scores = _dot_nt(qs, k2)
            inners.append((scores * dmat_ref[p]).astype(BF16))
            kd = (k2.astype(F32) * kdec_ref[:, sl]).astype(BF16)
            kvs.append(_dot_tn(kd, rv_ref[rows, sl]) * bdmask)
        state = state_ref[p]
        for ci in range(n_chunks):
            rows = slice(ci * c, (ci + 1) * c)
            v2 = rv_ref[rows, sl]
            zero = jnp.zeros_like(v2)
            qd = (rq_ref[rows, sl].astype(F32) * qdec_ref[:, sl]).astype(BF16)
            lhs = jnp.concatenate([inners[ci][:c], inners[ci][c:], qd], axis=1)
            rhs = jnp.concatenate([jnp.where(head_a, v2, zero), jnp.where(head_a, zero, v2),
                                   state.astype(BF16)], axis=0)
            acc_ref[rows, sl] = _dot(lhs, rhs)
            state = state * cdec_ref[:, sl] + kvs[ci]
        state_ref[p] = state

        o = acc_ref[:, sl]
        ssq = _half_sums(o * o)
        y = o * lax.rsqrt(ssq * (1.0 / RET_HD) + NORM_EPS) * gain_ref[:, sl]
        g = rg_ref[:, sl].astype(F32)
        ret_ref[0, :, sl] = (y * (g / (1.0 + jnp.exp(-g)))).astype(BF16)

    def qk_norm(t, g):
        ssq = _half_sums(t * t)
        return t * lax.rsqrt(ssq * (1.0 / DIFF_HD) + NORM_EPS) * g

    def diff_q():
        dq = proj(4)
        q_scale = (DIFF_HD ** -0.5) * LOG2E
        for sl in tiles:
            qn = qk_norm(dq[:, sl], gq_ref[...]) * q_scale
            dqt_ref[0, sl, :] = qn.T.astype(BF16)

    def diff_k():
        dk = proj(5)
        for sl in tiles:
            dk_ref[0, :, sl] = qk_norm(dk[:, sl], gk_ref[...]).astype(BF16)

    def diff_v():
        dv = proj(6)
        for sl in tiles:
            vt = dv[:, sl].T.astype(BF16)
            for j in range(tm // kv_block):
                dvt_ref[0, j, sl, :] = vt[:, j * kv_block:(j + 1) * kv_block]

    others = [diff_q, diff_k, diff_v]
    for p in range(n_tiles):
        retention(p)
        if p < len(others):
            others[p]()
    for f in others[n_tiles:]:
        f()


def _diffattn_kernel(qt_ref, k_ref, vt_ref, lq1_ref, lk1_ref, lq2_ref, lk2_ref, gain_ref,
                     o_ref, s_ref, tri_ref, acc_ref, smax_ref, m_ref, l_ref, *, qblock, block, strip, lead):
    qi = pl.program_id(2)
    ring = s_ref.shape[0]
    n_strips = qblock // strip
    chains = [(mi, t) for mi in range(2) for t in reversed(range(n_strips))]
    n_chains = len(chains)
    per_q = qblock // block
    qt = qt_ref[0]
    row = lax.broadcasted_iota(jnp.int32, qt.shape, 0)
    zero = jnp.zeros_like(qt)
    q_maps = (jnp.where(row < DIFF_HD, qt, zero), jnp.where(row < DIFF_HD, zero, qt))

    kpos = lax.broadcasted_iota(jnp.int32, (strip, strip), 0)
    qpos = lax.broadcasted_iota(jnp.int32, (strip, strip), 1)
    tri_ref[...] = jnp.where(kpos <= qpos, 0.0, MASK_VALUE)

    def scores(kj, c, slot, nkeys=block, tri_rows=None):
        mi, t = chains[c]
        start = pl.multiple_of(kj * block, block)
        k = k_ref[0, pl.ds(start, nkeys), :]
        sc = _dot(k, q_maps[mi][:, t * strip:(t + 1) * strip])
        if tri_rows is not None:
            own = sc[tri_rows:] + tri_ref[...]
            sc = own if tri_rows == 0 else jnp.concatenate([sc[:tri_rows], own], axis=0)
        s_ref[slot, :nkeys, :] = sc
        smax_ref[slot] = jnp.max(sc, axis=0, keepdims=True)

    def softmax_pv(kj, c, slot, nkeys=block):
        mi, t = chains[c]
        cols = slice(t * strip, (t + 1) * strip)
        s = s_ref[slot, :nkeys, :]
        m_old = m_ref[mi, :, cols]
        m_new = jnp.maximum(m_old, smax_ref[slot])
        alpha = jnp.exp2(m_old - m_new)
        p = jnp.exp2(s - m_new)
        l_ref[mi, :, cols] = alpha * l_ref[mi, :, cols] + jnp.sum(p, axis=0, keepdims=True)
        acc_ref[mi, :, cols] = (alpha * acc_ref[mi, :, cols]
                                + _dot(vt_ref[0, kj, :, :nkeys], p.astype(BF16)))
        m_ref[mi, :, cols] = m_new

    for c in range(lead):
        scores(0, c, c)
    m_ref[...] = jnp.full_like(m_ref, MASK_VALUE)
    l_ref[...] = jnp.zeros_like(l_ref)
    acc_ref[...] = jnp.zeros_like(acc_ref)

    trip = 2 * n_chains

    def body(t, carry):
        for j in range(trip):
            nxt = j + lead
            scores(2 * t + nxt // n_chains, nxt % n_chains, nxt % ring)
            softmax_pv(2 * t + j // n_chains, j % n_chains, j % ring)
        return carry

    n_full = qi * per_q
    lax.fori_loop(0, n_full // 2, body, 0)

    tail = []
    for d in range(per_q):
        for c, (mi, t) in enumerate(chains):
            nkeys = min(block, (t + 1) * strip - d * block)
            if nkeys > 0:
                own = t * strip - d * block
                tail.append((d, c, nkeys, own if own < nkeys else None))
    lam = (jnp.exp(jnp.sum(lq1_ref[...] * lk1_ref[...], axis=-1, keepdims=True))
           - jnp.exp(jnp.sum(lq2_ref[...] * lk2_ref[...], axis=-1, keepdims=True))
           + LAMBDA_INIT)

    def finish(t):
        cols = slice(t * strip, (t + 1) * strip)
        o = (acc_ref[0, :, cols] / l_ref[0, :, cols]
             - lam * (acc_ref[1, :, cols] / l_ref[1, :, cols]))
        ms = jnp.mean(o * o, axis=0, keepdims=True)
        o = o * lax.rsqrt(ms + NORM_EPS)
        o_ref[0, cols, :] = (o.T * gain_ref[...] * (1.0 - LAMBDA_INIT)).astype(BF16)

    last_use = {chains[c][1]: i for i, (_, c, _, _) in enumerate(tail)}
    for i, (d, c, nkeys, tri_rows) in enumerate(tail):
        if i < lead:
            assert d == 0 and c == i and nkeys == block and tri_rows is None
        else:
            assert tri_rows is None or (tri_rows >= 0 and tri_rows + strip == nkeys)
        if i + lead < len(tail):
            d2, c2, nkeys2, tri2 = tail[i + lead]
            scores(n_full + d2, c2, (i + lead) % ring, nkeys2, tri2)
        softmax_pv(n_full + d, c, i % ring, nkeys)
        if i == last_use[chains[c][1]]:
            finish(chains[c][1])


def _out_ffn_kernel(x_ref, ret_ref, dif_ref, wo_ref, g2_ref, wg_ref, wu_ref, wd_ref, o_ref, *,
                    ret_width, ff_chunks):
    x1 = (x_ref[0] + _dot(ret_ref[0], wo_ref[:ret_width, :])
          + _dot(dif_ref[0], wo_ref[ret_width:, :]))
    ms = jnp.mean(x1 * x1, axis=-1, keepdims=True)
    h = (x1 * lax.rsqrt(ms + NORM_EPS) * g2_ref[...]).astype(BF16)
    o_ref[0] = x1
    ff = None
    for lo, hi in ff_chunks:
        g = _dot(h, wg_ref[:, lo:hi])
        u = _dot(h, wu_ref[:, lo:hi])
        a = (g / (1.0 + jnp.exp(-g)) * u).astype(BF16)
        d = _dot(a, wd_ref[lo:hi, :])
        ff = d if ff is None else ff + d
    o_ref[0] += ff


def _resident(shape):
    return pl.BlockSpec(shape, lambda *_: (0,) * len(shape), pipeline_mode=pl.Buffered(1))


def _ff_chunks(d_ff, mxu_cols=256, target=1536):
    chunks, lo = [], 0
    while lo < d_ff:
        hi = min(d_ff, lo + target)
        if d_ff - hi and (d_ff - hi) < mxu_cols:
            hi = d_ff
        chunks.append((lo, hi))
        lo = hi
    return tuple(chunks)


def kernel(x, norm1_g, w_in, ret_norm_g, diff_q_norm_g, diff_k_norm_g, lambda_q1, lambda_k1,
           lambda_q2, lambda_k2, diff_subln_g, w_out, norm2_g, w_gate, w_up, w_down):
    B, S, D = x.shape
    depth = w_in.shape[0]
    n_ret_heads = ret_norm_g.shape[1]
    ret_width = n_ret_heads * RET_HD
    in_width = w_in.shape[2]
    width = ret_width
    assert in_width == 7 * width and w_out.shape[1] == 2 * width
    assert width % LANES == 0 and S % ROW_TILE == 0 and S % ATT_BLOCK == 0
    assert ROW_TILE % ATT_BLOCK == 0 and ROW_TILE % RET_CHUNK == 0
    assert all(w.shape[1] % (B * (S // ROW_TILE)) == 0 for w in (w_out, w_gate, w_up, w_down))
    assert S % ATT_QBLOCK == 0 and (ATT_QBLOCK // ATT_BLOCK) % 2 == 0 and ATT_QBLOCK % ATT_STRIP == 0
    assert (4 * ATT_QBLOCK // ATT_STRIP) % ATT_RING == 0 and ATT_LEAD < ATT_RING
    assert ATT_LEAD <= 2 * ATT_QBLOCK // ATT_STRIP
    n_diff_heads = width // DIFF_VD
    n_tiles = width // LANES
    d_ff = w_gate.shape[2]
    nk = S // ATT_BLOCK
    params = functools.partial(pltpu.CompilerParams, vmem_limit_bytes=VMEM_LIMIT_BYTES)

    pos = np.arange(S, dtype=np.float64)
    freqs = 1.0 / (ROPE_BASE ** (np.arange(0, RET_HD, 2, dtype=np.float64) / RET_HD))
    ang = pos[:, None] * freqs[None, :]
    cos_t = jnp.asarray(np.tile(np.cos(ang), (1, LANES // (RET_HD // 2))), F32)
    sin_half = np.sin(ang)
    sin_t = jnp.asarray(np.tile(np.concatenate([-sin_half, sin_half], axis=-1), (1, HEADS_PER_TILE)), F32)
    grp = np.arange(LANES) // RET_HD
    same_group = (grp[:, None] == grp[None, :]).astype(np.float32)
    bdmask = jnp.asarray(same_group, F32)

    log_g = np.log(1.0 - 2.0 ** (-5.0 - np.arange(n_ret_heads, dtype=np.float64)))
    idx = np.arange(RET_CHUNK, dtype=np.float64)
    rel = idx[:, None] - idx[None, :]
    decay_in = np.where(rel[None] >= 0, np.exp(log_g[:, None, None] * np.maximum(rel, 0.0)[None]), 0.0)
    dmat = jnp.asarray(decay_in.reshape(n_tiles, HEADS_PER_TILE * RET_CHUNK, RET_CHUNK), F32)
    by_lane = lambda t: jnp.asarray(np.repeat(t, RET_HD, axis=0).T, F32)
    qdec = by_lane(np.exp(log_g[:, None] * (idx[None, :] + 1.0)))
    kdec = by_lane(np.exp(log_g[:, None] * (RET_CHUNK - 1.0 - idx[None, :])))
    cdec = by_lane(np.exp(log_g * RET_CHUNK)[:, None])

    for l in range(depth):
        row = lambda b, i: (b, i, 0)
        col = lambda b, i: (b, 0, i)
        tab = lambda b, i: (i, 0)
        act = jax.ShapeDtypeStruct((B, S, width), BF16)
        act_t = jax.ShapeDtypeStruct((B, width, S), BF16)
        tile_bf16 = pltpu.VMEM((ROW_TILE, width), BF16)
        n_steps = B * (S // ROW_TILE)
        later = [w_out[l], w_gate[l], w_up[l], w_down[l]]
        later = [w.reshape(n_steps, w.shape[0] // n_steps, w.shape[1]) for w in later]
        step = lambda b, i: (b * (S // ROW_TILE) + i, 0, 0)
        cast_specs = [pl.BlockSpec((1,) + w.shape[1:], step) for w in later]
        ret, dqt, dk, dvt, wo_b, wg_b, wu_b, wd_b = pl.pallas_call(
            functools.partial(_inproj_retention_kernel, width=width, kv_block=ATT_BLOCK,
                              n_cast=len(later)),
            grid=(B, S // ROW_TILE),
            in_specs=[
                pl.BlockSpec((1, ROW_TILE, D), row),
                _resident((1, D)),
                _resident((D, in_width)),
                pl.BlockSpec((ROW_TILE, LANES), tab),
                pl.BlockSpec((ROW_TILE, LANES), tab),
                _resident((1, LANES)),
                _resident((1, LANES)),
                _resident(dmat.shape), _resident(qdec.shape), _resident(kdec.shape),
                _resident(cdec.shape), _resident(bdmask.shape), _resident((1, width)),
            ] + cast_specs,
            out_specs=[
                pl.BlockSpec((1, ROW_TILE, width), row),
                pl.BlockSpec((1, width, ROW_TILE), col),
                pl.BlockSpec((1, ROW_TILE, width), row),
                pl.BlockSpec((1, ROW_TILE // ATT_BLOCK, width, ATT_BLOCK), lambda b, i: (b, i, 0, 0)),
            ] + cast_specs,
            out_shape=[act, act_t, act, jax.ShapeDtypeStruct((B, nk, width, ATT_BLOCK), BF16)]
                      + [jax.ShapeDtypeStruct(w.shape, BF16) for w in later],
            scratch_shapes=[pltpu.VMEM((D, in_width), BF16),
                            pltpu.VMEM((n_tiles, LANES, LANES), F32),
                            tile_bf16, tile_bf16, tile_bf16, tile_bf16,
                            pltpu.VMEM((ROW_TILE, width), F32)],
            compiler_params=params(dimension_semantics=("arbitrary", "arbitrary")),
            name="inproj_retention",
        )(x, norm1_g[l][None, :], w_in[l], cos_t, sin_t,
          jnp.tile(diff_q_norm_g[l], LANES // DIFF_HD)[None, :],
          jnp.tile(diff_k_norm_g[l], LANES // DIFF_HD)[None, :],
          dmat, qdec, kdec, cdec, bdmask, ret_norm_g[l].reshape(1, width), *later)
        wo_b, wg_b, wu_b, wd_b = (b.reshape(w.shape) for b, w in
                                  zip((wo_b, wg_b, wu_b, wd_b), (w_out[l], w_gate[l], w_up[l], w_down[l])))

        lam_spec = _resident((1, DIFF_HD))
        dif = pl.pallas_call(
            functools.partial(_diffattn_kernel, qblock=ATT_QBLOCK, block=ATT_BLOCK, strip=ATT_STRIP,
                              lead=ATT_LEAD),
            grid=(B, n_diff_heads, S // ATT_QBLOCK),
            in_specs=[
                pl.BlockSpec((1, DIFF_VD, ATT_QBLOCK), lambda b, h, i: (b, h, i)),
                pl.BlockSpec((1, S, DIFF_VD), lambda b, h, i: (b, 0, h)),
                pl.BlockSpec((1, nk, DIFF_VD, ATT_BLOCK), lambda b, h, i: (b, 0, h, 0)),
                lam_spec, lam_spec, lam_spec, lam_spec,
                _resident((1, DIFF_VD)),
            ],
            out_specs=pl.BlockSpec((1, ATT_QBLOCK, DIFF_VD), lambda b, h, i: (b, i, h)),
            out_shape=act,
            scratch_shapes=[pltpu.VMEM((ATT_RING, ATT_BLOCK, ATT_STRIP), F32),
                            pltpu.VMEM((ATT_STRIP, ATT_STRIP), F32),
                            pltpu.VMEM((2, DIFF_VD, ATT_QBLOCK), F32),
                            pltpu.VMEM((ATT_RING, 1, ATT_STRIP), F32),
                            pltpu.VMEM((2, 1, ATT_QBLOCK), F32),
                            pltpu.VMEM((2, 1, ATT_QBLOCK), F32)],
            compiler_params=params(dimension_semantics=("parallel", "parallel", "arbitrary")),
            name="diffattn",
        )(dqt, dk, dvt, lambda_q1[l][None, :], lambda_k1[l][None, :], lambda_q2[l][None, :],
          lambda_k2[l][None, :], diff_subln_g[l][None, :])

        tile = pl.BlockSpec((1, ROW_TILE, D), row)
        half = pl.BlockSpec((1, ROW_TILE, width), row)
        x = pl.pallas_call(
            functools.partial(_out_ffn_kernel, ret_width=width, ff_chunks=_ff_chunks(d_ff)),
            grid=(B, S // ROW_TILE),
            in_specs=[tile, half, half,
                      _resident((2 * width, D)), _resident((1, D)),
                      _resident((D, d_ff)), _resident((D, d_ff)), _resident((d_ff, D))],
            out_specs=tile,
            out_shape=jax.ShapeDtypeStruct((B, S, D), F32),
            compiler_params=params(dimension_semantics=("parallel", "parallel")),
            name="out_ffn",
        )(x, ret, dif, wo_b, norm2_g[l][None, :], wg_b, wu_b, wd_b)
    return x
```

```python
import functools
import math

import jax
import jax.numpy as jnp
import numpy as np
from jax import lax
from jax.experimental import pallas as pl
from jax.experimental.pallas import tpu as pltpu

F32 = jnp.float32
BF16 = jnp.bfloat16

LANES = 128
VMEM_LIMIT_BYTES = 56 * 1024 * 1024

RET_HD = 64
DIFF_HD = 64
DIFF_VD = 2 * DIFF_HD
HEADS_PER_TILE = LANES // RET_HD
RET_CHUNK = 128
ROPE_BASE = 10000.0
NORM_EPS = 1e-6
LAMBDA_INIT = 0.8 - 0.6 * math.exp(-0.3 * 0)
MASK_VALUE = -1e30
LOG2E = 1.0 / math.log(2.0)

ROW_TILE = 512
ATT_QBLOCK = 4096
ATT_BLOCK = 512
ATT_STRIP = 256
ATT_RING = 16
ATT_LEAD = 4


def _dot(a, b):
    return jnp.dot(a, b, preferred_element_type=F32)


def _dot_nt(a, b):
    return lax.dot_general(a, b, (((1,), (1,)), ((), ())), preferred_element_type=F32)


def _dot_tn(a, b):
    return lax.dot_general(a, b, (((0,), (0,)), ((), ())), preferred_element_type=F32)


def _half_sums(sq):
    first = lax.broadcasted_iota(jnp.int32, sq.shape, 1) < (LANES // 2)
    a = jnp.sum(jnp.where(first, sq, 0.0), axis=-1, keepdims=True)
    b = jnp.sum(jnp.where(first, 0.0, sq), axis=-1, keepdims=True)
    return jnp.where(first, a, b)


def _inproj_retention_kernel(x_ref, g1_ref, w32_ref, cos_ref, sin_ref, gq_ref, gk_ref,
                             dmat_ref, qdec_ref, kdec_ref, cdec_ref, bdmask_ref, gain_ref,
                             *rest, width, kv_block, n_cast):
    cast_in, rest = rest[:n_cast], rest[n_cast:]
    (ret_ref, dqt_ref, dk_ref, dvt_ref), rest = rest[:4], rest[4:]
    cast_out, rest = rest[:n_cast], rest[n_cast:]
    w_ref, state_ref, rq_ref, rk_ref, rv_ref, rg_ref, acc_ref = rest

    @pl.when((pl.program_id(0) == 0) & (pl.program_id(1) == 0))
    def _():
        w_ref[...] = w32_ref[...].astype(BF16)

    @pl.when(pl.program_id(1) == 0)
    def _():
        state_ref[...] = jnp.zeros_like(state_ref)

    for src_ref, dst_ref in zip(cast_in, cast_out):
        dst_ref[...] = src_ref[...].astype(BF16)

    x = x_ref[0]
    tm = x.shape[0]
    ms = jnp.mean(x * x, axis=-1, keepdims=True)
    h = (x * lax.rsqrt(ms + NORM_EPS) * g1_ref[...]).astype(BF16)

    def proj(j):
        return _dot(h, w_ref[:, j * width:(j + 1) * width])

    n_tiles = width // LANES
    tiles = [slice(t * LANES, (t + 1) * LANES) for t in range(n_tiles)]
    cos = cos_ref[...]
    sin = sin_ref[...]
    lane = lax.broadcasted_iota(jnp.int32, (tm, LANES), 1)
    first_half = (lane % RET_HD) < (RET_HD // 2)

    def rotary(t):
        partner = jnp.where(first_half, pltpu.roll(t, LANES - RET_HD // 2, 1),
                            pltpu.roll(t, RET_HD // 2, 1))
        return t * cos + partner * sin

    rq = proj(0)
    for sl in tiles:
        rq_ref[:, sl] = rotary(rq[:, sl]).astype(BF16)
    rk = proj(1)
    for sl in tiles:
        rk_ref[:, sl] = (rotary(rk[:, sl]) * (RET_HD ** -0.5)).astype(BF16)
    rv_ref[...] = proj(2).astype(BF16)
    rg_ref[...] = proj(3).astype(BF16)

    c = RET_CHUNK
    n_chunks = tm // c
    head_a = lax.broadcasted_iota(jnp.int32, (c, LANES), 1) < RET_HD
    bdmask = bdmask_ref[...]

    def retention(p):
        sl = tiles[p]
        inners, kvs = [], []
        for ci in range(n_chunks):
            rows = slice(ci * c, (ci + 1) * c)
            q2 = rq_ref[rows, sl]
            k2 = rk_ref[rows, sl]
            zero = jnp.zeros_like(q2)
            qs = jnp.concatenate([jnp.where(head_a, q2, zero), jnp.where(head_a, zero, q2)], axis=0)
            scores = _dot_nt(qs, k2)
            inners.append((scores * dmat_ref[p]).astype(BF16))
            kd = (k2.astype(F32) * kdec_ref[:, sl]).astype(BF16)
            kvs.append(_dot_tn(kd, rv_ref[rows, sl]) * bdmask)
        state = state_ref[p]
        for ci in range(n_chunks):
            rows = slice(ci * c, (ci + 1) * c)
            v2 = rv_ref[rows, sl]
            zero = jnp.zeros_like(v2)
            qd = (rq_ref[rows, sl].astype(F32) * qdec_ref[:, sl]).astype(BF16)
            lhs = jnp.concatenate([inners[ci][:c], inners[ci][c:], qd], axis=1)
            rhs = jnp.concatenate([jnp.where(head_a, v2, zero), jnp.where(head_a, zero, v2),
                                   state.astype(BF16)], axis=0)
            acc_ref[rows, sl] = _dot(lhs, rhs)
            state = state * cdec_ref[:, sl] + kvs[ci]
        state_ref[p] = state

        o = acc_ref[:, sl]
        ssq = _half_sums(o * o)
        y = o * lax.rsqrt(ssq * (1.0 / RET_HD) + NORM_EPS) * gain_ref[:, sl]
        g = rg_ref[:, sl].astype(F32)
        ret_ref[0, :, sl] = (y * (g / (1.0 + jnp.exp(-g)))).astype(BF16)

    def qk_norm(t, g):
        ssq = _half_sums(t * t)
        return t * lax.rsqrt(ssq * (1.0 / DIFF_HD) + NORM_EPS) * g

    def diff_q():
        dq = proj(4)
        q_scale = (DIFF_HD ** -0.5) * LOG2E
        for sl in tiles:
            qn = qk_norm(dq[:, sl], gq_ref[...]) * q_scale
            dqt_ref[0, sl, :] = qn.T.astype(BF16)

    def diff_k():
        dk = proj(5)
        for sl in tiles:
            dk_ref[0, :, sl] = qk_norm(dk[:, sl], gk_ref[...]).astype(BF16)

    def diff_v():
        dv = proj(6)
        for sl in tiles:
            vt = dv[:, sl].T.astype(BF16)
            for j in range(tm // kv_block):
                dvt_ref[0, j, sl, :] = vt[:, j * kv_block:(j + 1) * kv_block]

    others = [diff_q, diff_k, diff_v]
    for p in range(n_tiles):
        retention(p)
        if p < len(others):
            others[p]()
    for f in others[n_tiles:]:
        f()


def _diffattn_kernel(qt_ref, k_ref, vt_ref, lq1_ref, lk1_ref, lq2_ref, lk2_ref, gain_ref,
                     o_ref, s_ref, smax_ref, tri_ref, m_ref, l_ref, acc_ref, *, qblock, block, strip, lead):
    qi = pl.program_id(2)
    ring = s_ref.shape[0]
    n_strips = qblock // strip
    chains = [(mi, t) for mi in range(2) for t in reversed(range(n_strips))]
    n_chains = len(chains)
    per_q = qblock // block
    qt = qt_ref[0]
    row = lax.broadcasted_iota(jnp.int32, qt.shape, 0)
    zero = jnp.zeros_like(qt)
    q_maps = (jnp.where(row < DIFF_HD, qt, zero), jnp.where(row < DIFF_HD, zero, qt))

    kpos = lax.broadcasted_iota(jnp.int32, (strip, strip), 0)
    qpos = lax.broadcasted_iota(jnp.int32, (strip, strip), 1)
    tri_ref[...] = jnp.where(kpos <= qpos, 0.0, MASK_VALUE)

    def scores(kj, c, slot, nkeys=block, tri_rows=None):
        mi, t = chains[c]
        start = pl.multiple_of(kj * block, block)
        k = k_ref[0, pl.ds(start, nkeys), :]
        sc = _dot(k, q_maps[mi][:, t * strip:(t + 1) * strip])
        if tri_rows is not None:
            own = sc[tri_rows:] + tri_ref[...]
            sc = own if tri_rows == 0 else jnp.concatenate([sc[:tri_rows], own], axis=0)
        s_ref[slot, :nkeys, :] = sc
        smax_ref[slot] = jnp.max(sc, axis=0, keepdims=True)

    def softmax_pv(kj, c, slot, nkeys=block):
        mi, t = chains[c]
        cols = slice(t * strip, (t + 1) * strip)
        s = s_ref[slot, :nkeys, :]
        m_old = m_ref[mi, :, cols]
        m_new = jnp.maximum(m_old, smax_ref[slot])
        alpha = jnp.exp2(m_old - m_new)
        p = jnp.exp2(s - m_new)
        l_ref[mi, :, cols] = alpha * l_ref[mi, :, cols] + jnp.sum(p, axis=0, keepdims=True)
        acc_ref[mi, :, cols] = (alpha * acc_ref[mi, :, cols]
                                + _dot(vt_ref[0, kj, :, :nkeys], p.astype(BF16)))
        m_ref[mi, :, cols] = m_new

    for c in range(lead):
        scores(0, c, c)
    m_ref[...] = jnp.full_like(m_ref, MASK_VALUE)
    l_ref[...] = jnp.zeros_like(l_ref)
    acc_ref[...] = jnp.zeros_like(acc_ref)

    trip = 2 * n_chains

    def body(t, carry):
        for j in range(trip):
            nxt = j + lead
            scores(2 * t + nxt // n_chains, nxt % n_chains, nxt % ring)
            softmax_pv(2 * t + j // n_chains, j % n_chains, j % ring)
        return carry

    n_full = qi * per_q
    lax.fori_loop(0, n_full // 2, body, 0)

    tail = []
    for d in range(per_q):
        for c, (mi, t) in enumerate(chains):
            nkeys = min(block, (t + 1) * strip - d * block)
            if nkeys > 0:
                own = t * strip - d * block
                tail.append((d, c, nkeys, own if own < nkeys else None))
    lam = (jnp.exp(jnp.sum(lq1_ref[...] * lk1_ref[...], axis=-1, keepdims=True))
           - jnp.exp(jnp.sum(lq2_ref[...] * lk2_ref[...], axis=-1, keepdims=True))
           + LAMBDA_INIT)

    def finish(t):
        cols = slice(t * strip, (t + 1) * strip)
        o = (acc_ref[0, :, cols] / l_ref[0, :, cols]
             - lam * (acc_ref[1, :, cols] / l_ref[1, :, cols]))
        ms = jnp.mean(o * o, axis=0, keepdims=True)
        o = o * lax.rsqrt(ms + NORM_EPS)
        o_ref[0, cols, :] = (o.T * gain_ref[...] * (1.0 - LAMBDA_INIT)).astype(BF16)

    last_use = {chains[c][1]: i for i, (_, c, _, _) in enumerate(tail)}
    for i, (d, c, nkeys, tri_rows) in enumerate(tail):
        if i < lead:
            assert d == 0 and c == i and nkeys == block and tri_rows is None
        else:
            assert tri_rows is None or (tri_rows >= 0 and tri_rows + strip == nkeys)
        if i + lead < len(tail):
            d2, c2, nkeys2, tri2 = tail[i + lead]
            scores(n_full + d2, c2, (i + lead) % ring, nkeys2, tri2)
        softmax_pv(n_full + d, c, i % ring, nkeys)
        if i == last_use[chains[c][1]]:
            finish(chains[c][1])


def _out_ffn_kernel(x_ref, ret_ref, dif_ref, wo_ref, g2_ref, wg_ref, wu_ref, wd_ref, o_ref, *,
                    ret_width, ff_chunks):
    del ret_width
    mix = jnp.concatenate([ret_ref[0], dif_ref[0]], axis=1)
    x1 = x_ref[0] + _dot(mix, wo_ref[...])
    ms = jnp.mean(x1 * x1, axis=-1, keepdims=True)
    h = (x1 * lax.rsqrt(ms + NORM_EPS) * g2_ref[...]).astype(BF16)
    o_ref[0] = x1
    ff = None
    for lo, hi in ff_chunks:
        g = _dot(h, wg_ref[:, lo:hi])
        u = _dot(h, wu_ref[:, lo:hi])
        a = (g / (1.0 + jnp.exp(-g)) * u).astype(BF16)
        d = _dot(a, wd_ref[lo:hi, :])
        ff = d if ff is None else ff + d
    o_ref[0] += ff


def _resident(shape):
    return pl.BlockSpec(shape, lambda *_: (0,) * len(shape), pipeline_mode=pl.Buffered(1))


def _ff_chunks(d_ff, mxu_cols=256, target=1536):
    chunks, lo = [], 0
    while lo < d_ff:
        hi = min(d_ff, lo + target)
        if d_ff - hi and (d_ff - hi) < mxu_cols:
            hi = d_ff
        chunks.append((lo, hi))
        lo = hi
    return tuple(chunks)


def kernel(x, norm1_g, w_in, ret_norm_g, diff_q_norm_g, diff_k_norm_g, lambda_q1, lambda_k1,
           lambda_q2, lambda_k2, diff_subln_g, w_out, norm2_g, w_gate, w_up, w_down):
    B, S, D = x.shape
    depth = w_in.shape[0]
    n_ret_heads = ret_norm_g.shape[1]
    ret_width = n_ret_heads * RET_HD
    in_width = w_in.shape[2]
    width = ret_width
    assert in_width == 7 * width and w_out.shape[1] == 2 * width
    assert width % LANES == 0 and S % ROW_TILE == 0 and S % ATT_BLOCK == 0
    assert ROW_TILE % ATT_BLOCK == 0 and ROW_TILE % RET_CHUNK == 0
    assert all(w.shape[1] % (B * (S // ROW_TILE)) == 0 for w in (w_out, w_gate, w_up, w_down))
    assert S % ATT_QBLOCK == 0 and (ATT_QBLOCK // ATT_BLOCK) % 2 == 0 and ATT_QBLOCK % ATT_STRIP == 0
    assert (4 * ATT_QBLOCK // ATT_STRIP) % ATT_RING == 0 and ATT_LEAD < ATT_RING
    assert ATT_LEAD <= 2 * ATT_QBLOCK // ATT_STRIP
    n_diff_heads = width // DIFF_VD
    n_tiles = width // LANES
    d_ff = w_gate.shape[2]
    nk = S // ATT_BLOCK
    params = functools.partial(pltpu.CompilerParams, vmem_limit_bytes=VMEM_LIMIT_BYTES)

    pos = np.arange(S, dtype=np.float64)
    freqs = 1.0 / (ROPE_BASE ** (np.arange(0, RET_HD, 2, dtype=np.float64) / RET_HD))
    ang = pos[:, None] * freqs[None, :]
    cos_t = jnp.asarray(np.tile(np.cos(ang), (1, LANES // (RET_HD // 2))), F32)
    sin_half = np.sin(ang)
    sin_t = jnp.asarray(np.tile(np.concatenate([-sin_half, sin_half], axis=-1), (1, HEADS_PER_TILE)), F32)
    grp = np.arange(LANES) // RET_HD
    same_group = (grp[:, None] == grp[None, :]).astype(np.float32)
    bdmask = jnp.asarray(same_group, F32)

    log_g = np.log(1.0 - 2.0 ** (-5.0 - np.arange(n_ret_heads, dtype=np.float64)))
    idx = np.arange(RET_CHUNK, dtype=np.float64)
    rel = idx[:, None] - idx[None, :]
    decay_in = np.where(rel[None] >= 0, np.exp(log_g[:, None, None] * np.maximum(rel, 0.0)[None]), 0.0)
    dmat = jnp.asarray(decay_in.reshape(n_tiles, HEADS_PER_TILE * RET_CHUNK, RET_CHUNK), F32)
    by_lane = lambda t: jnp.asarray(np.repeat(t, RET_HD, axis=0).T, F32)
    qdec = by_lane(np.exp(log_g[:, None] * (idx[None, :] + 1.0)))
    kdec = by_lane(np.exp(log_g[:, None] * (RET_CHUNK - 1.0 - idx[None, :])))
    cdec = by_lane(np.exp(log_g * RET_CHUNK)[:, None])

    for l in range(depth):
        row = lambda b, i: (b, i, 0)
        col = lambda b, i: (b, 0, i)
        tab = lambda b, i: (i, 0)
        act = jax.ShapeDtypeStruct((B, S, width), BF16)
        act_t = jax.ShapeDtypeStruct((B, width, S), BF16)
        tile_bf16 = pltpu.VMEM((ROW_TILE, width), BF16)
        n_steps = B * (S // ROW_TILE)
        later = [w_out[l], w_gate[l], w_up[l], w_down[l]]
        later = [w.reshape(n_steps, w.shape[0] // n_steps, w.shape[1]) for w in later]
        step = lambda b, i: (b * (S // ROW_TILE) + i, 0, 0)
        cast_specs = [pl.BlockSpec((1,) + w.shape[1:], step) for w in later]
        ret, dqt, dk, dvt, wo_b, wg_b, wu_b, wd_b = pl.pallas_call(
            functools.partial(_inproj_retention_kernel, width=width, kv_block=ATT_BLOCK,
                              n_cast=len(later)),
            grid=(B, S // ROW_TILE),
            in_specs=[
                pl.BlockSpec((1, ROW_TILE, D), row),
                _resident((1, D)),
                _resident((D, in_width)),
                pl.BlockSpec((ROW_TILE, LANES), tab),
                pl.BlockSpec((ROW_TILE, LANES), tab),
                _resident((1, LANES)),
                _resident((1, LANES)),
                _resident(dmat.shape), _resident(qdec.shape), _resident(kdec.shape),
                _resident(cdec.shape), _resident(bdmask.shape), _resident((1, width)),
            ] + cast_specs,
            out_specs=[
                pl.BlockSpec((1, ROW_TILE, width), row),
                pl.BlockSpec((1, width, ROW_TILE), col),
                pl.BlockSpec((1, ROW_TILE, width), row),
                pl.BlockSpec((1, ROW_TILE // ATT_BLOCK, width, ATT_BLOCK), lambda b, i: (b, i, 0, 0)),
            ] + cast_specs,
            out_shape=[act, act_t, act, jax.ShapeDtypeStruct((B, nk, width, ATT_BLOCK), BF16)]
                      + [jax.ShapeDtypeStruct(w.shape, BF16) for w in later],
            scratch_shapes=[pltpu.VMEM((D, in_width), BF16),
                            pltpu.VMEM((n_tiles, LANES, LANES), F32),
                            tile_bf16, tile_bf16, tile_bf16, tile_bf16,
                            pltpu.VMEM((ROW_TILE, width), F32)],
            compiler_params=params(dimension_semantics=("arbitrary", "arbitrary")),
            name="inproj_retention",
        )(x, norm1_g[l][None, :], w_in[l], cos_t, sin_t,
          jnp.tile(diff_q_norm_g[l], LANES // DIFF_HD)[None, :],
          jnp.tile(diff_k_norm_g[l], LANES // DIFF_HD)[None, :],
          dmat, qdec, kdec, cdec, bdmask, ret_norm_g[l].reshape(1, width), *later)
        wo_b, wg_b, wu_b, wd_b = (b.reshape(w.shape) for b, w in
                                  zip((wo_b, wg_b, wu_b, wd_b), (w_out[l], w_gate[l], w_up[l], w_down[l])))

        lam_spec = _resident((1, DIFF_HD))
        dif = pl.pallas_call(
            functools.partial(_diffattn_kernel, qblock=ATT_QBLOCK, block=ATT_BLOCK, strip=ATT_STRIP,
                              lead=ATT_LEAD),
            grid=(B, n_diff_heads, S // ATT_QBLOCK),
            in_specs=[
                pl.BlockSpec((1, DIFF_VD, ATT_QBLOCK), lambda b, h, i: (b, h, i)),
                pl.BlockSpec((1, S, DIFF_VD), lambda b, h, i: (b, 0, h)),
                pl.BlockSpec((1, nk, DIFF_VD, ATT_BLOCK), lambda b, h, i: (b, 0, h, 0)),
                lam_spec, lam_spec, lam_spec, lam_spec,
                _resident((1, DIFF_VD)),
            ],
            out_specs=pl.BlockSpec((1, ATT_QBLOCK, DIFF_VD), lambda b, h, i: (b, i, h)),
            out_shape=act,
            scratch_shapes=[pltpu.VMEM((ATT_RING, ATT_BLOCK, ATT_STRIP), F32),
                            pltpu.VMEM((ATT_RING, 1, ATT_STRIP), F32),
                            pltpu.VMEM((ATT_STRIP, ATT_STRIP), F32),
                            pltpu.VMEM((2, 1, ATT_QBLOCK), F32),
                            pltpu.VMEM((2, 1, ATT_QBLOCK), F32),
                            pltpu.VMEM((2, DIFF_VD, ATT_QBLOCK), F32)],
            compiler_params=params(dimension_semantics=("parallel", "parallel", "arbitrary")),
            name="diffattn",
        )(dqt, dk, dvt, lambda_q1[l][None, :], lambda_k1[l][None, :], lambda_q2[l][None, :],
          lambda_k2[l][None, :], diff_subln_g[l][None, :])

        tile = pl.BlockSpec((1, ROW_TILE, D), row)
        half = pl.BlockSpec((1, ROW_TILE, width), row)
        x = pl.pallas_call(
            functools.partial(_out_ffn_kernel, ret_width=width, ff_chunks=_ff_chunks(d_ff)),
            grid=(B, S // ROW_TILE),
            in_specs=[tile, half, half,
                      _resident((2 * width, D)), _resident((1, D)),
                      _resident((D, d_ff)), _resident((D, d_ff)), _resident((d_ff, D))],
            out_specs=tile,
            out_shape=jax.ShapeDtypeStruct((B, S, D), F32),
            compiler_params=params(dimension_semantics=("parallel", "parallel")),
            name="out_ffn",
        )(x, ret, dif, wo_b, norm2_g[l][None, :], wg_b, wu_b, wd_b)
    return x
```
